```python
import math
import jax, jax.numpy as jnp
from jax import lax
import numpy as np

D_MODEL = 1024
BATCH = 1
SEQ = 16384
DEPTH = 2

A_PATTERNS = ((128, 1), (512, 4), (2048, 16))
A_GROUPS = 3
A_HEADS = 8
A_HEAD_DIM = 64
A_WIDTH = A_HEADS * A_HEAD_DIM
A_BLOCK = 128
M_HEADS = 16
M_Q_LORA = 256
M_KV_LORA = 128
M_NOPE = 64
M_ROPE = 32
M_V = 64
M_WIDTH = M_HEADS * M_V
ROPE_THETA = 10000.0
Q_BLOCK = 128
REL_BUCKETS = 32
REL_MAX_DIST = 2048
N_BIAS_HEADS = A_GROUPS * A_HEADS
EPS = 1e-6
NEG_INF = -1e30

SPLIT_SIZES = (3 * A_GROUPS * A_WIDTH, A_WIDTH, M_Q_LORA, M_KV_LORA, M_ROPE, M_WIDTH, 2 * D_MODEL)
D_IN = sum(SPLIT_SIZES)
SPLIT_IDX = tuple(sum(SPLIT_SIZES[:i + 1]) for i in range(len(SPLIT_SIZES) - 1))

kernel_name = 'hybrid_dilated_mla_gated_block'


def rms_norm(x, g):
    xf = x.astype(jnp.float32)
    y = xf * lax.rsqrt(jnp.mean(xf * xf, axis=-1, keepdims=True) + EPS)
    return (y * g.astype(jnp.float32)).astype(x.dtype)


def t5_bucket(dist):
    exact = REL_BUCKETS // 2
    d = jnp.maximum(dist, 1).astype(jnp.float32)
    large = exact + (jnp.log(d / exact) / math.log(REL_MAX_DIST / exact) * (REL_BUCKETS - exact)).astype(jnp.int32)
    large = jnp.minimum(large, REL_BUCKETS - 1)
    return jnp.where(dist < exact, dist, large)


def apply_rope(t, cos, sin):
    half = t.shape[-1] // 2
    t1 = t[..., :half].astype(jnp.float32)
    t2 = t[..., half:].astype(jnp.float32)
    return jnp.concatenate([t1 * cos - t2 * sin, t1 * sin + t2 * cos], axis=-1).astype(t.dtype)


def dilated_attention_group(q, k, v, bias_tab, window, dilation):
    B, S, H, Dh = q.shape
    r = dilation
    n_rel = window // r
    L = -(-S // r)
    nb = -(-L // A_BLOCK)
    Lp = nb * A_BLOCK
    pad = Lp * r - S

    def to_blocks(t):
        t = jnp.pad(t, ((0, 0), (0, pad), (0, 0), (0, 0)))
        t = t.reshape(B, Lp, r, H, Dh).transpose(0, 2, 1, 3, 4)
        return t.reshape(B, r, nb, A_BLOCK, H, Dh)

    def with_prev(t):
        prev = jnp.concatenate([jnp.zeros_like(t[:, :, :1]), t[:, :, :-1]], axis=2)
        return jnp.concatenate([prev, t], axis=3)

    qb = to_blocks(q)
    kw = with_prev(to_blocks(k))
    vw = with_prev(to_blocks(v))
    s = jnp.einsum('bpnqhd,bpnkhd->bpnhqk', qb, kw).astype(jnp.float32) * (Dh ** -0.5)
    qi = jnp.arange(A_BLOCK)[:, None]
    ki = jnp.arange(2 * A_BLOCK)[None, :]
    j = qi + A_BLOCK - ki
    band = (j >= 0) & (j <= n_rel)
    bias = bias_tab[t5_bucket(jnp.maximum(j, 0) * r)].astype(jnp.float32).transpose(2, 0, 1)
    key_sub = jnp.arange(nb)[:, None, None] * A_BLOCK + ki[None] - A_BLOCK
    valid = band[None] & (key_sub >= 0)
    s = jnp.where(valid[None, None, :, None], s + bias, NEG_INF)
    m = jnp.max(s, axis=-1, keepdims=True)
    p = jnp.exp(s - m)
    l = jnp.sum(p, axis=-1, keepdims=True)
    o = jnp.einsum('bpnhqk,bpnkhd->bpnqhd', (p / l).astype(v.dtype), vw)
    lse = (m + jnp.log(l))[..., 0]
    o = o.reshape(B, r, Lp, H, Dh).transpose(0, 2, 1, 3, 4).reshape(B, Lp * r, H, Dh)[:, :S]
    lse = lse.transpose(0, 1, 2, 4, 3).reshape(B, r, Lp, H).transpose(0, 2, 1, 3).reshape(B, Lp * r, H)[:, :S]
    return o, lse


def dilated_mixer(a_qkv, rel_bias):
    B, S, _ = a_qkv.shape
    qkv = a_qkv.reshape(B, S, 3, A_GROUPS, A_HEADS, A_HEAD_DIM)
    outs, lses = [], []
    for g, (w, r) in enumerate(A_PATTERNS):
        o, lse = dilated_attention_group(qkv[:, :, 0, g], qkv[:, :, 1, g], qkv[:, :, 2, g],
                                         rel_bias[:, g * A_HEADS:(g + 1) * A_HEADS], w, r)
        outs.append(o)
        lses.append(lse)
    wts = jax.nn.softmax(jnp.stack(lses), axis=0)
    o = jnp.sum(jnp.stack(outs).astype(jnp.float32) * wts[..., None], axis=0)
    return o.reshape(B, S, A_WIDTH).astype(a_qkv.dtype)


def mla_mixer(c_q, c_kv, k_rope_raw, q_norm_g, w_uq, kv_norm_g, w_ukv, cos, sin):
    B, S, _ = c_q.shape
    q = (rms_norm(c_q, q_norm_g) @ w_uq).reshape(B, S, M_HEADS, M_NOPE + M_ROPE)
    q_nope = q[..., :M_NOPE]
    q_rope = apply_rope(q[..., M_NOPE:], cos[:, :, None], sin[:, :, None])
    kv = (rms_norm(c_kv, kv_norm_g) @ w_ukv).reshape(B, S, M_HEADS, M_NOPE + M_V)
    k_nope = kv[..., :M_NOPE]
    v = kv[..., M_NOPE:]
    k_rope = apply_rope(k_rope_raw, cos, sin)
    nb = S // Q_BLOCK
    scale = (M_NOPE + M_ROPE) ** -0.5
    k_pos = jnp.arange(S)

    def blocks(t):
        return t.reshape((B, nb, Q_BLOCK) + t.shape[2:]).swapaxes(0, 1)

    def attend(args):
        qn, qr, start = args
        s = (jnp.einsum('bqhd,bkhd->bhqk', qn, k_nope)
             + jnp.einsum('bqhd,bkd->bhqk', qr, k_rope)).astype(jnp.float32) * scale
        q_pos = start + jnp.arange(Q_BLOCK)
        s = jnp.where(k_pos[None, :] <= q_pos[:, None], s, NEG_INF)
        p = jax.nn.softmax(s, axis=-1)
        return jnp.einsum('bhqk,bkhd->bqhd', p.astype(v.dtype), v)

    o = lax.map(attend, (blocks(q_nope), blocks(q_rope), jnp.arange(nb) * Q_BLOCK))
    return o.swapaxes(0, 1).reshape(B, S, M_WIDTH)


def hybrid_layer(x, mod, norm_g, w_in, q_norm_g, w_uq, kv_norm_g, w_ukv, w_out_a, w_out_b, w_o, rel_bias, cos, sin):
    shift, scale, gate = jnp.split(mod, 3, axis=-1)
    h = rms_norm(x, norm_g) * (1 + scale[:, None]) + shift[:, None]
    a_qkv, a_z, m_cq, m_ckv, m_kr, m_z, merge = jnp.split(h @ w_in, SPLIT_IDX, axis=-1)
    y_a = dilated_mixer(a_qkv, rel_bias) * jax.nn.silu(a_z)
    y_m = mla_mixer(m_cq, m_ckv, m_kr, q_norm_g, w_uq, kv_norm_g, w_ukv, cos, sin) * jax.nn.silu(m_z)
    g_a, g_m = jnp.split(jax.nn.sigmoid(merge), 2, axis=-1)
    merged = g_a * (y_a @ w_out_a) + g_m * (y_m @ w_out_b)
    return x + gate[:, None] * (merged @ w_o)


def setup_inputs(seed: int = 0) -> dict:
    key = jax.random.key(seed)
    ks = jax.random.split(key, 16)
    n = jax.random.normal
    f32 = jnp.float32
    x = n(ks[0], (BATCH, SEQ, D_MODEL), f32)
    c = n(ks[1], (BATCH, D_MODEL), f32)
    positions = jnp.broadcast_to(jnp.arange(SEQ, dtype=jnp.int32)[None], (BATCH, SEQ))
    w_ada = n(ks[2], (DEPTH, D_MODEL, 3 * D_MODEL), f32) * (0.5 * D_MODEL ** -0.5)
    b_ada = n(ks[3], (DEPTH, 3 * D_MODEL), f32) * 0.01
    norm_g = 1.0 + 0.01 * n(ks[4], (DEPTH, D_MODEL), f32)
    w_in = n(ks[5], (DEPTH, D_MODEL, D_IN), f32) * D_MODEL ** -0.5
    q_norm_g = 1.0 + 0.01 * n(ks[6], (DEPTH, M_Q_LORA), f32)
    w_uq = n(ks[7], (DEPTH, M_Q_LORA, M_HEADS * (M_NOPE + M_ROPE)), f32) * M_Q_LORA ** -0.5
    kv_norm_g = 1.0 + 0.01 * n(ks[8], (DEPTH, M_KV_LORA), f32)
    w_ukv = n(ks[9], (DEPTH, M_KV_LORA, M_HEADS * (M_NOPE + M_V)), f32) * M_KV_LORA ** -0.5
    w_out_a = n(ks[10], (DEPTH, A_WIDTH, D_MODEL), f32) * A_WIDTH ** -0.5
    w_out_b = n(ks[11], (DEPTH, M_WIDTH, D_MODEL), f32) * M_WIDTH ** -0.5
    w_o = n(ks[12], (DEPTH, D_MODEL, D_MODEL), f32) * D_MODEL ** -0.5
    rel_bias = n(ks[13], (REL_BUCKETS, N_BIAS_HEADS), f32) * 0.5
    final_norm_g = 1.0 + 0.01 * n(ks[14], (D_MODEL,), f32)
    return {'x': x, 'c': c, 'positions': positions, 'w_ada': w_ada, 'b_ada': b_ada, 'norm_g': norm_g,
            'w_in': w_in, 'q_norm_g': q_norm_g, 'w_uq': w_uq, 'kv_norm_g': kv_norm_g, 'w_ukv': w_ukv,
            'w_out_a': w_out_a, 'w_out_b': w_out_b, 'w_o': w_o, 'rel_bias': rel_bias, 'final_norm_g': final_norm_g}


def reference(x, c, positions, w_ada, b_ada, norm_g, w_in, q_norm_g, w_uq, kv_norm_g, w_ukv,
              w_out_a, w_out_b, w_o, rel_bias, final_norm_g):
    inv_freq = 1.0 / (ROPE_THETA ** (jnp.arange(0, M_ROPE, 2, dtype=jnp.float32) / M_ROPE))
    ang = positions.astype(jnp.float32)[..., None] * inv_freq
    cos, sin = jnp.cos(ang), jnp.sin(ang)
    c_act = jax.nn.silu(c)
    for l in range(DEPTH):
        mod = c_act @ w_ada[l] + b_ada[l]
        x = hybrid_layer(x, mod, norm_g[l], w_in[l], q_norm_g[l], w_uq[l], kv_norm_g[l], w_ukv[l],
                         w_out_a[l], w_out_b[l], w_o[l], rel_bias, cos, sin)
    return rms_norm(x, final_norm_g)
```

```python
import functools
import math

import jax
import jax.numpy as jnp
from jax import lax
from jax.experimental import pallas as pl
from jax.experimental.pallas import tpu as pltpu

F32 = jnp.float32
BF16 = jnp.bfloat16

D_MODEL = 1024
A_PATTERNS = ((128, 1), (512, 4), (2048, 16))
A_GROUPS = 3
A_HEADS = 8
A_HEAD_DIM = 64
A_WIDTH = A_HEADS * A_HEAD_DIM
A_BLOCK = 128
A_QKV = 3 * A_GROUPS * A_WIDTH
M_HEADS = 16
M_Q_LORA = 256
M_KV_LORA = 128
M_NOPE = 64
M_ROPE = 32
M_V = 64
M_QK = M_NOPE + M_ROPE
M_QK_PAD = 128
M_WIDTH = M_HEADS * M_V
ROPE_THETA = 10000.0
REL_BUCKETS = 32
REL_MAX_DIST = 2048
EPS = 1e-6
NEG_INF = -1e30

ROW_TILE = 512
ATT_TILE = 512
VMEM_LIMIT = 56 * 1024 * 1024


def _cparams(*sem):
    return pltpu.CompilerParams(dimension_semantics=sem, vmem_limit_bytes=VMEM_LIMIT)


def _const_spec(shape):
    nd = len(shape)
    return pl.BlockSpec(shape, lambda *_: (0,) * nd, pipeline_mode=pl.Buffered(1))


def _nt_dot(a, b):
    return lax.dot_general(a, b, (((1,), (1,)), ((), ())), preferred_element_type=F32)


def _modulated_norm(x, g, scale, shift):
    y = x * lax.rsqrt(jnp.mean(x * x, axis=-1, keepdims=True) + EPS)
    return (y * g) * (1.0 + scale) + shift


def _mod_kernel(c_ref, w_ref, b_ref, o_ref):
    c = c_ref[...]
    c_act = c * (1.0 / (1.0 + jnp.exp(-c)))
    o_ref[0, 0] = jnp.sum(c_act * w_ref[0], axis=0, keepdims=True) + b_ref[0, 0]


def _modulation(c, w_ada, b_ada):
    depth = w_ada.shape[0]
    c_col = c.reshape(D_MODEL, 1)
    b4 = b_ada.reshape(depth, 3, 1, D_MODEL)
    return pl.pallas_call(
        _mod_kernel,
        grid=(depth, 3),
        in_specs=[pl.BlockSpec((D_MODEL, 1), lambda l, j: (0, 0)),
                  pl.BlockSpec((1, D_MODEL, D_MODEL), lambda l, j: (l, 0, j)),
                  pl.BlockSpec((1, 1, 1, D_MODEL), lambda l, j: (l, j, 0, 0))],
        out_specs=pl.BlockSpec((1, 1, 1, D_MODEL), lambda l, j: (l, j, 0, 0)),
        out_shape=jax.ShapeDtypeStruct((depth, 3, 1, D_MODEL), F32),
        compiler_params=_cparams("parallel", "parallel"),
        name="adaln_mod",
    )(c_col, w_ada, b4)


def _in_a_kernel(x_ref, g_ref, sc_ref, sh_ref, wa_ref, wg_ref, qkv_ref, az_ref, mg_ref):
    h = _modulated_norm(x_ref[...], g_ref[...], sc_ref[...], sh_ref[...]).astype(BF16)
    a = jnp.dot(h, wa_ref[...], preferred_element_type=F32)
    qkv_ref[...] = a[:, :A_QKV].astype(BF16)
    az_ref[...] = a[:, A_QKV:]
    mg_ref[...] = jnp.dot(h, wg_ref[...], preferred_element_type=F32)


def _in_proj_a(x, g, scale, shift, w_a, w_g):
    s = x.shape[0]
    tm = ROW_TILE
    row = lambda i: (i, 0)
    return pl.pallas_call(
        _in_a_kernel,
        grid=(s // tm,),
        in_specs=[pl.BlockSpec((tm, D_MODEL), row),
                  _const_spec((1, D_MODEL)), _const_spec((1, D_MODEL)), _const_spec((1, D_MODEL)),
                  _const_spec(w_a.shape), _const_spec(w_g.shape)],
        out_specs=[pl.BlockSpec((tm, A_QKV), row),
                   pl.BlockSpec((tm, A_WIDTH), row),
                   pl.BlockSpec((tm, 2 * D_MODEL), row)],
        out_shape=[jax.ShapeDtypeStruct((s, A_QKV), BF16),
                   jax.ShapeDtypeStruct((s, A_WIDTH), F32),
                   jax.ShapeDtypeStruct((s, 2 * D_MODEL), F32)],
        compiler_params=_cparams("parallel"),
        name="in_proj_a",
    )(x, g, scale, shift, w_a, w_g)


def _in_m_kernel(x_ref, g_ref, sc_ref, sh_ref, wc_ref, wzt_ref, qg_ref, kvg_ref, wuqt_ref, wk_ref, wvt_ref,
                 cost_ref, sint_ref, ck_ref, sk_ref, qt_ref, k_ref, vt_ref, mzt_ref):
    h = _modulated_norm(x_ref[...], g_ref[...], sc_ref[...], sh_ref[...]).astype(BF16)
    mzt_ref[...] = _nt_dot(wzt_ref[...], h)
    c = jnp.dot(h, wc_ref[...], preferred_element_type=F32)
    cq = c[:, :M_Q_LORA]
    ckv = c[:, M_Q_LORA:M_Q_LORA + M_KV_LORA]
    kr_a = c[:, M_Q_LORA + M_KV_LORA:M_Q_LORA + M_KV_LORA + M_QK_PAD]
    kr_b = c[:, M_Q_LORA + M_KV_LORA + M_QK_PAD:]
    cqn = (cq * lax.rsqrt(jnp.mean(cq * cq, axis=-1, keepdims=True) + EPS) * qg_ref[...]).astype(BF16)
    ckvn = (ckv * lax.rsqrt(jnp.mean(ckv * ckv, axis=-1, keepdims=True) + EPS) * kvg_ref[...]).astype(BF16)

    tm = cq.shape[0]
    half = M_ROPE // 2
    qt = _nt_dot(wuqt_ref[...], cqn).reshape(M_HEADS, M_QK_PAD, tm)
    cos = cost_ref[...][None]
    sin = sint_ref[...][None]
    t1 = qt[:, M_NOPE:M_NOPE + half]
    t2 = qt[:, M_NOPE + half:M_QK]
    qt_ref[:, :M_NOPE] = qt[:, :M_NOPE].astype(BF16)
    qt_ref[:, M_NOPE:M_NOPE + half] = (t1 * cos - t2 * sin).astype(BF16)
    qt_ref[:, M_NOPE + half:M_QK] = (t1 * sin + t2 * cos).astype(BF16)
    qt_ref[:, M_QK:] = qt[:, M_QK:].astype(BF16)

    kr = kr_a * ck_ref[...] + kr_b * sk_ref[...]
    k_all = jnp.dot(ckvn, wk_ref[...], preferred_element_type=F32)
    for hd in range(M_HEADS):
        k_ref[hd] = (k_all[:, hd * M_QK_PAD:(hd + 1) * M_QK_PAD] + kr).astype(BF16)

    vt = _nt_dot(wvt_ref[...], ckvn)
    vt_ref[:, 0] = vt.reshape(M_HEADS, M_V, tm).astype(BF16)


def _in_proj_m(x, g, scale, shift, w_c, w_zt, qg, kvg, w_uqt, w_k, w_vt, cos_t, sin_t, cos_k, sin_k):
    s = x.shape[0]
    tm = ATT_TILE
    row = lambda i: (i, 0)
    col = lambda i: (0, i)
    return pl.pallas_call(
        _in_m_kernel,
        grid=(s // tm,),
        in_specs=[pl.BlockSpec((tm, D_MODEL), row),
                  _const_spec((1, D_MODEL)), _const_spec((1, D_MODEL)), _const_spec((1, D_MODEL)),
                  _const_spec(w_c.shape), _const_spec(w_zt.shape),
                  _const_spec(qg.shape), _const_spec(kvg.shape),
                  _const_spec(w_uqt.shape), _const_spec(w_k.shape), _const_spec(w_vt.shape),
                  pl.BlockSpec((M_ROPE // 2, tm), col), pl.BlockSpec((M_ROPE // 2, tm), col),
                  pl.BlockSpec((tm, M_QK_PAD), row), pl.BlockSpec((tm, M_QK_PAD), row)],
        out_specs=[pl.BlockSpec((M_HEADS, M_QK_PAD, tm), lambda i: (0, 0, i)),
                   pl.BlockSpec((M_HEADS, tm, M_QK_PAD), lambda i: (0, i, 0)),
                   pl.BlockSpec((M_HEADS, 1, M_V, tm), lambda i: (0, i, 0, 0)),
                   pl.BlockSpec((M_WIDTH, tm), col)],
        out_shape=[jax.ShapeDtypeStruct((M_HEADS, M_QK_PAD, s), BF16),
                   jax.ShapeDtypeStruct((M_HEADS, s, M_QK_PAD), BF16),
                   jax.ShapeDtypeStruct((M_HEADS, s // tm, M_V, tm), BF16),
                   jax.ShapeDtypeStruct((M_WIDTH, s), F32)],
        compiler_params=_cparams("parallel"),
        name="in_proj_m",
    )(x, g, scale, shift, w_c, w_zt, qg, kvg, w_uqt, w_k, w_vt, cos_t, sin_t, cos_k, sin_k)


def _dilated_kernel(q_ref, kp_ref, kc_ref, vp_ref, vc_ref, bias_ref, o_ref, lse_ref):
    n = pl.program_id(1)
    col = lax.broadcasted_iota(jnp.int32, (A_BLOCK, 2 * A_BLOCK), 1)
    no_prev = jnp.logical_and(n == 0, col < A_BLOCK)
    q_all = q_ref[...] * jnp.asarray(A_HEAD_DIM ** -0.5, BF16)
    for hd in range(A_HEADS):
        sl = slice(hd * A_HEAD_DIM, (hd + 1) * A_HEAD_DIM)
        k = jnp.concatenate([kp_ref[:, sl], kc_ref[:, sl]], axis=0)
        v = jnp.concatenate([vp_ref[:, sl], vc_ref[:, sl]], axis=0)
        s = _nt_dot(q_all[:, sl], k) + bias_ref[hd]
        s = jnp.where(no_prev, NEG_INF, s)
        m = jnp.max(s, axis=-1, keepdims=True)
        p = jnp.exp(s - m)
        l = jnp.sum(p, axis=-1, keepdims=True)
        o = jnp.dot(p.astype(BF16), v, preferred_element_type=F32) / l
        o_ref[:, sl] = o
        lse_ref[:, sl] = jnp.broadcast_to(m + jnp.log(l), (A_BLOCK, A_HEAD_DIM))


def _dilated_group(a_qkv, bias, g, r):
    s = a_qkv.shape[0]
    l_sub = s // r
    nb = l_sub // A_BLOCK
    per_phase = A_QKV // A_WIDTH
    view = a_qkv.reshape(l_sub, r * A_QKV)
    blk = (A_BLOCK, A_WIDTH)

    def spec(t, prev):
        if prev:
            return pl.BlockSpec(blk, lambda p, n: (jnp.maximum(n - 1, 0), p * per_phase + t * A_GROUPS + g))
        return pl.BlockSpec(blk, lambda p, n: (n, p * per_phase + t * A_GROUPS + g))

    out_spec = pl.BlockSpec(blk, lambda p, n: (n, p))
    o, lse = pl.pallas_call(
        _dilated_kernel,
        grid=(r, nb),
        in_specs=[spec(0, False), spec(1, True), spec(1, False), spec(2, True), spec(2, False),
                  pl.BlockSpec(bias.shape, lambda p, n: (0, 0, 0))],
        out_specs=[out_spec, out_spec],
        out_shape=[jax.ShapeDtypeStruct((l_sub, r * A_WIDTH), F32)] * 2,
        compiler_params=_cparams("parallel", "parallel"),
        name=f"dilated_g{g}",
    )(view, view, view, view, view, bias)
    return o.reshape(s, A_WIDTH), lse.reshape(s, A_WIDTH)


def _t5_bucket(dist):
    exact = REL_BUCKETS // 2
    d = jnp.maximum(dist, 1).astype(F32)
    large = exact + (jnp.log(d / exact) / math.log(REL_MAX_DIST / exact) * (REL_BUCKETS - exact)).astype(jnp.int32)
    large = jnp.minimum(large, REL_BUCKETS - 1)
    return jnp.where(dist < exact, dist, large)


def _dilated_bias(rel_bias, g, window, r):
    qi = jnp.arange(A_BLOCK)[:, None]
    ki = jnp.arange(2 * A_BLOCK)[None, :]
    j = qi + A_BLOCK - ki
    band = (j >= 0) & (j <= window // r)
    tab = rel_bias[:, g * A_HEADS:(g + 1) * A_HEADS]
    b = tab[_t5_bucket(jnp.maximum(j, 0) * r)].astype(F32).transpose(2, 0, 1)
    return jnp.where(band[None], b, NEG_INF)


def _mla_kernel(qt_ref, k_ref, vt_ref, mzt_ref, o_ref, m_sc, l_sc, acc_sc):
    i = pl.program_id(1)
    t = ATT_TILE
    qt = qt_ref[0]
    m_sc[...] = jnp.full(m_sc.shape, NEG_INF, F32)
    l_sc[...] = jnp.zeros(l_sc.shape, F32)
    acc_sc[...] = jnp.zeros(acc_sc.shape, F32)

    def step(kt, diagonal):
        k = k_ref[0, pl.ds(pl.multiple_of(kt * t, t), t), :]
        s = jnp.dot(k, qt, preferred_element_type=F32)
        if diagonal:
            kpos = lax.broadcasted_iota(jnp.int32, (t, t), 0)
            qpos = lax.broadcasted_iota(jnp.int32, (t, t), 1)
            s = jnp.where(kpos <= qpos, s, NEG_INF)
        m_prev = m_sc[...]
        m_new = jnp.maximum(m_prev, jnp.max(s, axis=0, keepdims=True))
        alpha = jnp.exp(m_prev - m_new)
        p = jnp.exp(s - m_new)
        l_sc[...] = alpha * l_sc[...] + jnp.sum(p, axis=0, keepdims=True)
        acc_sc[...] = alpha * acc_sc[...] + jnp.dot(vt_ref[0, kt], p.astype(BF16), preferred_element_type=F32)
        m_sc[...] = m_new

    def body(kt, carry):
        step(kt, False)
        return carry

    lax.fori_loop(0, i, body, 0)
    step(i, True)
    mz = mzt_ref[...]
    gate = mz * (1.0 / (1.0 + jnp.exp(-mz)))
    o_ref[...] = (acc_sc[...] / l_sc[...] * gate).astype(BF16)


def _mla_attention(qt, k, vt, mzt):
    s = k.shape[1]
    t = ATT_TILE
    return pl.pallas_call(
        _mla_kernel,
        grid=(M_HEADS, s // t),
        in_specs=[pl.BlockSpec((1, M_QK_PAD, t), lambda h, i: (h, 0, i)),
                  pl.BlockSpec((1, s, M_QK_PAD), lambda h, i: (h, 0, 0)),
                  pl.BlockSpec((1, s // t, M_V, t), lambda h, i: (h, 0, 0, 0)),
                  pl.BlockSpec((M_V, t), lambda h, i: (h, i))],
        out_specs=pl.BlockSpec((M_V, t), lambda h, i: (h, i)),
        out_shape=jax.ShapeDtypeStruct((M_WIDTH, s), BF16),
        scratch_shapes=[pltpu.VMEM((1, t), F32), pltpu.VMEM((1, t), F32), pltpu.VMEM((M_V, t), F32)],
        compiler_params=_cparams("parallel", "arbitrary"),
        name="mla_attention",
    )(qt, k, vt, mzt)


def _out_kernel(final, x_ref, o0_ref, o1_ref, o2_ref, l0_ref, l1_ref, l2_ref, az_ref, ymt_ref, mg_ref, gate_ref,
                wa_ref, wb_ref, wo_ref, fg_ref, out_ref):
    l0, l1, l2 = l0_ref[...], l1_ref[...], l2_ref[...]
    mx = jnp.maximum(jnp.maximum(l0, l1), l2)
    e0, e1, e2 = jnp.exp(l0 - mx), jnp.exp(l1 - mx), jnp.exp(l2 - mx)
    mix = (o0_ref[...] * e0 + o1_ref[...] * e1 + o2_ref[...] * e2) / (e0 + e1 + e2)
    az = az_ref[...]
    y_a = (mix * (az * (1.0 / (1.0 + jnp.exp(-az))))).astype(BF16)
    t_a = jnp.dot(y_a, wa_ref[...], preferred_element_type=F32)
    t_b = lax.dot_general(ymt_ref[...], wb_ref[...], (((0,), (0,)), ((), ())), preferred_element_type=F32)
    mg = mg_ref[...]
    g_a = 1.0 / (1.0 + jnp.exp(-mg[:, :D_MODEL]))
    g_m = 1.0 / (1.0 + jnp.exp(-mg[:, D_MODEL:]))
    merged = (g_a * t_a + g_m * t_b).astype(BF16)
    y = x_ref[...] + gate_ref[...] * jnp.dot(merged, wo_ref[...], preferred_element_type=F32)
    if final:
        y = y * lax.rsqrt(jnp.mean(y * y, axis=-1, keepdims=True) + EPS) * fg_ref[...]
    out_ref[...] = y


def _out_proj(x, o_groups, lse_groups, a_z, ymt, mg, gate, w_a, w_b, w_o, final_g, final):
    s = x.shape[0]
    tm = ROW_TILE
    row = lambda i: (i, 0)
    a_spec = pl.BlockSpec((tm, A_WIDTH), row)
    return pl.pallas_call(
        functools.partial(_out_kernel, final),
        grid=(s // tm,),
        in_specs=[pl.BlockSpec((tm, D_MODEL), row)] + [a_spec] * 7
                 + [pl.BlockSpec((M_WIDTH, tm), lambda i: (0, i)),
                    pl.BlockSpec((tm, 2 * D_MODEL), row),
                    _const_spec((1, D_MODEL)),
                    _const_spec(w_a.shape), _const_spec(w_b.shape), _const_spec(w_o.shape),
                    _const_spec((1, D_MODEL))],
        out_specs=pl.BlockSpec((tm, D_MODEL), row),
        out_shape=jax.ShapeDtypeStruct((s, D_MODEL), F32),
        compiler_params=_cparams("parallel"),
        name="out_proj",
    )(x, *o_groups, *lse_groups, a_z, ymt, mg, gate, w_a, w_b, w_o, final_g)


def _prep_layer(w_in, w_uq, w_ukv, w_out_a, w_out_b, w_o):
    o_az = A_QKV
    o_cq = o_az + A_WIDTH
    o_ckv = o_cq + M_Q_LORA
    o_kr = o_ckv + M_KV_LORA
    o_mz = o_kr + M_ROPE
    o_mg = o_mz + M_WIDTH
    half = M_ROPE // 2
    w_a = w_in[:, :o_cq].astype(BF16)
    w_g = w_in[:, o_mg:].astype(BF16)
    w_kr = w_in[:, o_kr:o_mz]
    z_lo = jnp.zeros((D_MODEL, M_NOPE), F32)
    z_hi = jnp.zeros((D_MODEL, M_QK_PAD - M_QK), F32)
    kr_a = jnp.concatenate([z_lo, w_kr, z_hi], axis=1)
    kr_b = jnp.concatenate([z_lo, w_kr[:, half:], w_kr[:, :half], z_hi], axis=1)
    w_c = jnp.concatenate([w_in[:, o_cq:o_kr], kr_a, kr_b], axis=1).astype(BF16)
    w_zt = w_in[:, o_mz:o_mg].T.astype(BF16)
    scale = M_QK ** -0.5
    uq = (w_uq * scale).reshape(M_Q_LORA, M_HEADS, M_QK)
    uq = jnp.pad(uq, ((0, 0), (0, 0), (0, M_QK_PAD - M_QK)))
    w_uqt = uq.reshape(M_Q_LORA, M_HEADS * M_QK_PAD).T.astype(BF16)
    ukv = w_ukv.reshape(M_KV_LORA, M_HEADS, M_NOPE + M_V)
    w_k = jnp.pad(ukv[:, :, :M_NOPE], ((0, 0), (0, 0), (0, M_QK_PAD - M_NOPE)))
    w_k = w_k.reshape(M_KV_LORA, M_HEADS * M_QK_PAD).astype(BF16)
    w_vt = ukv[:, :, M_NOPE:].reshape(M_KV_LORA, M_WIDTH).T.astype(BF16)
    return (w_a, w_g, w_c, w_zt, w_uqt, w_k, w_vt,
            w_out_a.astype(BF16), w_out_b.astype(BF16), w_o.astype(BF16))


def _rope_tables(positions):
    half = M_ROPE // 2
    inv_freq = 1.0 / (ROPE_THETA ** (jnp.arange(0, M_ROPE, 2, dtype=F32) / M_ROPE))
    ang = positions.astype(F32)[:, None] * inv_freq
    cos, sin = jnp.cos(ang), jnp.sin(ang)
    s = positions.shape[0]
    z_lo = jnp.zeros((s, M_NOPE), F32)
    z_hi = jnp.zeros((s, M_QK_PAD - M_QK), F32)
    cos_k = jnp.concatenate([z_lo, cos, cos, z_hi], axis=1)
    sin_k = jnp.concatenate([z_lo, -sin, sin, z_hi], axis=1)
    return cos.T, sin.T, cos_k, sin_k


def kernel(x, c, positions, w_ada, b_ada, norm_g, w_in, q_norm_g, w_uq, kv_norm_g, w_ukv, w_out_a, w_out_b, w_o,
           rel_bias, final_norm_g):
    batch, s, _ = x.shape
    assert batch == 1 and s % (A_BLOCK * A_PATTERNS[-1][1]) == 0 and s % ATT_TILE == 0 and s % ROW_TILE == 0
    depth = w_ada.shape[0]
    xs = x[0]
    mod = _modulation(c, w_ada, b_ada)
    cos_t, sin_t, cos_k, sin_k = _rope_tables(positions[0])
    biases = [_dilated_bias(rel_bias, g, w, r) for g, (w, r) in enumerate(A_PATTERNS)]
    final_g = final_norm_g.reshape(1, D_MODEL)
    for l in range(depth):
        (w_a, w_g, w_c, w_zt, w_uqt, w_k, w_vt, wo_a, wo_b, wo) = _prep_layer(
            w_in[l], w_uq[l], w_ukv[l], w_out_a[l], w_out_b[l], w_o[l])
        shift, scale, gate = mod[l, 0], mod[l, 1], mod[l, 2]
        g = norm_g[l].reshape(1, D_MODEL)
        a_qkv, a_z, mg = _in_proj_a(xs, g, scale, shift, w_a, w_g)
        qt, k, vt, mzt = _in_proj_m(xs, g, scale, shift, w_c, w_zt,
                                    q_norm_g[l].reshape(1, M_Q_LORA), kv_norm_g[l].reshape(1, M_KV_LORA),
                                    w_uqt, w_k, w_vt, cos_t, sin_t, cos_k, sin_k)
        outs = [_dilated_group(a_qkv, biases[gi], gi, r) for gi, (_, r) in enumerate(A_PATTERNS)]
        ymt = _mla_attention(qt, k, vt, mzt)
        xs = _out_proj(xs, [o for o, _ in outs], [ls for _, ls in outs], a_z, ymt, mg, gate,
                       wo_a, wo_b, wo, final_g, l == depth - 1)
    return xs[None]
```

```python
import functools
import math

import jax
import jax.numpy as jnp
from jax import lax
from jax.experimental import pallas as pl
from jax.experimental.pallas import tpu as pltpu

F32 = jnp.float32
BF16 = jnp.bfloat16

D_MODEL = 1024
A_PATTERNS = ((128, 1), (512, 4), (2048, 16))
A_GROUPS = 3
A_HEADS = 8
A_HEAD_DIM = 64
A_WIDTH = A_HEADS * A_HEAD_DIM
A_BLOCK = 128
A_QKV = 3 * A_GROUPS * A_WIDTH
M_HEADS = 16
M_Q_LORA = 256
M_KV_LORA = 128
M_NOPE = 64
M_ROPE = 32
M_V = 64
M_QK = M_NOPE + M_ROPE
M_QK_PAD = 128
M_WIDTH = M_HEADS * M_V
M_V_EXT = M_V + 16
ROPE_THETA = 10000.0
REL_BUCKETS = 32
REL_MAX_DIST = 2048
EPS = 1e-6
NEG_INF = -1e30

ROW_TILE = 512
ATT_TILE = 512
MLA_CHAINS = 2
VMEM_LIMIT = 56 * 1024 * 1024


def _cparams(*sem):
    return pltpu.CompilerParams(dimension_semantics=sem, vmem_limit_bytes=VMEM_LIMIT)


def _const_spec(shape):
    nd = len(shape)
    return pl.BlockSpec(shape, lambda *_: (0,) * nd, pipeline_mode=pl.Buffered(1))


def _nt_dot(a, b):
    return lax.dot_general(a, b, (((1,), (1,)), ((), ())), preferred_element_type=F32)


def _modulated_norm(x, g, scale, shift):
    y = x * lax.rsqrt(jnp.mean(x * x, axis=-1, keepdims=True) + EPS)
    return (y * g) * (1.0 + scale) + shift


def _mod_kernel(c_ref, w_ref, b_ref, o_ref):
    c = c_ref[...]
    c_act = c * (1.0 / (1.0 + jnp.exp(-c)))
    o_ref[0, 0] = jnp.sum(c_act * w_ref[0], axis=0, keepdims=True) + b_ref[0, 0]


def _modulation(c, w_ada, b_ada):
    depth = w_ada.shape[0]
    c_col = c.reshape(D_MODEL, 1)
    b4 = b_ada.reshape(depth, 3, 1, D_MODEL)
    return pl.pallas_call(
        _mod_kernel,
        grid=(depth, 3),
        in_specs=[pl.BlockSpec((D_MODEL, 1), lambda l, j: (0, 0)),
                  pl.BlockSpec((1, D_MODEL, D_MODEL), lambda l, j: (l, 0, j)),
                  pl.BlockSpec((1, 1, 1, D_MODEL), lambda l, j: (l, j, 0, 0))],
        out_specs=pl.BlockSpec((1, 1, 1, D_MODEL), lambda l, j: (l, j, 0, 0)),
        out_shape=jax.ShapeDtypeStruct((depth, 3, 1, D_MODEL), F32),
        compiler_params=_cparams("parallel", "parallel"),
        name="adaln_mod",
    )(c_col, w_ada, b4)


def _in_a_kernel(x_ref, g_ref, sc_ref, sh_ref, wa_ref, wg_ref, qkv_ref, az_ref, mg_ref):
    h = _modulated_norm(x_ref[...], g_ref[...], sc_ref[...], sh_ref[...]).astype(BF16)
    a = jnp.dot(h, wa_ref[...], preferred_element_type=F32)
    qkv_ref[...] = a[:, :A_QKV].astype(BF16)
    az_ref[...] = a[:, A_QKV:]
    mg_ref[...] = jnp.dot(h, wg_ref[...], preferred_element_type=F32)


def _in_proj_a(x, g, scale, shift, w_a, w_g):
    s = x.shape[0]
    tm = ROW_TILE
    row = lambda i: (i, 0)
    return pl.pallas_call(
        _in_a_kernel,
        grid=(s // tm,),
        in_specs=[pl.BlockSpec((tm, D_MODEL), row),
                  _const_spec((1, D_MODEL)), _const_spec((1, D_MODEL)), _const_spec((1, D_MODEL)),
                  _const_spec(w_a.shape), _const_spec(w_g.shape)],
        out_specs=[pl.BlockSpec((tm, A_QKV), row),
                   pl.BlockSpec((tm, A_WIDTH), row),
                   pl.BlockSpec((tm, 2 * D_MODEL), row)],
        out_shape=[jax.ShapeDtypeStruct((s, A_QKV), BF16),
                   jax.ShapeDtypeStruct((s, A_WIDTH), F32),
                   jax.ShapeDtypeStruct((s, 2 * D_MODEL), F32)],
        compiler_params=_cparams("parallel"),
        name="in_proj_a",
    )(x, g, scale, shift, w_a, w_g)


def _in_m_kernel(x_ref, g_ref, sc_ref, sh_ref, wc_ref, wzt_ref, qg_ref, kvg_ref, wuqt_ref, wk_ref, wvt_ref,
                 cost_ref, sint_ref, ck_ref, sk_ref, qt_ref, k_ref, vt_ref, mzt_ref):
    h = _modulated_norm(x_ref[...], g_ref[...], sc_ref[...], sh_ref[...]).astype(BF16)
    mzt_ref[...] = _nt_dot(wzt_ref[...], h)
    c = jnp.dot(h, wc_ref[...], preferred_element_type=F32)
    cq = c[:, :M_Q_LORA]
    ckv = c[:, M_Q_LORA:M_Q_LORA + M_KV_LORA]
    kr_a = c[:, M_Q_LORA + M_KV_LORA:M_Q_LORA + M_KV_LORA + M_QK_PAD]
    kr_b = c[:, M_Q_LORA + M_KV_LORA + M_QK_PAD:]
    cqn = (cq * lax.rsqrt(jnp.mean(cq * cq, axis=-1, keepdims=True) + EPS) * qg_ref[...]).astype(BF16)
    ckvn = (ckv * lax.rsqrt(jnp.mean(ckv * ckv, axis=-1, keepdims=True) + EPS) * kvg_ref[...]).astype(BF16)

    tm = cq.shape[0]
    half = M_ROPE // 2
    qt = _nt_dot(wuqt_ref[...], cqn).reshape(M_HEADS, M_QK_PAD, tm)
    cos = cost_ref[...][None]
    sin = sint_ref[...][None]
    t1 = qt[:, M_NOPE:M_NOPE + half]
    t2 = qt[:, M_NOPE + half:M_QK]
    qt_ref[:, :M_NOPE] = qt[:, :M_NOPE].astype(BF16)
    qt_ref[:, M_NOPE:M_NOPE + half] = (t1 * cos - t2 * sin).astype(BF16)
    qt_ref[:, M_NOPE + half:M_QK] = (t1 * sin + t2 * cos).astype(BF16)
    qt_ref[:, M_QK:] = qt[:, M_QK:].astype(BF16)

    kr = kr_a * ck_ref[...] + kr_b * sk_ref[...]
    k_all = jnp.dot(ckvn, wk_ref[...], preferred_element_type=F32)
    for hd in range(M_HEADS):
        k_ref[hd] = (k_all[:, hd * M_QK_PAD:(hd + 1) * M_QK_PAD] + kr).astype(BF16)

    vt = _nt_dot(wvt_ref[...], ckvn)
    vt_ref[:, 0, :M_V] = vt.reshape(M_HEADS, M_V, tm).astype(BF16)
    vt_ref[:, 0, M_V:] = jnp.ones((M_HEADS, M_V_EXT - M_V, tm), BF16)


def _in_proj_m(x, g, scale, shift, w_c, w_zt, qg, kvg, w_uqt, w_k, w_vt, cos_t, sin_t, cos_k, sin_k):
    s = x.shape[0]
    tm = ATT_TILE
    row = lambda i: (i, 0)
    col = lambda i: (0, i)
    return pl.pallas_call(
        _in_m_kernel,
        grid=(s // tm,),
        in_specs=[pl.BlockSpec((tm, D_MODEL), row),
                  _const_spec((1, D_MODEL)), _const_spec((1, D_MODEL)), _const_spec((1, D_MODEL)),
                  _const_spec(w_c.shape), _const_spec(w_zt.shape),
                  _const_spec(qg.shape), _const_spec(kvg.shape),
                  _const_spec(w_uqt.shape), _const_spec(w_k.shape), _const_spec(w_vt.shape),
                  pl.BlockSpec((M_ROPE // 2, tm), col), pl.BlockSpec((M_ROPE // 2, tm), col),
                  pl.BlockSpec((tm, M_QK_PAD), row), pl.BlockSpec((tm, M_QK_PAD), row)],
        out_specs=[pl.BlockSpec((M_HEADS, M_QK_PAD, tm), lambda i: (0, 0, i)),
                   pl.BlockSpec((M_HEADS, tm, M_QK_PAD), lambda i: (0, i, 0)),
                   pl.BlockSpec((M_HEADS, 1, M_V_EXT, tm), lambda i: (0, i, 0, 0)),
                   pl.BlockSpec((M_WIDTH, tm), col)],
        out_shape=[jax.ShapeDtypeStruct((M_HEADS, M_QK_PAD, s), BF16),
                   jax.ShapeDtypeStruct((M_HEADS, s, M_QK_PAD), BF16),
                   jax.ShapeDtypeStruct((M_HEADS, s // tm, M_V_EXT, tm), BF16),
                   jax.ShapeDtypeStruct((M_WIDTH, s), F32)],
        compiler_params=_cparams("parallel"),
        name="in_proj_m",
    )(x, g, scale, shift, w_c, w_zt, qg, kvg, w_uqt, w_k, w_vt, cos_t, sin_t, cos_k, sin_k)


def _dilated_kernel(q_ref, kp_ref, kc_ref, vp_ref, vc_ref, bias_ref, o_ref, lse_ref):
    n = pl.program_id(1)
    col = lax.broadcasted_iota(jnp.int32, (A_BLOCK, 2 * A_BLOCK), 1)
    no_prev = jnp.logical_and(n == 0, col < A_BLOCK)
    q_all = q_ref[...] * jnp.asarray(A_HEAD_DIM ** -0.5, BF16)
    for hd in range(A_HEADS):
        sl = slice(hd * A_HEAD_DIM, (hd + 1) * A_HEAD_DIM)
        k = jnp.concatenate([kp_ref[:, sl], kc_ref[:, sl]], axis=0)
        v = jnp.concatenate([vp_ref[:, sl], vc_ref[:, sl]], axis=0)
        s = _nt_dot(q_all[:, sl], k) + bias_ref[hd]
        s = jnp.where(no_prev, NEG_INF, s)
        m = jnp.max(s, axis=-1, keepdims=True)
        p = jnp.exp(s - m)
        l = jnp.sum(p, axis=-1, keepdims=True)
        o = jnp.dot(p.astype(BF16), v, preferred_element_type=F32) / l
        o_ref[:, sl] = o
        lse_ref[:, sl] = jnp.broadcast_to(m + jnp.log(l), (A_BLOCK, A_HEAD_DIM))


def _dilated_group(a_qkv, bias, g, r):
    s = a_qkv.shape[0]
    l_sub = s // r
    nb = l_sub // A_BLOCK
    per_phase = A_QKV // A_WIDTH
    view = a_qkv.reshape(l_sub, r * A_QKV)
    blk = (A_BLOCK, A_WIDTH)

    def spec(t, prev):
        if prev:
            return pl.BlockSpec(blk, lambda p, n: (jnp.maximum(n - 1, 0), p * per_phase + t * A_GROUPS + g))
        return pl.BlockSpec(blk, lambda p, n: (n, p * per_phase + t * A_GROUPS + g))

    out_spec = pl.BlockSpec(blk, lambda p, n: (n, p))
    o, lse = pl.pallas_call(
        _dilated_kernel,
        grid=(r, nb),
        in_specs=[spec(0, False), spec(1, True), spec(1, False), spec(2, True), spec(2, False),
                  pl.BlockSpec(bias.shape, lambda p, n: (0, 0, 0))],
        out_specs=[out_spec, out_spec],
        out_shape=[jax.ShapeDtypeStruct((l_sub, r * A_WIDTH), F32)] * 2,
        compiler_params=_cparams("parallel", "parallel"),
        name=f"dilated_g{g}",
    )(view, view, view, view, view, bias)
    return o.reshape(s, A_WIDTH), lse.reshape(s, A_WIDTH)


def _t5_bucket(dist):
    exact = REL_BUCKETS // 2
    d = jnp.maximum(dist, 1).astype(F32)
    large = exact + (jnp.log(d / exact) / math.log(REL_MAX_DIST / exact) * (REL_BUCKETS - exact)).astype(jnp.int32)
    large = jnp.minimum(large, REL_BUCKETS - 1)
    return jnp.where(dist < exact, dist, large)


def _dilated_bias(rel_bias, g, window, r):
    qi = jnp.arange(A_BLOCK)[:, None]
    ki = jnp.arange(2 * A_BLOCK)[None, :]
    j = qi + A_BLOCK - ki
    band = (j >= 0) & (j <= window // r)
    tab = rel_bias[:, g * A_HEADS:(g + 1) * A_HEADS]
    b = tab[_t5_bucket(jnp.maximum(j, 0) * r)].astype(F32).transpose(2, 0, 1)
    return jnp.where(band[None], b, NEG_INF)


def _mla_kernel(qt_ref, k_ref, vt_ref, mzt_ref, o_ref, s0_sc, s1_sc, mt0_sc, mt1_sc, m_sc, acc_sc):
    i = pl.program_id(1)
    t = ATT_TILE
    hw = t // MLA_CHAINS
    bufs = ((s0_sc, mt0_sc), (s1_sc, mt1_sc))
    m_sc[...] = jnp.full(m_sc.shape, NEG_INF, F32)
    acc_sc[...] = jnp.zeros(acc_sc.shape, F32)

    def scores(kt, diagonal):
        k = k_ref[0, pl.ds(pl.multiple_of(kt * t, t), t), :]
        s = jnp.dot(k, qt_ref[0], preferred_element_type=F32)
        if diagonal:
            kpos = lax.broadcasted_iota(jnp.int32, (t, t), 0)
            qpos = lax.broadcasted_iota(jnp.int32, (t, t), 1)
            s = jnp.where(kpos <= qpos, s, NEG_INF)
        return s

    def fill(slot, s):
        s_ref, mt_ref = bufs[slot]
        s_ref[...] = s
        mt_ref[...] = jnp.max(s, axis=0, keepdims=True)

    def accumulate(kt, slot):
        s_ref, mt_ref = bufs[slot]
        for c in range(MLA_CHAINS):
            cols = slice(c * hw, (c + 1) * hw)
            m_prev = m_sc[:, cols]
            m_new = jnp.maximum(m_prev, mt_ref[:, cols])
            alpha = jnp.exp2(m_prev - m_new)
            p = jnp.exp2(s_ref[:, cols] - m_new).astype(BF16)
            acc_sc[:, cols] = alpha * acc_sc[:, cols] + jnp.dot(vt_ref[0, kt], p, preferred_element_type=F32)
            m_sc[:, cols] = m_new

    def stage(kt, slot, refill):
        s_next = None if refill is None else scores(kt + 2, refill == "diagonal")
        accumulate(kt, slot)
        if s_next is not None:
            fill(slot, s_next)

    @pl.when(i == 0)
    def _():
        fill(0, scores(0, True))
        accumulate(0, 0)

    @pl.when(i == 1)
    def _():
        fill(0, scores(0, False))
        fill(1, scores(1, True))
        accumulate(0, 0)
        accumulate(1, 1)

    @pl.when(i >= 2)
    def _():
        fill(0, scores(0, False))
        fill(1, scores(1, False))

        def body(pair, carry):
            stage(2 * pair, 0, "full")
            stage(2 * pair + 1, 1, "full")
            return carry

        lax.fori_loop(0, (i - 2) // 2, body, 0)

        @pl.when(i % 2 == 0)
        def _():
            stage(i - 2, 0, "diagonal")
            stage(i - 1, 1, None)
            stage(i, 0, None)

        @pl.when(i % 2 == 1)
        def _():
            stage(i - 3, 0, "full")
            stage(i - 2, 1, "diagonal")
            stage(i - 1, 0, None)
            stage(i, 1, None)

    mz = mzt_ref[...]
    gate = mz * (1.0 / (1.0 + jnp.exp(-mz)))
    acc = acc_sc[...]
    o_ref[...] = (acc[:M_V] / acc[M_V:M_V + 1] * gate).astype(BF16)


def _mla_attention(qt, k, vt, mzt):
    s = k.shape[1]
    t = ATT_TILE
    return pl.pallas_call(
        _mla_kernel,
        grid=(M_HEADS, s // t),
        in_specs=[pl.BlockSpec((1, M_QK_PAD, t), lambda h, i: (h, 0, i)),
                  pl.BlockSpec((1, s, M_QK_PAD), lambda h, i: (h, 0, 0)),
                  pl.BlockSpec((1, s // t, M_V_EXT, t), lambda h, i: (h, 0, 0, 0)),
                  pl.BlockSpec((M_V, t), lambda h, i: (h, i))],
        out_specs=pl.BlockSpec((M_V, t), lambda h, i: (h, i)),
        out_shape=jax.ShapeDtypeStruct((M_WIDTH, s), BF16),
        scratch_shapes=[pltpu.VMEM((t, t), F32), pltpu.VMEM((t, t), F32),
                        pltpu.VMEM((1, t), F32), pltpu.VMEM((1, t), F32),
                        pltpu.VMEM((1, t), F32), pltpu.VMEM((M_V_EXT, t), F32)],
        compiler_params=_cparams("parallel", "arbitrary"),
        name="mla_attention",
    )(qt, k, vt, mzt)


def _out_kernel(final, x_ref, o0_ref, o1_ref, o2_ref, l0_ref, l1_ref, l2_ref, az_ref, ymt_ref, mg_ref, gate_ref,
                wa_ref, wb_ref, wo_ref, fg_ref, out_ref):
    l0, l1, l2 = l0_ref[...], l1_ref[...], l2_ref[...]
    mx = jnp.maximum(jnp.maximum(l0, l1), l2)
    e0, e1, e2 = jnp.exp(l0 - mx), jnp.exp(l1 - mx), jnp.exp(l2 - mx)
    mix = (o0_ref[...] * e0 + o1_ref[...] * e1 + o2_ref[...] * e2) / (e0 + e1 + e2)
    az = az_ref[...]
    y_a = (mix * (az * (1.0 / (1.0 + jnp.exp(-az))))).astype(BF16)
    t_a = jnp.dot(y_a, wa_ref[...], preferred_element_type=F32)
    t_b = lax.dot_general(ymt_ref[...], wb_ref[...], (((0,), (0,)), ((), ())), preferred_element_type=F32)
    mg = mg_ref[...]
    g_a = 1.0 / (1.0 + jnp.exp(-mg[:, :D_MODEL]))
    g_m = 1.0 / (1.0 + jnp.exp(-mg[:, D_MODEL:]))
    merged = (g_a * t_a + g_m * t_b).astype(BF16)
    y = x_ref[...] + gate_ref[...] * jnp.dot(merged, wo_ref[...], preferred_element_type=F32)
    if final:
        y = y * lax.rsqrt(jnp.mean(y * y, axis=-1, keepdims=True) + EPS) * fg_ref[...]
    out_ref[...] = y


def _out_proj(x, o_groups, lse_groups, a_z, ymt, mg, gate, w_a, w_b, w_o, final_g, final):
    s = x.shape[0]
    tm = ROW_TILE
    row = lambda i: (i, 0)
    a_spec = pl.BlockSpec((tm, A_WIDTH), row)
    return pl.pallas_call(
        functools.partial(_out_kernel, final),
        grid=(s // tm,),
        in_specs=[pl.BlockSpec((tm, D_MODEL), row)] + [a_spec] * 7
                 + [pl.BlockSpec((M_WIDTH, tm), lambda i: (0, i)),
                    pl.BlockSpec((tm, 2 * D_MODEL), row),
                    _const_spec((1, D_MODEL)),
                    _const_spec(w_a.shape), _const_spec(w_b.shape), _const_spec(w_o.shape),
                    _const_spec((1, D_MODEL))],
        out_specs=pl.BlockSpec((tm, D_MODEL), row),
        out_shape=jax.ShapeDtypeStruct((s, D_MODEL), F32),
        compiler_params=_cparams("parallel"),
        name="out_proj",
    )(x, *o_groups, *lse_groups, a_z, ymt, mg, gate, w_a, w_b, w_o, final_g)


def _prep_layer(w_in, w_uq, w_ukv, w_out_a, w_out_b, w_o):
    o_az = A_QKV
    o_cq = o_az + A_WIDTH
    o_ckv = o_cq + M_Q_LORA
    o_kr = o_ckv + M_KV_LORA
    o_mz = o_kr + M_ROPE
    o_mg = o_mz + M_WIDTH
    half = M_ROPE // 2
    w_a = w_in[:, :o_cq].astype(BF16)
    w_g = w_in[:, o_mg:].astype(BF16)
    w_kr = w_in[:, o_kr:o_mz]
    z_lo = jnp.zeros((D_MODEL, M_NOPE), F32)
    z_hi = jnp.zeros((D_MODEL, M_QK_PAD - M_QK), F32)
    kr_a = jnp.concatenate([z_lo, w_kr, z_hi], axis=1)
    kr_b = jnp.concatenate([z_lo, w_kr[:, half:], w_kr[:, :half], z_hi], axis=1)
    w_c = jnp.concatenate([w_in[:, o_cq:o_kr], kr_a, kr_b], axis=1).astype(BF16)
    w_zt = w_in[:, o_mz:o_mg].T.astype(BF16)
    scale = M_QK ** -0.5 * math.log2(math.e)
    uq = (w_uq * scale).reshape(M_Q_LORA, M_HEADS, M_QK)
    uq = jnp.pad(uq, ((0, 0), (0, 0), (0, M_QK_PAD - M_QK)))
    w_uqt = uq.reshape(M_Q_LORA, M_HEADS * M_QK_PAD).T.astype(BF16)
    ukv = w_ukv.reshape(M_KV_LORA, M_HEADS, M_NOPE + M_V)
    w_k = jnp.pad(ukv[:, :, :M_NOPE], ((0, 0), (0, 0), (0, M_QK_PAD - M_NOPE)))
    w_k = w_k.reshape(M_KV_LORA, M_HEADS * M_QK_PAD).astype(BF16)
    w_vt = ukv[:, :, M_NOPE:].reshape(M_KV_LORA, M_WIDTH).T.astype(BF16)
    return (w_a, w_g, w_c, w_zt, w_uqt, w_k, w_vt,
            w_out_a.astype(BF16), w_out_b.astype(BF16), w_o.astype(BF16))


def _rope_tables(positions):
    half = M_ROPE // 2
    inv_freq = 1.0 / (ROPE_THETA ** (jnp.arange(0, M_ROPE, 2, dtype=F32) / M_ROPE))
    ang = positions.astype(F32)[:, None] * inv_freq
    cos, sin = jnp.cos(ang), jnp.sin(ang)
    s = positions.shape[0]
    z_lo = jnp.zeros((s, M_NOPE), F32)
    z_hi = jnp.zeros((s, M_QK_PAD - M_QK), F32)
    cos_k = jnp.concatenate([z_lo, cos, cos, z_hi], axis=1)
    sin_k = jnp.concatenate([z_lo, -sin, sin, z_hi], axis=1)
    return cos.T, sin.T, cos_k, sin_k


def kernel(x, c, positions, w_ada, b_ada, norm_g, w_in, q_norm_g, w_uq, kv_norm_g, w_ukv, w_out_a, w_out_b, w_o,
           rel_bias, final_norm_g):
    batch, s, _ = x.shape
    assert batch == 1 and s % (A_BLOCK * A_PATTERNS[-1][1]) == 0 and s % ATT_TILE == 0 and s % ROW_TILE == 0
    depth = w_ada.shape[0]
    xs = x[0]
    mod = _modulation(c, w_ada, b_ada)
    cos_t, sin_t, cos_k, sin_k = _rope_tables(positions[0])
    biases = [_dilated_bias(rel_bias, g, w, r) for g, (w, r) in enumerate(A_PATTERNS)]
    final_g = final_norm_g.reshape(1, D_MODEL)
    for l in range(depth):
        (w_a, w_g, w_c, w_zt, w_uqt, w_k, w_vt, wo_a, wo_b, wo) = _prep_layer(
            w_in[l], w_uq[l], w_ukv[l], w_out_a[l], w_out_b[l], w_o[l])
        shift, scale, gate = mod[l, 0], mod[l, 1], mod[l, 2]
        g = norm_g[l].reshape(1, D_MODEL)
        a_qkv, a_z, mg = _in_proj_a(xs, g, scale, shift, w_a, w_g)
        qt, k, vt, mzt = _in_proj_m(xs, g, scale, shift, w_c, w_zt,
                                    q_norm_g[l].reshape(1, M_Q_LORA), kv_norm_g[l].reshape(1, M_KV_LORA),
                                    w_uqt, w_k, w_vt, cos_t, sin_t, cos_k, sin_k)
        outs = [_dilated_group(a_qkv, biases[gi], gi, r) for gi, (_, r) in enumerate(A_PATTERNS)]
        ymt = _mla_attention(qt, k, vt, mzt)
        xs = _out_proj(xs, [o for o, _ in outs], [ls for _, ls in outs], a_z, ymt, mg, gate,
                       wo_a, wo_b, wo, final_g, l == depth - 1)
    return xs[None]
```

```python
import functools
import math

import jax
import jax.numpy as jnp
from jax import lax
from jax.experimental import pallas as pl
from jax.experimental.pallas import tpu as pltpu

F32 = jnp.float32
BF16 = jnp.bfloat16

D_MODEL = 1024
A_PATTERNS = ((128, 1), (512, 4), (2048, 16))
A_GROUPS = 3
A_HEADS = 8
A_HEAD_DIM = 64
A_WIDTH = A_HEADS * A_HEAD_DIM
A_BLOCK = 128
A_QKV = 3 * A_GROUPS * A_WIDTH
M_HEADS = 16
M_Q_LORA = 256
M_KV_LORA = 128
M_NOPE = 64
M_ROPE = 32
M_V = 64
M_QK = M_NOPE + M_ROPE
M_QK_PAD = 128
M_WIDTH = M_HEADS * M_V
M_V_EXT = M_V + 16
ROPE_THETA = 10000.0
REL_BUCKETS = 32
REL_MAX_DIST = 2048
EPS = 1e-6
NEG_INF = -1e30
LANES = 128

ROW_TILE = 512
ATT_TILE = 512
MLA_Q_TILE = 2 * ATT_TILE
MLA_CHUNK = 256
VMEM_LIMIT = 56 * 1024 * 1024


def _cparams(*sem):
    return pltpu.CompilerParams(dimension_semantics=sem, vmem_limit_bytes=VMEM_LIMIT)


def _const_spec(shape):
    nd = len(shape)
    return pl.BlockSpec(shape, lambda *_: (0,) * nd, pipeline_mode=pl.Buffered(1))


def _nt_dot(a, b):
    return lax.dot_general(a, b, (((1,), (1,)), ((), ())), preferred_element_type=F32)


def _modulated_norm(x, g, scale, shift):
    y = x * lax.rsqrt(jnp.mean(x * x, axis=-1, keepdims=True) + EPS)
    return (y * g) * (1.0 + scale) + shift


def _mod_kernel(c_ref, w_ref, b_ref, o_ref):
    c = c_ref[...]
    c_act = c * (1.0 / (1.0 + jnp.exp(-c)))
    o_ref[0, 0] = jnp.sum(c_act * w_ref[0], axis=0, keepdims=True) + b_ref[0, 0]


def _modulation(c, w_ada, b_ada):
    depth = w_ada.shape[0]
    c_col = c.reshape(D_MODEL, 1)
    b4 = b_ada.reshape(depth, 3, 1, D_MODEL)
    return pl.pallas_call(
        _mod_kernel,
        grid=(depth, 3),
        in_specs=[pl.BlockSpec((D_MODEL, 1), lambda l, j: (0, 0)),
                  pl.BlockSpec((1, D_MODEL, D_MODEL), lambda l, j: (l, 0, j)),
                  pl.BlockSpec((1, 1, 1, D_MODEL), lambda l, j: (l, j, 0, 0))],
        out_specs=pl.BlockSpec((1, 1, 1, D_MODEL), lambda l, j: (l, j, 0, 0)),
        out_shape=jax.ShapeDtypeStruct((depth, 3, 1, D_MODEL), F32),
        compiler_params=_cparams("parallel", "parallel"),
        name="adaln_mod",
    )(c_col, w_ada, b4)


def _in_a_kernel(x_ref, g_ref, sc_ref, sh_ref, wa_ref, wg_ref, qkv0_ref, qkv1_ref, qkv2_ref, az_ref, mg_ref,
                 stage_sc):
    h = _modulated_norm(x_ref[...], g_ref[...], sc_ref[...], sh_ref[...]).astype(BF16)
    a = jnp.dot(h, wa_ref[...], preferred_element_type=F32)
    tm = a.shape[0]
    gw = 3 * A_WIDTH
    qkv0_ref[0] = a[:, :gw].astype(BF16)
    for g, out_ref in ((1, qkv1_ref), (2, qkv2_ref)):
        r = A_PATTERNS[g][1]
        for j in range(gw // LANES):
            stage_sc[j] = a[:, g * gw + j * LANES:g * gw + (j + 1) * LANES]
        for p in range(r):
            for j in range(gw // LANES):
                out_ref[p, :, j * LANES:(j + 1) * LANES] = stage_sc[j, pl.ds(p, tm // r, stride=r), :].astype(BF16)
    az_ref[...] = a[:, A_GROUPS * gw:]
    mg_ref[...] = jnp.dot(h, wg_ref[...], preferred_element_type=F32)


def _in_proj_a(x, g, scale, shift, w_a, w_g):
    s = x.shape[0]
    tm = ROW_TILE
    gw = 3 * A_WIDTH
    row = lambda i: (i, 0)
    phase_specs = [pl.BlockSpec((r, tm // r, gw), lambda i: (0, i, 0)) for _, r in A_PATTERNS]
    phase_shapes = [jax.ShapeDtypeStruct((r, s // r, gw), BF16) for _, r in A_PATTERNS]
    return pl.pallas_call(
        _in_a_kernel,
        grid=(s // tm,),
        in_specs=[pl.BlockSpec((tm, D_MODEL), row),
                  _const_spec((1, D_MODEL)), _const_spec((1, D_MODEL)), _const_spec((1, D_MODEL)),
                  _const_spec(w_a.shape), _const_spec(w_g.shape)],
        out_specs=phase_specs + [pl.BlockSpec((tm, A_WIDTH), row),
                                 pl.BlockSpec((tm, 2 * D_MODEL), row)],
        out_shape=phase_shapes + [jax.ShapeDtypeStruct((s, A_WIDTH), F32),
                                  jax.ShapeDtypeStruct((s, 2 * D_MODEL), F32)],
        scratch_shapes=[pltpu.VMEM((gw // LANES, tm, LANES), F32)],
        compiler_params=_cparams("parallel"),
        name="in_proj_a",
    )(x, g, scale, shift, w_a, w_g)


def _in_m_kernel(x_ref, g_ref, sc_ref, sh_ref, wc_ref, wzt_ref, qg_ref, kvg_ref, wuqt_ref, wk_ref, wvt_ref,
                 cost_ref, sint_ref, ck_ref, sk_ref, qt_ref, k_ref, vt_ref, mzt_ref):
    h = _modulated_norm(x_ref[...], g_ref[...], sc_ref[...], sh_ref[...]).astype(BF16)
    mzt_ref[...] = _nt_dot(wzt_ref[...], h)
    c = jnp.dot(h, wc_ref[...], preferred_element_type=F32)
    cq = c[:, :M_Q_LORA]
    ckv = c[:, M_Q_LORA:M_Q_LORA + M_KV_LORA]
    kr_a = c[:, M_Q_LORA + M_KV_LORA:M_Q_LORA + M_KV_LORA + M_QK_PAD]
    kr_b = c[:, M_Q_LORA + M_KV_LORA + M_QK_PAD:]
    cqn = (cq * lax.rsqrt(jnp.mean(cq * cq, axis=-1, keepdims=True) + EPS) * qg_ref[...]).astype(BF16)
    ckvn = (ckv * lax.rsqrt(jnp.mean(ckv * ckv, axis=-1, keepdims=True) + EPS) * kvg_ref[...]).astype(BF16)

    tm = cq.shape[0]
    half = M_ROPE // 2
    qt = _nt_dot(wuqt_ref[...], cqn).reshape(M_HEADS, M_QK_PAD, tm)
    cos = cost_ref[...][None]
    sin = sint_ref[...][None]
    t1 = qt[:, M_NOPE:M_NOPE + half]
    t2 = qt[:, M_NOPE + half:M_QK]
    qt_ref[:, :M_NOPE] = qt[:, :M_NOPE].astype(BF16)
    qt_ref[:, M_NOPE:M_NOPE + half] = (t1 * cos - t2 * sin).astype(BF16)
    qt_ref[:, M_NOPE + half:M_QK] = (t1 * sin + t2 * cos).astype(BF16)
    qt_ref[:, M_QK:] = qt[:, M_QK:].astype(BF16)

    kr = kr_a * ck_ref[...] + kr_b * sk_ref[...]
    k_all = jnp.dot(ckvn, wk_ref[...], preferred_element_type=F32)
    for hd in range(M_HEADS):
        k_ref[hd] = (k_all[:, hd * M_QK_PAD:(hd + 1) * M_QK_PAD] + kr).astype(BF16)

    vt = _nt_dot(wvt_ref[...], ckvn)
    vt_ref[:, 0, :M_V] = vt.reshape(M_HEADS, M_V, tm).astype(BF16)
    vt_ref[:, 0, M_V:] = jnp.ones((M_HEADS, M_V_EXT - M_V, tm), BF16)


def _in_proj_m(x, g, scale, shift, w_c, w_zt, qg, kvg, w_uqt, w_k, w_vt, cos_t, sin_t, cos_k, sin_k):
    s = x.shape[0]
    tm = ATT_TILE
    row = lambda i: (i, 0)
    col = lambda i: (0, i)
    return pl.pallas_call(
        _in_m_kernel,
        grid=(s // tm,),
        in_specs=[pl.BlockSpec((tm, D_MODEL), row),
                  _const_spec((1, D_MODEL)), _const_spec((1, D_MODEL)), _const_spec((1, D_MODEL)),
                  _const_spec(w_c.shape), _const_spec(w_zt.shape),
                  _const_spec(qg.shape), _const_spec(kvg.shape),
                  _const_spec(w_uqt.shape), _const_spec(w_k.shape), _const_spec(w_vt.shape),
                  pl.BlockSpec((M_ROPE // 2, tm), col), pl.BlockSpec((M_ROPE // 2, tm), col),
                  pl.BlockSpec((tm, M_QK_PAD), row), pl.BlockSpec((tm, M_QK_PAD), row)],
        out_specs=[pl.BlockSpec((M_HEADS, M_QK_PAD, tm), lambda i: (0, 0, i)),
                   pl.BlockSpec((M_HEADS, tm, M_QK_PAD), lambda i: (0, i, 0)),
                   pl.BlockSpec((M_HEADS, 1, M_V_EXT, tm), lambda i: (0, i, 0, 0)),
                   pl.BlockSpec((M_WIDTH, tm), col)],
        out_shape=[jax.ShapeDtypeStruct((M_HEADS, M_QK_PAD, s), BF16),
                   jax.ShapeDtypeStruct((M_HEADS, s, M_QK_PAD), BF16),
                   jax.ShapeDtypeStruct((M_HEADS, s // tm, M_V_EXT, tm), BF16),
                   jax.ShapeDtypeStruct((M_WIDTH, s), F32)],
        compiler_params=_cparams("parallel"),
        name="in_proj_m",
    )(x, g, scale, shift, w_c, w_zt, qg, kvg, w_uqt, w_k, w_vt, cos_t, sin_t, cos_k, sin_k)


def _dilated_kernel(q_ref, kp_ref, kc_ref, vp_ref, vc_ref, bias_ref, o_ref, lse_ref):
    n = pl.program_id(1)
    col = lax.broadcasted_iota(jnp.int32, (A_BLOCK, 2 * A_BLOCK), 1)
    no_prev = jnp.logical_and(n == 0, col < A_BLOCK)
    q_all = q_ref[0] * jnp.asarray(A_HEAD_DIM ** -0.5, BF16)
    for hd in range(A_HEADS):
        sl = slice(hd * A_HEAD_DIM, (hd + 1) * A_HEAD_DIM)
        k = jnp.concatenate([kp_ref[0, :, sl], kc_ref[0, :, sl]], axis=0)
        v = jnp.concatenate([vp_ref[0, :, sl], vc_ref[0, :, sl]], axis=0)
        s = _nt_dot(q_all[:, sl], k) + bias_ref[hd]
        s = jnp.where(no_prev, NEG_INF, s)
        m = jnp.max(s, axis=-1, keepdims=True)
        p = jnp.exp(s - m)
        l = jnp.sum(p, axis=-1, keepdims=True)
        o = jnp.dot(p.astype(BF16), v, preferred_element_type=F32) / l
        o_ref[0, :, sl] = o
        lse_ref[0, :, sl] = jnp.broadcast_to(m + jnp.log(l), (A_BLOCK, A_HEAD_DIM))


def _dilated_group(qkv, bias, g):
    r, l_sub, _ = qkv.shape
    blk = (1, A_BLOCK, A_WIDTH)

    def spec(t, prev):
        if prev:
            return pl.BlockSpec(blk, lambda p, n: (p, jnp.maximum(n - 1, 0), t))
        return pl.BlockSpec(blk, lambda p, n: (p, n, t))

    out_spec = pl.BlockSpec(blk, lambda p, n: (p, n, 0))
    return pl.pallas_call(
        _dilated_kernel,
        grid=(r, l_sub // A_BLOCK),
        in_specs=[spec(0, False), spec(1, True), spec(1, False), spec(2, True), spec(2, False),
                  pl.BlockSpec(bias.shape, lambda p, n: (0, 0, 0))],
        out_specs=[out_spec, out_spec],
        out_shape=[jax.ShapeDtypeStruct((r, l_sub, A_WIDTH), F32)] * 2,
        compiler_params=_cparams("parallel", "parallel"),
        name=f"dilated_g{g}",
    )(qkv, qkv, qkv, qkv, qkv, bias)


def _t5_bucket(dist):
    exact = REL_BUCKETS // 2
    d = jnp.maximum(dist, 1).astype(F32)
    large = exact + (jnp.log(d / exact) / math.log(REL_MAX_DIST / exact) * (REL_BUCKETS - exact)).astype(jnp.int32)
    large = jnp.minimum(large, REL_BUCKETS - 1)
    return jnp.where(dist < exact, dist, large)


def _dilated_bias(rel_bias, g, window, r):
    qi = jnp.arange(A_BLOCK)[:, None]
    ki = jnp.arange(2 * A_BLOCK)[None, :]
    j = qi + A_BLOCK - ki
    band = (j >= 0) & (j <= window // r)
    tab = rel_bias[:, g * A_HEADS:(g + 1) * A_HEADS].astype(F32)
    pick = jax.nn.one_hot(_t5_bucket(jnp.maximum(j, 0) * r), REL_BUCKETS, dtype=F32)
    b = jnp.einsum("qkb,bh->hqk", pick, tab, precision=lax.Precision.HIGHEST)
    return jnp.where(band[None], b, NEG_INF)


def _mla_kernel(qt_ref, k_ref, vt_ref, mzt_ref, o_ref, s0_sc, s1_sc, mt0_sc, mt1_sc, m_sc, acc_sc):
    i = pl.program_id(1)
    tk, tq, cw = ATT_TILE, MLA_Q_TILE, MLA_CHUNK
    bufs = ((s0_sc, mt0_sc), (s1_sc, mt1_sc))
    m_sc[...] = jnp.full(m_sc.shape, NEG_INF, F32)
    acc_sc[...] = jnp.zeros(acc_sc.shape, F32)

    def scores(kt, diagonal_offset):
        k = k_ref[0, pl.ds(pl.multiple_of(kt * tk, tk), tk), :]
        s = jnp.dot(k, qt_ref[0], preferred_element_type=F32)
        if diagonal_offset is not None:
            kpos = lax.broadcasted_iota(jnp.int32, (tk, tq), 0) + diagonal_offset
            qpos = lax.broadcasted_iota(jnp.int32, (tk, tq), 1)
            s = jnp.where(kpos <= qpos, s, NEG_INF)
        return s

    def fill(slot, s):
        s_ref, mt_ref = bufs[slot]
        s_ref[...] = s
        mt_ref[...] = jnp.max(s, axis=0, keepdims=True)

    def accumulate(kt, slot):
        s_ref, mt_ref = bufs[slot]
        for c in range(tq // cw):
            cols = slice(c * cw, (c + 1) * cw)
            m_prev = m_sc[:, cols]
            m_new = jnp.maximum(m_prev, mt_ref[:, cols])
            alpha = jnp.exp2(m_prev - m_new)
            p = jnp.exp2((s_ref[:, cols] - m_new).astype(BF16))
            acc_sc[:, cols] = alpha * acc_sc[:, cols] + jnp.dot(vt_ref[0, kt], p, preferred_element_type=F32)
            m_sc[:, cols] = m_new

    def stage(kt, slot, refill, diagonal_offset=None):
        s_next = scores(kt + 2, diagonal_offset) if refill else None
        accumulate(kt, slot)
        if refill:
            fill(slot, s_next)

    @pl.when(i == 0)
    def _():
        fill(0, scores(0, 0))
        fill(1, scores(1, tk))

    @pl.when(i > 0)
    def _():
        fill(0, scores(0, None))
        fill(1, scores(1, None))

        def body(pair, carry):
            stage(2 * pair, 0, True)
            stage(2 * pair + 1, 1, True)
            return carry

        lax.fori_loop(0, i - 1, body, 0)
        stage(2 * i - 2, 0, True, 0)
        stage(2 * i - 1, 1, True, tk)

    stage(2 * i, 0, False)
    stage(2 * i + 1, 1, False)
    mz = mzt_ref[...]
    gate = mz * (1.0 / (1.0 + jnp.exp(-mz)))
    acc = acc_sc[...]
    o_ref[...] = (acc[:M_V] / acc[M_V:M_V + 1] * gate).astype(BF16)


def _mla_attention(qt, k, vt, mzt):
    s = k.shape[1]
    tk, tq = ATT_TILE, MLA_Q_TILE
    return pl.pallas_call(
        _mla_kernel,
        grid=(M_HEADS, s // tq),
        in_specs=[pl.BlockSpec((1, M_QK_PAD, tq), lambda h, i: (h, 0, i)),
                  pl.BlockSpec((1, s, M_QK_PAD), lambda h, i: (h, 0, 0)),
                  pl.BlockSpec((1, s // tk, M_V_EXT, tk), lambda h, i: (h, 0, 0, 0)),
                  pl.BlockSpec((M_V, tq), lambda h, i: (h, i))],
        out_specs=pl.BlockSpec((M_V, tq), lambda h, i: (h, i)),
        out_shape=jax.ShapeDtypeStruct((M_WIDTH, s), BF16),
        scratch_shapes=[pltpu.VMEM((tk, tq), F32), pltpu.VMEM((tk, tq), F32),
                        pltpu.VMEM((1, tq), F32), pltpu.VMEM((1, tq), F32),
                        pltpu.VMEM((1, tq), F32), pltpu.VMEM((M_V_EXT, tq), F32)],
        compiler_params=_cparams("parallel", "arbitrary"),
        name="mla_attention",
    )(qt, k, vt, mzt)


def _out_kernel(final, x_ref, o0_ref, o1_ref, o2_ref, l0_ref, l1_ref, l2_ref, az_ref, ymt_ref, mg_ref, gate_ref,
                wa_ref, wb_ref, wo_ref, fg_ref, out_ref, o1_sc, o2_sc, l1_sc, l2_sc):
    tm = x_ref.shape[0]
    nj = A_WIDTH // LANES
    for src, dst in ((o1_ref, o1_sc), (o2_ref, o2_sc), (l1_ref, l1_sc), (l2_ref, l2_sc)):
        r = src.shape[0]
        for p in range(r):
            for j in range(nj):
                dst[j, pl.ds(p, tm // r, stride=r), :] = src[p, :, j * LANES:(j + 1) * LANES]

    def rows(sc):
        return jnp.concatenate([sc[j] for j in range(nj)], axis=1)

    l0, l1, l2 = l0_ref[0], rows(l1_sc), rows(l2_sc)
    mx = jnp.maximum(jnp.maximum(l0, l1), l2)
    e0, e1, e2 = jnp.exp(l0 - mx), jnp.exp(l1 - mx), jnp.exp(l2 - mx)
    mix = (o0_ref[0] * e0 + rows(o1_sc) * e1 + rows(o2_sc) * e2) / (e0 + e1 + e2)
    az = az_ref[...]
    y_a = (mix * (az * (1.0 / (1.0 + jnp.exp(-az))))).astype(BF16)
    t_a = jnp.dot(y_a, wa_ref[...], preferred_element_type=F32)
    t_b = lax.dot_general(ymt_ref[...], wb_ref[...], (((0,), (0,)), ((), ())), preferred_element_type=F32)
    mg = mg_ref[...]
    g_a = 1.0 / (1.0 + jnp.exp(-mg[:, :D_MODEL]))
    g_m = 1.0 / (1.0 + jnp.exp(-mg[:, D_MODEL:]))
    merged = (g_a * t_a + g_m * t_b).astype(BF16)
    y = x_ref[...] + gate_ref[...] * jnp.dot(merged, wo_ref[...], preferred_element_type=F32)
    if final:
        y = y * lax.rsqrt(jnp.mean(y * y, axis=-1, keepdims=True) + EPS) * fg_ref[...]
    out_ref[...] = y


def _out_proj(x, o_groups, lse_groups, a_z, ymt, mg, gate, w_a, w_b, w_o, final_g, final):
    s = x.shape[0]
    tm = ROW_TILE
    row = lambda i: (i, 0)
    phase_specs = [pl.BlockSpec((r, tm // r, A_WIDTH), lambda i: (0, i, 0)) for _, r in A_PATTERNS]
    return pl.pallas_call(
        functools.partial(_out_kernel, final),
        grid=(s // tm,),
        in_specs=[pl.BlockSpec((tm, D_MODEL), row)] + phase_specs * 2
                 + [pl.BlockSpec((tm, A_WIDTH), row),
                    pl.BlockSpec((M_WIDTH, tm), lambda i: (0, i)),
                    pl.BlockSpec((tm, 2 * D_MODEL), row),
                    _const_spec((1, D_MODEL)),
                    _const_spec(w_a.shape), _const_spec(w_b.shape), _const_spec(w_o.shape),
                    _const_spec((1, D_MODEL))],
        out_specs=pl.BlockSpec((tm, D_MODEL), row),
        out_shape=jax.ShapeDtypeStruct((s, D_MODEL), F32),
        scratch_shapes=[pltpu.VMEM((A_WIDTH // LANES, tm, LANES), F32)] * 4,
        compiler_params=_cparams("parallel"),
        name="out_proj",
    )(x, *o_groups, *lse_groups, a_z, ymt, mg, gate, w_a, w_b, w_o, final_g)


def _prep_layer(w_in, w_uq, w_ukv, w_out_a, w_out_b, w_o):
    o_az = A_QKV
    o_cq = o_az + A_WIDTH
    o_ckv = o_cq + M_Q_LORA
    o_kr = o_ckv + M_KV_LORA
    o_mz = o_kr + M_ROPE
    o_mg = o_mz + M_WIDTH
    half = M_ROPE // 2
    w_qkv = w_in[:, :A_QKV].reshape(D_MODEL, 3, A_GROUPS, A_WIDTH).transpose(0, 2, 1, 3).reshape(D_MODEL, A_QKV)
    w_a = jnp.concatenate([w_qkv, w_in[:, o_az:o_cq]], axis=1).astype(BF16)
    w_g = w_in[:, o_mg:].astype(BF16)
    w_kr = w_in[:, o_kr:o_mz]
    z_lo = jnp.zeros((D_MODEL, M_NOPE), F32)
    z_hi = jnp.zeros((D_MODEL, M_QK_PAD - M_QK), F32)
    kr_a = jnp.concatenate([z_lo, w_kr, z_hi], axis=1)
    kr_b = jnp.concatenate([z_lo, w_kr[:, half:], w_kr[:, :half], z_hi], axis=1)
    w_c = jnp.concatenate([w_in[:, o_cq:o_kr], kr_a, kr_b], axis=1).astype(BF16)
    w_zt = w_in[:, o_mz:o_mg].T.astype(BF16)
    scale = M_QK ** -0.5 * math.log2(math.e)
    uq = (w_uq * scale).reshape(M_Q_LORA, M_HEADS, M_QK)
    uq = jnp.pad(uq, ((0, 0), (0, 0), (0, M_QK_PAD - M_QK)))
    w_uqt = uq.reshape(M_Q_LORA, M_HEADS * M_QK_PAD).T.astype(BF16)
    ukv = w_ukv.reshape(M_KV_LORA, M_HEADS, M_NOPE + M_V)
    w_k = jnp.pad(ukv[:, :, :M_NOPE], ((0, 0), (0, 0), (0, M_QK_PAD - M_NOPE)))
    w_k = w_k.reshape(M_KV_LORA, M_HEADS * M_QK_PAD).astype(BF16)
    w_vt = ukv[:, :, M_NOPE:].reshape(M_KV_LORA, M_WIDTH).T.astype(BF16)
    return (w_a, w_g, w_c, w_zt, w_uqt, w_k, w_vt,
            w_out_a.astype(BF16), w_out_b.astype(BF16), w_o.astype(BF16))


def _rope_tables(positions):
    half = M_ROPE // 2
    inv_freq = 1.0 / (ROPE_THETA ** (jnp.arange(0, M_ROPE, 2, dtype=F32) / M_ROPE))
    ang = positions.astype(F32)[:, None] * inv_freq
    cos, sin = jnp.cos(ang), jnp.sin(ang)
    s = positions.shape[0]
    z_lo = jnp.zeros((s, M_NOPE), F32)
    z_hi = jnp.zeros((s, M_QK_PAD - M_QK), F32)
    cos_k = jnp.concatenate([z_lo, cos, cos, z_hi], axis=1)
    sin_k = jnp.concatenate([z_lo, -sin, sin, z_hi], axis=1)
    return cos.T, sin.T, cos_k, sin_k


def kernel(x, c, positions, w_ada, b_ada, norm_g, w_in, q_norm_g, w_uq, kv_norm_g, w_ukv, w_out_a, w_out_b, w_o,
           rel_bias, final_norm_g):
    batch, s, _ = x.shape
    assert batch == 1 and s % (A_BLOCK * A_PATTERNS[-1][1]) == 0 and s % MLA_Q_TILE == 0 and s % ROW_TILE == 0
    depth = w_ada.shape[0]
    xs = x[0]
    mod = _modulation(c, w_ada, b_ada)
    cos_t, sin_t, cos_k, sin_k = _rope_tables(positions[0])
    biases = [_dilated_bias(rel_bias, g, w, r) for g, (w, r) in enumerate(A_PATTERNS)]
    final_g = final_norm_g.reshape(1, D_MODEL)
    for l in range(depth):
        (w_a, w_g, w_c, w_zt, w_uqt, w_k, w_vt, wo_a, wo_b, wo) = _prep_layer(
            w_in[l], w_uq[l], w_ukv[l], w_out_a[l], w_out_b[l], w_o[l])
        shift, scale, gate = mod[l, 0], mod[l, 1], mod[l, 2]
        g = norm_g[l].reshape(1, D_MODEL)
        qkv0, qkv1, qkv2, a_z, mg = _in_proj_a(xs, g, scale, shift, w_a, w_g)
        qt, k, vt, mzt = _in_proj_m(xs, g, scale, shift, w_c, w_zt,
                                    q_norm_g[l].reshape(1, M_Q_LORA), kv_norm_g[l].reshape(1, M_KV_LORA),
                                    w_uqt, w_k, w_vt, cos_t, sin_t, cos_k, sin_k)
        outs = [_dilated_group(qkv, biases[gi], gi) for gi, qkv in enumerate((qkv0, qkv1, qkv2))]
        ymt = _mla_attention(qt, k, vt, mzt)
        xs = _out_proj(xs, [o for o, _ in outs], [ls for _, ls in outs], a_z, ymt, mg, gate,
                       wo_a, wo_b, wo, final_g, l == depth - 1)
    return xs[None]
```

```python
import functools
import math

import jax
import jax.numpy as jnp
from jax import lax
from jax.experimental import pallas as pl
from jax.experimental.pallas import tpu as pltpu

F32 = jnp.float32
BF16 = jnp.bfloat16

D_MODEL = 1024
A_PATTERNS = ((128, 1), (512, 4), (2048, 16))
A_GROUPS = 3
A_HEADS = 8
A_HEAD_DIM = 64
A_WIDTH = A_HEADS * A_HEAD_DIM
A_BLOCK = 128
A_QKV = 3 * A_GROUPS * A_WIDTH
M_HEADS = 16
M_Q_LORA = 256
M_KV_LORA = 128
M_NOPE = 64
M_ROPE = 32
M_V = 64
M_QK = M_NOPE + M_ROPE
M_QK_PAD = 128
M_WIDTH = M_HEADS * M_V
M_V_EXT = M_V + 16
ROPE_THETA = 10000.0
REL_BUCKETS = 32
REL_MAX_DIST = 2048
EPS = 1e-6
NEG_INF = -1e30
LANES = 128

ROW_TILE = 512
ATT_TILE = 512
MLA_Q_TILE = 2 * ATT_TILE
MLA_CHUNK = 256
VMEM_LIMIT = 56 * 1024 * 1024


def _cparams(*sem):
    return pltpu.CompilerParams(dimension_semantics=sem, vmem_limit_bytes=VMEM_LIMIT)


def _const_spec(shape):
    nd = len(shape)
    return pl.BlockSpec(shape, lambda *_: (0,) * nd, pipeline_mode=pl.Buffered(1))


def _nt_dot(a, b):
    return lax.dot_general(a, b, (((1,), (1,)), ((), ())), preferred_element_type=F32)


def _modulated_norm(x, g, scale, shift):
    y = x * lax.rsqrt(jnp.mean(x * x, axis=-1, keepdims=True) + EPS)
    return (y * g) * (1.0 + scale) + shift


def _mod_kernel(c_ref, w_ref, b_ref, o_ref):
    c = c_ref[...]
    c_act = c * (1.0 / (1.0 + jnp.exp(-c)))
    o_ref[0, 0] = jnp.sum(c_act * w_ref[0], axis=0, keepdims=True) + b_ref[0, 0]


def _modulation(c, w_ada, b_ada):
    depth = w_ada.shape[0]
    c_col = c.reshape(D_MODEL, 1)
    b4 = b_ada.reshape(depth, 3, 1, D_MODEL)
    return pl.pallas_call(
        _mod_kernel,
        grid=(depth, 3),
        in_specs=[pl.BlockSpec((D_MODEL, 1), lambda l, j: (0, 0)),
                  pl.BlockSpec((1, D_MODEL, D_MODEL), lambda l, j: (l, 0, j)),
                  pl.BlockSpec((1, 1, 1, D_MODEL), lambda l, j: (l, j, 0, 0))],
        out_specs=pl.BlockSpec((1, 1, 1, D_MODEL), lambda l, j: (l, j, 0, 0)),
        out_shape=jax.ShapeDtypeStruct((depth, 3, 1, D_MODEL), F32),
        compiler_params=_cparams("parallel", "parallel"),
        name="adaln_mod",
    )(c_col, w_ada, b4)


def _in_a_kernel(x_ref, g_ref, sc_ref, sh_ref, wa_ref, wg_ref, qkv0_ref, qkv1_ref, qkv2_ref, az_ref, mg_ref,
                 stage_sc):
    h = _modulated_norm(x_ref[...], g_ref[...], sc_ref[...], sh_ref[...]).astype(BF16)
    a = jnp.dot(h, wa_ref[...], preferred_element_type=F32)
    tm = a.shape[0]
    gw = 3 * A_WIDTH
    qkv0_ref[0] = a[:, :gw].astype(BF16)
    for g, out_ref in ((1, qkv1_ref), (2, qkv2_ref)):
        r = A_PATTERNS[g][1]
        for j in range(gw // LANES):
            stage_sc[j] = a[:, g * gw + j * LANES:g * gw + (j + 1) * LANES]
        for p in range(r):
            for j in range(gw // LANES):
                out_ref[p, :, j * LANES:(j + 1) * LANES] = stage_sc[j, pl.ds(p, tm // r, stride=r), :].astype(BF16)
    az_ref[...] = a[:, A_GROUPS * gw:]
    mg_ref[...] = jnp.dot(h, wg_ref[...], preferred_element_type=F32)


def _in_proj_a(x, g, scale, shift, w_a, w_g):
    s = x.shape[0]
    tm = ROW_TILE
    gw = 3 * A_WIDTH
    row = lambda i: (i, 0)
    phase_specs = [pl.BlockSpec((r, tm // r, gw), lambda i: (0, i, 0)) for _, r in A_PATTERNS]
    phase_shapes = [jax.ShapeDtypeStruct((r, s // r, gw), BF16) for _, r in A_PATTERNS]
    return pl.pallas_call(
        _in_a_kernel,
        grid=(s // tm,),
        in_specs=[pl.BlockSpec((tm, D_MODEL), row),
                  _const_spec((1, D_MODEL)), _const_spec((1, D_MODEL)), _const_spec((1, D_MODEL)),
                  _const_spec(w_a.shape), _const_spec(w_g.shape)],
        out_specs=phase_specs + [pl.BlockSpec((tm, A_WIDTH), row),
                                 pl.BlockSpec((tm, 2 * D_MODEL), row)],
        out_shape=phase_shapes + [jax.ShapeDtypeStruct((s, A_WIDTH), F32),
                                  jax.ShapeDtypeStruct((s, 2 * D_MODEL), F32)],
        scratch_shapes=[pltpu.VMEM((gw // LANES, tm, LANES), F32)],
        compiler_params=_cparams("parallel"),
        name="in_proj_a",
    )(x, g, scale, shift, w_a, w_g)


def _in_m_kernel(x_ref, g_ref, sc_ref, sh_ref, wc_ref, wzt_ref, qg_ref, kvg_ref, wuqt_ref, wk_ref, wvt_ref,
                 cost_ref, sint_ref, ck_ref, sk_ref, qt_ref, k_ref, vt_ref, mzt_ref):
    h = _modulated_norm(x_ref[...], g_ref[...], sc_ref[...], sh_ref[...]).astype(BF16)
    mzt_ref[...] = _nt_dot(wzt_ref[...], h)
    c = jnp.dot(h, wc_ref[...], preferred_element_type=F32)
    cq = c[:, :M_Q_LORA]
    ckv = c[:, M_Q_LORA:M_Q_LORA + M_KV_LORA]
    kr_a = c[:, M_Q_LORA + M_KV_LORA:M_Q_LORA + M_KV_LORA + M_QK_PAD]
    kr_b = c[:, M_Q_LORA + M_KV_LORA + M_QK_PAD:]
    cqn = (cq * lax.rsqrt(jnp.mean(cq * cq, axis=-1, keepdims=True) + EPS) * qg_ref[...]).astype(BF16)
    ckvn = (ckv * lax.rsqrt(jnp.mean(ckv * ckv, axis=-1, keepdims=True) + EPS) * kvg_ref[...]).astype(BF16)

    tm = cq.shape[0]
    half = M_ROPE // 2
    qt = _nt_dot(wuqt_ref[...], cqn).reshape(M_HEADS, M_QK_PAD, tm)
    cos = cost_ref[...][None]
    sin = sint_ref[...][None]
    t1 = qt[:, M_NOPE:M_NOPE + half]
    t2 = qt[:, M_NOPE + half:M_QK]
    qt_ref[:, :M_NOPE] = qt[:, :M_NOPE].astype(BF16)
    qt_ref[:, M_NOPE:M_NOPE + half] = (t1 * cos - t2 * sin).astype(BF16)
    qt_ref[:, M_NOPE + half:M_QK] = (t1 * sin + t2 * cos).astype(BF16)
    qt_ref[:, M_QK:] = qt[:, M_QK:].astype(BF16)

    kr = kr_a * ck_ref[...] + kr_b * sk_ref[...]
    k_all = jnp.dot(ckvn, wk_ref[...], preferred_element_type=F32)
    for hd in range(M_HEADS):
        k_ref[hd] = (k_all[:, hd * M_QK_PAD:(hd + 1) * M_QK_PAD] + kr).astype(BF16)

    vt = _nt_dot(wvt_ref[...], ckvn)
    vt_ref[:, 0, :M_V] = vt.reshape(M_HEADS, M_V, tm).astype(BF16)
    vt_ref[:, 0, M_V:] = jnp.ones((M_HEADS, M_V_EXT - M_V, tm), BF16)


def _in_proj_m(x, g, scale, shift, w_c, w_zt, qg, kvg, w_uqt, w_k, w_vt, cos_t, sin_t, cos_k, sin_k):
    s = x.shape[0]
    tm = ATT_TILE
    row = lambda i: (i, 0)
    col = lambda i: (0, i)
    return pl.pallas_call(
        _in_m_kernel,
        grid=(s // tm,),
        in_specs=[pl.BlockSpec((tm, D_MODEL), row),
                  _const_spec((1, D_MODEL)), _const_spec((1, D_MODEL)), _const_spec((1, D_MODEL)),
                  _const_spec(w_c.shape), _const_spec(w_zt.shape),
                  _const_spec(qg.shape), _const_spec(kvg.shape),
                  _const_spec(w_uqt.shape), _const_spec(w_k.shape), _const_spec(w_vt.shape),
                  pl.BlockSpec((M_ROPE // 2, tm), col), pl.BlockSpec((M_ROPE // 2, tm), col),
                  pl.BlockSpec((tm, M_QK_PAD), row), pl.BlockSpec((tm, M_QK_PAD), row)],
        out_specs=[pl.BlockSpec((M_HEADS, M_QK_PAD, tm), lambda i: (0, 0, i)),
                   pl.BlockSpec((M_HEADS, tm, M_QK_PAD), lambda i: (0, i, 0)),
                   pl.BlockSpec((M_HEADS, 1, M_V_EXT, tm), lambda i: (0, i, 0, 0)),
                   pl.BlockSpec((M_WIDTH, tm), col)],
        out_shape=[jax.ShapeDtypeStruct((M_HEADS, M_QK_PAD, s), BF16),
                   jax.ShapeDtypeStruct((M_HEADS, s, M_QK_PAD), BF16),
                   jax.ShapeDtypeStruct((M_HEADS, s // tm, M_V_EXT, tm), BF16),
                   jax.ShapeDtypeStruct((M_WIDTH, s), F32)],
        compiler_params=_cparams("parallel"),
        name="in_proj_m",
    )(x, g, scale, shift, w_c, w_zt, qg, kvg, w_uqt, w_k, w_vt, cos_t, sin_t, cos_k, sin_k)


def _dilated_kernel(q_ref, kp_ref, kc_ref, vp_ref, vc_ref, bias_ref, o_ref, lse_ref):
    pairs = A_WIDTH // LANES
    lo = lax.broadcasted_iota(jnp.int32, (1, LANES), 1) < A_HEAD_DIM
    q_all = q_ref[0] * jnp.asarray(A_HEAD_DIM ** -0.5, BF16)
    ones = jnp.ones((2 * A_BLOCK, LANES), BF16)
    zero = jnp.zeros((), BF16)

    scores, v_ext = [], []
    for pr in range(pairs):
        sl = slice(pr * LANES, (pr + 1) * LANES)
        k = jnp.concatenate([kp_ref[0, :, sl], kc_ref[0, :, sl]], axis=0)
        v = jnp.concatenate([vp_ref[0, :, sl], vc_ref[0, :, sl]], axis=0)
        v_ext.append(jnp.concatenate([v, ones], axis=1))
        q = q_all[:, sl]
        for half in range(2):
            qh = jnp.where(lo if half == 0 else jnp.logical_not(lo), q, zero)
            scores.append(_nt_dot(qh, k) + bias_ref[0, 2 * pr + half])

    probs, maxes = [], []
    for s in scores:
        m = jnp.max(s, axis=-1, keepdims=True)
        probs.append(jnp.exp(s - m).astype(BF16))
        maxes.append(m)

    for pr in range(pairs):
        sl = slice(pr * LANES, (pr + 1) * LANES)
        outs, lses = [], []
        for half in range(2):
            ol = jnp.dot(probs[2 * pr + half], v_ext[pr], preferred_element_type=F32)
            l = ol[:, LANES:]
            outs.append(ol[:, :LANES] / l)
            lses.append(maxes[2 * pr + half] + jnp.log(l))
        o_ref[0, :, sl] = jnp.where(lo, outs[0], outs[1])
        lse_ref[0, :, sl] = jnp.where(lo, lses[0], lses[1])


def _dilated_group(qkv, bias, g):
    r, l_sub, _ = qkv.shape
    blk = (1, A_BLOCK, A_WIDTH)
    bias_blk = (1,) + bias.shape[1:]

    def spec(t, prev):
        if prev:
            return pl.BlockSpec(blk, lambda p, n: (p, jnp.maximum(n - 1, 0), t))
        return pl.BlockSpec(blk, lambda p, n: (p, n, t))

    out_spec = pl.BlockSpec(blk, lambda p, n: (p, n, 0))
    return pl.pallas_call(
        _dilated_kernel,
        grid=(r, l_sub // A_BLOCK),
        in_specs=[spec(0, False), spec(1, True), spec(1, False), spec(2, True), spec(2, False),
                  pl.BlockSpec(bias_blk, lambda p, n: (jnp.where(n == 0, 1, 0), 0, 0, 0))],
        out_specs=[out_spec, out_spec],
        out_shape=[jax.ShapeDtypeStruct((r, l_sub, A_WIDTH), F32)] * 2,
        compiler_params=_cparams("parallel", "parallel"),
        name=f"dilated_g{g}",
    )(qkv, qkv, qkv, qkv, qkv, bias)


def _t5_bucket(dist):
    exact = REL_BUCKETS // 2
    d = jnp.maximum(dist, 1).astype(F32)
    large = exact + (jnp.log(d / exact) / math.log(REL_MAX_DIST / exact) * (REL_BUCKETS - exact)).astype(jnp.int32)
    large = jnp.minimum(large, REL_BUCKETS - 1)
    return jnp.where(dist < exact, dist, large)


def _dilated_bias(rel_bias, g, window, r):
    qi = jnp.arange(A_BLOCK)[:, None]
    ki = jnp.arange(2 * A_BLOCK)[None, :]
    j = qi + A_BLOCK - ki
    band = (j >= 0) & (j <= window // r)
    tab = rel_bias[:, g * A_HEADS:(g + 1) * A_HEADS].astype(F32)
    pick = jax.nn.one_hot(_t5_bucket(jnp.maximum(j, 0) * r), REL_BUCKETS, dtype=F32)
    b = jnp.einsum("qkb,bh->hqk", pick, tab, precision=lax.Precision.HIGHEST)
    b = jnp.where(band[None], b, NEG_INF)
    first = jnp.where(ki[None] < A_BLOCK, NEG_INF, b)
    return jnp.stack([b, first])


def _mla_kernel(qt_ref, qn_ref, k_ref, vt_ref, mzt_ref, o_ref, s0_sc, s1_sc, mt0_sc, mt1_sc, m_sc, acc_sc):
    i = pl.program_id(1)
    last = pl.num_programs(1) - 1
    tk, tq, cw = ATT_TILE, MLA_Q_TILE, MLA_CHUNK
    bufs = ((s0_sc, mt0_sc), (s1_sc, mt1_sc))
    m_sc[...] = jnp.full(m_sc.shape, NEG_INF, F32)
    acc_sc[...] = jnp.zeros(acc_sc.shape, F32)

    def scores(kt, diagonal_offset, q_ref=qt_ref):
        k = k_ref[0, pl.ds(pl.multiple_of(kt * tk, tk), tk), :]
        s = jnp.dot(k, q_ref[0], preferred_element_type=F32)
        if diagonal_offset is not None:
            kpos = lax.broadcasted_iota(jnp.int32, (tk, tq), 0) + diagonal_offset
            qpos = lax.broadcasted_iota(jnp.int32, (tk, tq), 1)
            s = jnp.where(kpos <= qpos, s, NEG_INF)
        return s

    def fill(slot, s):
        s_ref, mt_ref = bufs[slot]
        s_ref[...] = s
        mt_ref[...] = jnp.max(s, axis=0, keepdims=True)

    def accumulate(kt, slot):
        s_ref, mt_ref = bufs[slot]
        for c in range(tq // cw):
            cols = slice(c * cw, (c + 1) * cw)
            m_prev = m_sc[:, cols]
            m_new = jnp.maximum(m_prev, mt_ref[:, cols])
            alpha = jnp.exp2(m_prev - m_new)
            p = jnp.exp2((s_ref[:, cols] - m_new).astype(BF16))
            acc_sc[:, cols] = alpha * acc_sc[:, cols] + jnp.dot(vt_ref[0, kt], p, preferred_element_type=F32)
            m_sc[:, cols] = m_new

    def stage(kt, slot, refill, diagonal_offset=None):
        if refill == "ahead":
            s_next = scores(kt + 2, diagonal_offset)
        elif refill == "next":
            s_next = scores(slot, None, qn_ref)
        accumulate(kt, slot)
        if refill is not None:
            fill(slot, s_next)

    def pair(pr, refill="ahead", offsets=(None, None)):
        stage(2 * pr, 0, refill, offsets[0])
        stage(2 * pr + 1, 1, refill, offsets[1])

    @pl.when(i == 0)
    def _():
        fill(0, scores(0, 0))
        fill(1, scores(1, tk))

    @pl.when(i > 0)
    def _():
        def body(quad, carry):
            pair(2 * quad)
            pair(2 * quad + 1)
            return carry

        lax.fori_loop(0, (i - 1) // 2, body, 0)

        @pl.when((i - 1) % 2 == 1)
        def _():
            pair(i - 2)

        pair(i - 1, offsets=(0, tk))

    @pl.when(i < last)
    def _():
        pair(i, refill="next")

    @pl.when(i == last)
    def _():
        pair(i, refill=None)

    mz = mzt_ref[...]
    gate = mz * (1.0 / (1.0 + jnp.exp(-mz)))
    acc = acc_sc[...]
    o_ref[...] = (acc[:M_V] / acc[M_V:M_V + 1] * gate).astype(BF16)


def _mla_attention(qt, k, vt, mzt):
    s = k.shape[1]
    tk, tq = ATT_TILE, MLA_Q_TILE
    nq = s // tq
    return pl.pallas_call(
        _mla_kernel,
        grid=(M_HEADS, nq),
        in_specs=[pl.BlockSpec((1, M_QK_PAD, tq), lambda h, i: (h, 0, i)),
                  pl.BlockSpec((1, M_QK_PAD, tq), lambda h, i: (h, 0, jnp.minimum(i + 1, nq - 1))),
                  pl.BlockSpec((1, s, M_QK_PAD), lambda h, i: (h, 0, 0)),
                  pl.BlockSpec((1, s // tk, M_V_EXT, tk), lambda h, i: (h, 0, 0, 0)),
                  pl.BlockSpec((M_V, tq), lambda h, i: (h, i))],
        out_specs=pl.BlockSpec((M_V, tq), lambda h, i: (h, i)),
        out_shape=jax.ShapeDtypeStruct((M_WIDTH, s), BF16),
        scratch_shapes=[pltpu.VMEM((tk, tq), F32), pltpu.VMEM((tk, tq), F32),
                        pltpu.VMEM((1, tq), F32), pltpu.VMEM((1, tq), F32),
                        pltpu.VMEM((1, tq), F32), pltpu.VMEM((M_V_EXT, tq), F32)],
        compiler_params=_cparams("arbitrary", "arbitrary"),
        name="mla_attention",
    )(qt, qt, k, vt, mzt)


def _out_kernel(final, x_ref, o0_ref, o1_ref, o2_ref, l0_ref, l1_ref, l2_ref, az_ref, ymt_ref, mg_ref, gate_ref,
                wa_ref, wb_ref, wo_ref, fg_ref, out_ref, o1_sc, o2_sc, l1_sc, l2_sc):
    tm = x_ref.shape[0]
    nj = A_WIDTH // LANES
    for src, dst in ((o1_ref, o1_sc), (o2_ref, o2_sc), (l1_ref, l1_sc), (l2_ref, l2_sc)):
        r = src.shape[0]
        for p in range(r):
            for j in range(nj):
                dst[j, pl.ds(p, tm // r, stride=r), :] = src[p, :, j * LANES:(j + 1) * LANES]

    def rows(sc):
        return jnp.concatenate([sc[j] for j in range(nj)], axis=1)

    l0, l1, l2 = l0_ref[0], rows(l1_sc), rows(l2_sc)
    mx = jnp.maximum(jnp.maximum(l0, l1), l2)
    e0, e1, e2 = jnp.exp(l0 - mx), jnp.exp(l1 - mx), jnp.exp(l2 - mx)
    mix = (o0_ref[0] * e0 + rows(o1_sc) * e1 + rows(o2_sc) * e2) / (e0 + e1 + e2)
    az = az_ref[...]
    y_a = (mix * (az * (1.0 / (1.0 + jnp.exp(-az))))).astype(BF16)
    t_a = jnp.dot(y_a, wa_ref[...], preferred_element_type=F32)
    t_b = lax.dot_general(ymt_ref[...], wb_ref[...], (((0,), (0,)), ((), ())), preferred_element_type=F32)
    mg = mg_ref[...]
    g_a = 1.0 / (1.0 + jnp.exp(-mg[:, :D_MODEL]))
    g_m = 1.0 / (1.0 + jnp.exp(-mg[:, D_MODEL:]))
    merged = (g_a * t_a + g_m * t_b).astype(BF16)
    y = x_ref[...] + gate_ref[...] * jnp.dot(merged, wo_ref[...], preferred_element_type=F32)
    if final:
        y = y * lax.rsqrt(jnp.mean(y * y, axis=-1, keepdims=True) + EPS) * fg_ref[...]
    out_ref[...] = y


def _out_proj(x, o_groups, lse_groups, a_z, ymt, mg, gate, w_a, w_b, w_o, final_g, final):
    s = x.shape[0]
    tm = ROW_TILE
    row = lambda i: (i, 0)
    phase_specs = [pl.BlockSpec((r, tm // r, A_WIDTH), lambda i: (0, i, 0)) for _, r in A_PATTERNS]
    return pl.pallas_call(
        functools.partial(_out_kernel, final),
        grid=(s // tm,),
        in_specs=[pl.BlockSpec((tm, D_MODEL), row)] + phase_specs * 2
                 + [pl.BlockSpec((tm, A_WIDTH), row),
                    pl.BlockSpec((M_WIDTH, tm), lambda i: (0, i)),
                    pl.BlockSpec((tm, 2 * D_MODEL), row),
                    _const_spec((1, D_MODEL)),
                    _const_spec(w_a.shape), _const_spec(w_b.shape), _const_spec(w_o.shape),
                    _const_spec((1, D_MODEL))],
        out_specs=pl.BlockSpec((tm, D_MODEL), row),
        out_shape=jax.ShapeDtypeStruct((s, D_MODEL), F32),
        scratch_shapes=[pltpu.VMEM((A_WIDTH // LANES, tm, LANES), F32)] * 4,
        compiler_params=_cparams("parallel"),
        name="out_proj",
    )(x, *o_groups, *lse_groups, a_z, ymt, mg, gate, w_a, w_b, w_o, final_g)


def _prep_layer(w_in, w_uq, w_ukv, w_out_a, w_out_b, w_o):
    o_az = A_QKV
    o_cq = o_az + A_WIDTH
    o_ckv = o_cq + M_Q_LORA
    o_kr = o_ckv + M_KV_LORA
    o_mz = o_kr + M_ROPE
    o_mg = o_mz + M_WIDTH
    half = M_ROPE // 2
    w_qkv = w_in[:, :A_QKV].reshape(D_MODEL, 3, A_GROUPS, A_WIDTH).transpose(0, 2, 1, 3).reshape(D_MODEL, A_QKV)
    w_a = jnp.concatenate([w_qkv, w_in[:, o_az:o_cq]], axis=1).astype(BF16)
    w_g = w_in[:, o_mg:].astype(BF16)
    w_kr = w_in[:, o_kr:o_mz]
    z_lo = jnp.zeros((D_MODEL, M_NOPE), F32)
    z_hi = jnp.zeros((D_MODEL, M_QK_PAD - M_QK), F32)
    kr_a = jnp.concatenate([z_lo, w_kr, z_hi], axis=1)
    kr_b = jnp.concatenate([z_lo, w_kr[:, half:], w_kr[:, :half], z_hi], axis=1)
    w_c = jnp.concatenate([w_in[:, o_cq:o_kr], kr_a, kr_b], axis=1).astype(BF16)
    w_zt = w_in[:, o_mz:o_mg].T.astype(BF16)
    scale = M_QK ** -0.5 * math.log2(math.e)
    uq = (w_uq * scale).reshape(M_Q_LORA, M_HEADS, M_QK)
    uq = jnp.pad(uq, ((0, 0), (0, 0), (0, M_QK_PAD - M_QK)))
    w_uqt = uq.reshape(M_Q_LORA, M_HEADS * M_QK_PAD).T.astype(BF16)
    ukv = w_ukv.reshape(M_KV_LORA, M_HEADS, M_NOPE + M_V)
    w_k = jnp.pad(ukv[:, :, :M_NOPE], ((0, 0), (0, 0), (0, M_QK_PAD - M_NOPE)))
    w_k = w_k.reshape(M_KV_LORA, M_HEADS * M_QK_PAD).astype(BF16)
    w_vt = ukv[:, :, M_NOPE:].reshape(M_KV_LORA, M_WIDTH).T.astype(BF16)
    return (w_a, w_g, w_c, w_zt, w_uqt, w_k, w_vt,
            w_out_a.astype(BF16), w_out_b.astype(BF16), w_o.astype(BF16))


def _rope_tables(positions):
    half = M_ROPE // 2
    inv_freq = 1.0 / (ROPE_THETA ** (jnp.arange(0, M_ROPE, 2, dtype=F32) / M_ROPE))
    ang = positions.astype(F32)[:, None] * inv_freq
    cos, sin = jnp.cos(ang), jnp.sin(ang)
    s = positions.shape[0]
    z_lo = jnp.zeros((s, M_NOPE), F32)
    z_hi = jnp.zeros((s, M_QK_PAD - M_QK), F32)
    cos_k = jnp.concatenate([z_lo, cos, cos, z_hi], axis=1)
    sin_k = jnp.concatenate([z_lo, -sin, sin, z_hi], axis=1)
    return cos.T, sin.T, cos_k, sin_k


def kernel(x, c, positions, w_ada, b_ada, norm_g, w_in, q_norm_g, w_uq, kv_norm_g, w_ukv, w_out_a, w_out_b, w_o,
           rel_bias, final_norm_g):
    batch, s, _ = x.shape
    assert batch == 1 and s % (A_BLOCK * A_PATTERNS[-1][1]) == 0 and s % MLA_Q_TILE == 0 and s % ROW_TILE == 0
    depth = w_ada.shape[0]
    xs = x[0]
    mod = _modulation(c, w_ada, b_ada)
    cos_t, sin_t, cos_k, sin_k = _rope_tables(positions[0])
    biases = [_dilated_bias(rel_bias, g, w, r) for g, (w, r) in enumerate(A_PATTERNS)]
    final_g = final_norm_g.reshape(1, D_MODEL)
    for l in range(depth):
        (w_a, w_g, w_c, w_zt, w_uqt, w_k, w_vt, wo_a, wo_b, wo) = _prep_layer(
            w_in[l], w_uq[l], w_ukv[l], w_out_a[l], w_out_b[l], w_o[l])
        shift, scale, gate = mod[l, 0], mod[l, 1], mod[l, 2]
        g = norm_g[l].reshape(1, D_MODEL)
        qkv0, qkv1, qkv2, a_z, mg = _in_proj_a(xs, g, scale, shift, w_a, w_g)
        qt, k, vt, mzt = _in_proj_m(xs, g, scale, shift, w_c, w_zt,
                                    q_norm_g[l].reshape(1, M_Q_LORA), kv_norm_g[l].reshape(1, M_KV_LORA),
                                    w_uqt, w_k, w_vt, cos_t, sin_t, cos_k, sin_k)
        outs = [_dilated_group(qkv, biases[gi], gi) for gi, qkv in enumerate((qkv0, qkv1, qkv2))]
        ymt = _mla_attention(qt, k, vt, mzt)
        xs = _out_proj(xs, [o for o, _ in outs], [ls for _, ls in outs], a_z, ymt, mg, gate,
                       wo_a, wo_b, wo, final_g, l == depth - 1)
    return xs[None]
```

```python
import functools
import math

import jax
import jax.numpy as jnp
from jax import lax
from jax.experimental import pallas as pl
from jax.experimental.pallas import tpu as pltpu

F32 = jnp.float32
BF16 = jnp.bfloat16

D_MODEL = 1024
A_PATTERNS = ((128, 1), (512, 4), (2048, 16))
A_GROUPS = 3
A_HEADS = 8
A_HEAD_DIM = 64
A_WIDTH = A_HEADS * A_HEAD_DIM
A_BLOCK = 128
A_QKV = 3 * A_GROUPS * A_WIDTH
M_HEADS = 16
M_Q_LORA = 256
M_KV_LORA = 128
M_NOPE = 64
M_ROPE = 32
M_V = 64
M_QK = M_NOPE + M_ROPE
M_QK_PAD = 128
M_WIDTH = M_HEADS * M_V
M_V_EXT = M_V + 16
ROPE_THETA = 10000.0
REL_BUCKETS = 32
REL_MAX_DIST = 2048
EPS = 1e-6
NEG_INF = -1e30
LANES = 128

ROW_TILE = 512
ATT_TILE = 512
MLA_Q_TILE = 2 * ATT_TILE
MLA_CHUNK = 256
VMEM_LIMIT = 56 * 1024 * 1024


def _cparams(*sem):
    return pltpu.CompilerParams(dimension_semantics=sem, vmem_limit_bytes=VMEM_LIMIT)


def _const_spec(shape):
    nd = len(shape)
    return pl.BlockSpec(shape, lambda *_: (0,) * nd, pipeline_mode=pl.Buffered(1))


def _nt_dot(a, b):
    return lax.dot_general(a, b, (((1,), (1,)), ((), ())), preferred_element_type=F32)


def _modulated_norm(x, g, scale, shift):
    y = x * lax.rsqrt(jnp.mean(x * x, axis=-1, keepdims=True) + EPS)
    return (y * g) * (1.0 + scale) + shift


def _mod_kernel(c_ref, w_ref, b_ref, o_ref):
    c = c_ref[...]
    c_act = c * (1.0 / (1.0 + jnp.exp(-c)))
    o_ref[0, 0] = jnp.sum(c_act * w_ref[0], axis=0, keepdims=True) + b_ref[0, 0]


def _modulation(c, w_ada, b_ada):
    depth = w_ada.shape[0]
    c_col = c.reshape(D_MODEL, 1)
    b4 = b_ada.reshape(depth, 3, 1, D_MODEL)
    return pl.pallas_call(
        _mod_kernel,
        grid=(depth, 3),
        in_specs=[pl.BlockSpec((D_MODEL, 1), lambda l, j: (0, 0)),
                  pl.BlockSpec((1, D_MODEL, D_MODEL), lambda l, j: (l, 0, j)),
                  pl.BlockSpec((1, 1, 1, D_MODEL), lambda l, j: (l, j, 0, 0))],
        out_specs=pl.BlockSpec((1, 1, 1, D_MODEL), lambda l, j: (l, j, 0, 0)),
        out_shape=jax.ShapeDtypeStruct((depth, 3, 1, D_MODEL), F32),
        compiler_params=_cparams("parallel", "parallel"),
        name="adaln_mod",
    )(c_col, w_ada, b4)


def _in_a_kernel(x_ref, g_ref, sc_ref, sh_ref, wa_ref, wg_ref, qkv0_ref, qkv1_ref, qkv2_ref, az_ref, mg_ref,
                 stage_sc):
    h = _modulated_norm(x_ref[...], g_ref[...], sc_ref[...], sh_ref[...]).astype(BF16)
    a = jnp.dot(h, wa_ref[...], preferred_element_type=F32)
    tm = a.shape[0]
    gw = 3 * A_WIDTH
    qkv0_ref[0] = a[:, :gw].astype(BF16)
    for g, out_ref in ((1, qkv1_ref), (2, qkv2_ref)):
        r = A_PATTERNS[g][1]
        for j in range(gw // LANES):
            stage_sc[j] = a[:, g * gw + j * LANES:g * gw + (j + 1) * LANES]
        for p in range(r):
            for j in range(gw // LANES):
                out_ref[p, :, j * LANES:(j + 1) * LANES] = stage_sc[j, pl.ds(p, tm // r, stride=r), :].astype(BF16)
    az_ref[...] = a[:, A_GROUPS * gw:]
    mg_ref[...] = jnp.dot(h, wg_ref[...], preferred_element_type=F32)


def _in_proj_a(x, g, scale, shift, w_a, w_g):
    s = x.shape[0]
    tm = ROW_TILE
    gw = 3 * A_WIDTH
    row = lambda i: (i, 0)
    phase_specs = [pl.BlockSpec((r, tm // r, gw), lambda i: (0, i, 0)) for _, r in A_PATTERNS]
    phase_shapes = [jax.ShapeDtypeStruct((r, s // r, gw), BF16) for _, r in A_PATTERNS]
    return pl.pallas_call(
        _in_a_kernel,
        grid=(s // tm,),
        in_specs=[pl.BlockSpec((tm, D_MODEL), row),
                  _const_spec((1, D_MODEL)), _const_spec((1, D_MODEL)), _const_spec((1, D_MODEL)),
                  _const_spec(w_a.shape), _const_spec(w_g.shape)],
        out_specs=phase_specs + [pl.BlockSpec((tm, A_WIDTH), row),
                                 pl.BlockSpec((tm, 2 * D_MODEL), row)],
        out_shape=phase_shapes + [jax.ShapeDtypeStruct((s, A_WIDTH), F32),
                                  jax.ShapeDtypeStruct((s, 2 * D_MODEL), F32)],
        scratch_shapes=[pltpu.VMEM((gw // LANES, tm, LANES), F32)],
        compiler_params=_cparams("parallel"),
        name="in_proj_a",
    )(x, g, scale, shift, w_a, w_g)


def _in_m_kernel(x_ref, g_ref, sc_ref, sh_ref, wc_ref, wzt_ref, qg_ref, kvg_ref, wuqt_ref, wk_ref, wvt_ref,
                 cost_ref, sint_ref, ck_ref, sk_ref, qt_ref, k_ref, vt_ref, mzt_ref):
    h = _modulated_norm(x_ref[...], g_ref[...], sc_ref[...], sh_ref[...]).astype(BF16)
    mzt_ref[...] = _nt_dot(wzt_ref[...], h)
    c = jnp.dot(h, wc_ref[...], preferred_element_type=F32)
    cq = c[:, :M_Q_LORA]
    ckv = c[:, M_Q_LORA:M_Q_LORA + M_KV_LORA]
    kr_a = c[:, M_Q_LORA + M_KV_LORA:M_Q_LORA + M_KV_LORA + M_QK_PAD]
    kr_b = c[:, M_Q_LORA + M_KV_LORA + M_QK_PAD:]
    cqn = (cq * lax.rsqrt(jnp.mean(cq * cq, axis=-1, keepdims=True) + EPS) * qg_ref[...]).astype(BF16)
    ckvn = (ckv * lax.rsqrt(jnp.mean(ckv * ckv, axis=-1, keepdims=True) + EPS) * kvg_ref[...]).astype(BF16)

    tm = cq.shape[0]
    half = M_ROPE // 2
    qt = _nt_dot(wuqt_ref[...], cqn).reshape(M_HEADS, M_QK_PAD, tm)
    cos = cost_ref[...][None]
    sin = sint_ref[...][None]
    t1 = qt[:, M_NOPE:M_NOPE + half]
    t2 = qt[:, M_NOPE + half:M_QK]
    qt_ref[:, :M_NOPE] = qt[:, :M_NOPE].astype(BF16)
    qt_ref[:, M_NOPE:M_NOPE + half] = (t1 * cos - t2 * sin).astype(BF16)
    qt_ref[:, M_NOPE + half:M_QK] = (t1 * sin + t2 * cos).astype(BF16)
    qt_ref[:, M_QK:] = qt[:, M_QK:].astype(BF16)

    kr = kr_a * ck_ref[...] + kr_b * sk_ref[...]
    k_all = jnp.dot(ckvn, wk_ref[...], preferred_element_type=F32)
    for hd in range(M_HEADS):
        k_ref[hd] = (k_all[:, hd * M_QK_PAD:(hd + 1) * M_QK_PAD] + kr).astype(BF16)

    vt = _nt_dot(wvt_ref[...], ckvn)
    vt_ref[:, 0, :M_V] = vt.reshape(M_HEADS, M_V, tm).astype(BF16)
    vt_ref[:, 0, M_V:] = jnp.ones((M_HEADS, M_V_EXT - M_V, tm), BF16)


def _in_proj_m(x, g, scale, shift, w_c, w_zt, qg, kvg, w_uqt, w_k, w_vt, cos_t, sin_t, cos_k, sin_k):
    s = x.shape[0]
    tm = ATT_TILE
    row = lambda i: (i, 0)
    col = lambda i: (0, i)
    return pl.pallas_call(
        _in_m_kernel,
        grid=(s // tm,),
        in_specs=[pl.BlockSpec((tm, D_MODEL), row),
                  _const_spec((1, D_MODEL)), _const_spec((1, D_MODEL)), _const_spec((1, D_MODEL)),
                  _const_spec(w_c.shape), _const_spec(w_zt.shape),
                  _const_spec(qg.shape), _const_spec(kvg.shape),
                  _const_spec(w_uqt.shape), _const_spec(w_k.shape), _const_spec(w_vt.shape),
                  pl.BlockSpec((M_ROPE // 2, tm), col), pl.BlockSpec((M_ROPE // 2, tm), col),
                  pl.BlockSpec((tm, M_QK_PAD), row), pl.BlockSpec((tm, M_QK_PAD), row)],
        out_specs=[pl.BlockSpec((M_HEADS, M_QK_PAD, tm), lambda i: (0, 0, i)),
                   pl.BlockSpec((M_HEADS, tm, M_QK_PAD), lambda i: (0, i, 0)),
                   pl.BlockSpec((M_HEADS, 1, M_V_EXT, tm), lambda i: (0, i, 0, 0)),
                   pl.BlockSpec((M_WIDTH, tm), col)],
        out_shape=[jax.ShapeDtypeStruct((M_HEADS, M_QK_PAD, s), BF16),
                   jax.ShapeDtypeStruct((M_HEADS, s, M_QK_PAD), BF16),
                   jax.ShapeDtypeStruct((M_HEADS, s // tm, M_V_EXT, tm), BF16),
                   jax.ShapeDtypeStruct((M_WIDTH, s), F32)],
        compiler_params=_cparams("parallel"),
        name="in_proj_m",
    )(x, g, scale, shift, w_c, w_zt, qg, kvg, w_uqt, w_k, w_vt, cos_t, sin_t, cos_k, sin_k)


def _dilated_kernel(q_ref, kp_ref, kc_ref, vp_ref, vc_ref, bias_ref, o_ref, lse_ref):
    pairs = A_WIDTH // LANES
    lo = lax.broadcasted_iota(jnp.int32, (1, LANES), 1) < A_HEAD_DIM
    q_all = q_ref[0] * jnp.asarray(A_HEAD_DIM ** -0.5, BF16)
    ones = jnp.ones((2 * A_BLOCK, LANES), BF16)
    zero = jnp.zeros((), BF16)

    scores, v_ext = [], []
    for pr in range(pairs):
        sl = slice(pr * LANES, (pr + 1) * LANES)
        k = jnp.concatenate([kp_ref[0, :, sl], kc_ref[0, :, sl]], axis=0)
        v = jnp.concatenate([vp_ref[0, :, sl], vc_ref[0, :, sl]], axis=0)
        v_ext.append(jnp.concatenate([v, ones], axis=1))
        q = q_all[:, sl]
        for half in range(2):
            qh = jnp.where(lo if half == 0 else jnp.logical_not(lo), q, zero)
            scores.append(_nt_dot(qh, k) + bias_ref[0, 2 * pr + half])

    probs, maxes = [], []
    for s in scores:
        m = jnp.max(s, axis=-1, keepdims=True)
        probs.append(jnp.exp(s - m).astype(BF16))
        maxes.append(m)

    for pr in range(pairs):
        sl = slice(pr * LANES, (pr + 1) * LANES)
        outs, lses = [], []
        for half in range(2):
            ol = jnp.dot(probs[2 * pr + half], v_ext[pr], preferred_element_type=F32)
            l = ol[:, LANES:]
            outs.append(ol[:, :LANES] / l)
            lses.append(maxes[2 * pr + half] + jnp.log(l))
        o_ref[0, :, sl] = jnp.where(lo, outs[0], outs[1])
        lse_ref[0, :, sl] = jnp.where(lo, lses[0], lses[1])


def _dilated_group(qkv, bias, g):
    r, l_sub, _ = qkv.shape
    blk = (1, A_BLOCK, A_WIDTH)
    bias_blk = (1,) + bias.shape[1:]

    def spec(t, prev):
        if prev:
            return pl.BlockSpec(blk, lambda p, n: (p, jnp.maximum(n - 1, 0), t))
        return pl.BlockSpec(blk, lambda p, n: (p, n, t))

    out_spec = pl.BlockSpec(blk, lambda p, n: (p, n, 0))
    return pl.pallas_call(
        _dilated_kernel,
        grid=(r, l_sub // A_BLOCK),
        in_specs=[spec(0, False), spec(1, True), spec(1, False), spec(2, True), spec(2, False),
                  pl.BlockSpec(bias_blk, lambda p, n: (jnp.where(n == 0, 1, 0), 0, 0, 0))],
        out_specs=[out_spec, out_spec],
        out_shape=[jax.ShapeDtypeStruct((r, l_sub, A_WIDTH), F32)] * 2,
        compiler_params=_cparams("parallel", "parallel"),
        name=f"dilated_g{g}",
    )(qkv, qkv, qkv, qkv, qkv, bias)


def _t5_bucket(dist):
    exact = REL_BUCKETS // 2
    d = jnp.maximum(dist, 1).astype(F32)
    large = exact + (jnp.log(d / exact) / math.log(REL_MAX_DIST / exact) * (REL_BUCKETS - exact)).astype(jnp.int32)
    large = jnp.minimum(large, REL_BUCKETS - 1)
    return jnp.where(dist < exact, dist, large)


def _dilated_bias(rel_bias, g, window, r):
    qi = jnp.arange(A_BLOCK)[:, None]
    ki = jnp.arange(2 * A_BLOCK)[None, :]
    j = qi + A_BLOCK - ki
    band = (j >= 0) & (j <= window // r)
    tab = rel_bias[:, g * A_HEADS:(g + 1) * A_HEADS].astype(F32)
    pick = jax.nn.one_hot(_t5_bucket(jnp.maximum(j, 0) * r), REL_BUCKETS, dtype=F32)
    b = jnp.einsum("qkb,bh->hqk", pick, tab, precision=lax.Precision.HIGHEST)
    b = jnp.where(band[None], b, NEG_INF)
    first = jnp.where(ki[None] < A_BLOCK, NEG_INF, b)
    return jnp.stack([b, first])


def _mla_kernel(qt_ref, qn_ref, k_ref, vt_ref, mzt_ref, o_ref, s0_sc, s1_sc, mt0_sc, mt1_sc, m_sc, acc_sc):
    i = pl.program_id(1)
    last = pl.num_programs(1) - 1
    tk, tq, cw = ATT_TILE, MLA_Q_TILE, MLA_CHUNK
    bufs = ((s0_sc, mt0_sc), (s1_sc, mt1_sc))
    m_sc[...] = jnp.full(m_sc.shape, NEG_INF, F32)
    acc_sc[...] = jnp.zeros(acc_sc.shape, F32)

    def scores(kt, diagonal_offset, q_ref=qt_ref):
        k = k_ref[0, pl.ds(pl.multiple_of(kt * tk, tk), tk), :]
        s = jnp.dot(k, q_ref[0], preferred_element_type=F32)
        if diagonal_offset is not None:
            kpos = lax.broadcasted_iota(jnp.int32, (tk, tq), 0) + diagonal_offset
            qpos = lax.broadcasted_iota(jnp.int32, (tk, tq), 1)
            s = jnp.where(kpos <= qpos, s, NEG_INF)
        return s

    def fill(slot, s):
        s_ref, mt_ref = bufs[slot]
        s_ref[:, :tq] = s
        mt_ref[...] = jnp.max(s, axis=0, keepdims=True)

    def accumulate(kt, slot):
        s_ref, mt_ref = bufs[slot]
        for c in range(tq // cw):
            cols = slice(c * cw, (c + 1) * cw)
            m_prev = m_sc[:, cols]
            m_new = jnp.maximum(m_prev, mt_ref[:, cols])
            alpha = jnp.exp2(m_prev - m_new)
            p = jnp.exp2((s_ref[:, cols] - m_new).astype(BF16))
            acc_sc[:, cols] = alpha * acc_sc[:, cols] + jnp.dot(vt_ref[0, kt], p, preferred_element_type=F32)
            m_sc[:, cols] = m_new

    def stage(kt, slot, refill, diagonal_offset=None):
        if refill == "ahead":
            s_next = scores(kt + 2, diagonal_offset)
        elif refill == "next":
            s_next = scores(slot, None, qn_ref)
        accumulate(kt, slot)
        if refill is not None:
            fill(slot, s_next)

    def pair(pr, refill="ahead", offsets=(None, None)):
        stage(2 * pr, 0, refill, offsets[0])
        stage(2 * pr + 1, 1, refill, offsets[1])

    @pl.when(i == 0)
    def _():
        fill(0, scores(0, 0))
        fill(1, scores(1, tk))

    @pl.when(i > 0)
    def _():
        def body(quad, carry):
            pair(2 * quad)
            pair(2 * quad + 1)
            return carry

        lax.fori_loop(0, (i - 1) // 2, body, 0)

        @pl.when((i - 1) % 2 == 1)
        def _():
            pair(i - 2)

        pair(i - 1, offsets=(0, tk))

    @pl.when(i < last)
    def _():
        pair(i, refill="next")

    @pl.when(i == last)
    def _():
        pair(i, refill=None)

    mz = mzt_ref[...]
    gate = mz * (1.0 / (1.0 + jnp.exp(-mz)))
    acc = acc_sc[...]
    o_ref[...] = (acc[:M_V] / acc[M_V:M_V + 1] * gate).astype(BF16)


def _mla_attention(qt, k, vt, mzt):
    s = k.shape[1]
    tk, tq = ATT_TILE, MLA_Q_TILE
    nq = s // tq
    return pl.pallas_call(
        _mla_kernel,
        grid=(M_HEADS, nq),
        in_specs=[pl.BlockSpec((1, M_QK_PAD, tq), lambda h, i: (h, 0, i)),
                  pl.BlockSpec((1, M_QK_PAD, tq), lambda h, i: (h, 0, jnp.minimum(i + 1, nq - 1))),
                  pl.BlockSpec((1, s, M_QK_PAD), lambda h, i: (h, 0, 0)),
                  pl.BlockSpec((1, s // tk, M_V_EXT, tk), lambda h, i: (h, 0, 0, 0)),
                  pl.BlockSpec((M_V, tq), lambda h, i: (h, i))],
        out_specs=pl.BlockSpec((M_V, tq), lambda h, i: (h, i)),
        out_shape=jax.ShapeDtypeStruct((M_WIDTH, s), BF16),
        scratch_shapes=[pltpu.VMEM((tk, tq + LANES), F32), pltpu.VMEM((tk, tq + LANES), F32),
                        pltpu.VMEM((1, tq), F32), pltpu.VMEM((1, tq), F32),
                        pltpu.VMEM((1, tq), F32), pltpu.VMEM((M_V_EXT, tq), F32)],
        compiler_params=_cparams("arbitrary", "arbitrary"),
        name="mla_attention",
    )(qt, qt, k, vt, mzt)


def _out_kernel(final, x_ref, o0_ref, o1_ref, o2_ref, l0_ref, l1_ref, l2_ref, az_ref, ymt_ref, mg_ref, gate_ref,
                wa_ref, wb_ref, wo_ref, fg_ref, out_ref, o1_sc, o2_sc, l1_sc, l2_sc):
    tm = x_ref.shape[0]
    nj = A_WIDTH // LANES
    for src, dst in ((o1_ref, o1_sc), (o2_ref, o2_sc), (l1_ref, l1_sc), (l2_ref, l2_sc)):
        r = src.shape[0]
        for p in range(r):
            for j in range(nj):
                dst[j, pl.ds(p, tm // r, stride=r), :] = src[p, :, j * LANES:(j + 1) * LANES]

    def rows(sc):
        return jnp.concatenate([sc[j] for j in range(nj)], axis=1)

    l0, l1, l2 = l0_ref[0], rows(l1_sc), rows(l2_sc)
    mx = jnp.maximum(jnp.maximum(l0, l1), l2)
    e0, e1, e2 = jnp.exp(l0 - mx), jnp.exp(l1 - mx), jnp.exp(l2 - mx)
    mix = (o0_ref[0] * e0 + rows(o1_sc) * e1 + rows(o2_sc) * e2) / (e0 + e1 + e2)
    az = az_ref[...]
    y_a = (mix * (az * (1.0 / (1.0 + jnp.exp(-az))))).astype(BF16)
    t_a = jnp.dot(y_a, wa_ref[...], preferred_element_type=F32)
    t_b = lax.dot_general(ymt_ref[...], wb_ref[...], (((0,), (0,)), ((), ())), preferred_element_type=F32)
    mg = mg_ref[...]
    g_a = 1.0 / (1.0 + jnp.exp(-mg[:, :D_MODEL]))
    g_m = 1.0 / (1.0 + jnp.exp(-mg[:, D_MODEL:]))
    merged = (g_a * t_a + g_m * t_b).astype(BF16)
    y = x_ref[...] + gate_ref[...] * jnp.dot(merged, wo_ref[...], preferred_element_type=F32)
    if final:
        y = y * lax.rsqrt(jnp.mean(y * y, axis=-1, keepdims=True) + EPS) * fg_ref[...]
    out_ref[...] = y


def _out_proj(x, o_groups, lse_groups, a_z, ymt, mg, gate, w_a, w_b, w_o, final_g, final):
    s = x.shape[0]
    tm = ROW_TILE
    row = lambda i: (i, 0)
    phase_specs = [pl.BlockSpec((r, tm // r, A_WIDTH), lambda i: (0, i, 0)) for _, r in A_PATTERNS]
    return pl.pallas_call(
        functools.partial(_out_kernel, final),
        grid=(s // tm,),
        in_specs=[pl.BlockSpec((tm, D_MODEL), row)] + phase_specs * 2
                 + [pl.BlockSpec((tm, A_WIDTH), row),
                    pl.BlockSpec((M_WIDTH, tm), lambda i: (0, i)),
                    pl.BlockSpec((tm, 2 * D_MODEL), row),
                    _const_spec((1, D_MODEL)),
                    _const_spec(w_a.shape), _const_spec(w_b.shape), _const_spec(w_o.shape),
                    _const_spec((1, D_MODEL))],
        out_specs=pl.BlockSpec((tm, D_MODEL), row),
        out_shape=jax.ShapeDtypeStruct((s, D_MODEL), F32),
        scratch_shapes=[pltpu.VMEM((A_WIDTH // LANES, tm, LANES), F32)] * 4,
        compiler_params=_cparams("parallel"),
        name="out_proj",
    )(x, *o_groups, *lse_groups, a_z, ymt, mg, gate, w_a, w_b, w_o, final_g)


def _prep_layer(w_in, w_uq, w_ukv, w_out_a, w_out_b, w_o):
    o_az = A_QKV
    o_cq = o_az + A_WIDTH
    o_ckv = o_cq + M_Q_LORA
    o_kr = o_ckv + M_KV_LORA
    o_mz = o_kr + M_ROPE
    o_mg = o_mz + M_WIDTH
    half = M_ROPE // 2
    w_qkv = w_in[:, :A_QKV].reshape(D_MODEL, 3, A_GROUPS, A_WIDTH).transpose(0, 2, 1, 3).reshape(D_MODEL, A_QKV)
    w_a = jnp.concatenate([w_qkv, w_in[:, o_az:o_cq]], axis=1).astype(BF16)
    w_g = w_in[:, o_mg:].astype(BF16)
    w_kr = w_in[:, o_kr:o_mz]
    z_lo = jnp.zeros((D_MODEL, M_NOPE), F32)
    z_hi = jnp.zeros((D_MODEL, M_QK_PAD - M_QK), F32)
    kr_a = jnp.concatenate([z_lo, w_kr, z_hi], axis=1)
    kr_b = jnp.concatenate([z_lo, w_kr[:, half:], w_kr[:, :half], z_hi], axis=1)
    w_c = jnp.concatenate([w_in[:, o_cq:o_kr], kr_a, kr_b], axis=1).astype(BF16)
    w_zt = w_in[:, o_mz:o_mg].T.astype(BF16)
    scale = M_QK ** -0.5 * math.log2(math.e)
    uq = (w_uq * scale).reshape(M_Q_LORA, M_HEADS, M_QK)
    uq = jnp.pad(uq, ((0, 0), (0, 0), (0, M_QK_PAD - M_QK)))
    w_uqt = uq.reshape(M_Q_LORA, M_HEADS * M_QK_PAD).T.astype(BF16)
    ukv = w_ukv.reshape(M_KV_LORA, M_HEADS, M_NOPE + M_V)
    w_k = jnp.pad(ukv[:, :, :M_NOPE], ((0, 0), (0, 0), (0, M_QK_PAD - M_NOPE)))
    w_k = w_k.reshape(M_KV_LORA, M_HEADS * M_QK_PAD).astype(BF16)
    w_vt = ukv[:, :, M_NOPE:].reshape(M_KV_LORA, M_WIDTH).T.astype(BF16)
    return (w_a, w_g, w_c, w_zt, w_uqt, w_k, w_vt,
            w_out_a.astype(BF16), w_out_b.astype(BF16), w_o.astype(BF16))


def _rope_tables(positions):
    half = M_ROPE // 2
    inv_freq = 1.0 / (ROPE_THETA ** (jnp.arange(0, M_ROPE, 2, dtype=F32) / M_ROPE))
    ang = positions.astype(F32)[:, None] * inv_freq
    cos, sin = jnp.cos(ang), jnp.sin(ang)
    s = positions.shape[0]
    z_lo = jnp.zeros((s, M_NOPE), F32)
    z_hi = jnp.zeros((s, M_QK_PAD - M_QK), F32)
    cos_k = jnp.concatenate([z_lo, cos, cos, z_hi], axis=1)
    sin_k = jnp.concatenate([z_lo, -sin, sin, z_hi], axis=1)
    return cos.T, sin.T, cos_k, sin_k


def kernel(x, c, positions, w_ada, b_ada, norm_g, w_in, q_norm_g, w_uq, kv_norm_g, w_ukv, w_out_a, w_out_b, w_o,
           rel_bias, final_norm_g):
    batch, s, _ = x.shape
    assert batch == 1 and s % (A_BLOCK * A_PATTERNS[-1][1]) == 0 and s % MLA_Q_TILE == 0 and s % ROW_TILE == 0
    depth = w_ada.shape[0]
    xs = x[0]
    mod = _modulation(c, w_ada, b_ada)
    cos_t, sin_t, cos_k, sin_k = _rope_tables(positions[0])
    biases = [_dilated_bias(rel_bias, g, w, r) for g, (w, r) in enumerate(A_PATTERNS)]
    final_g = final_norm_g.reshape(1, D_MODEL)
    for l in range(depth):
        (w_a, w_g, w_c, w_zt, w_uqt, w_k, w_vt, wo_a, wo_b, wo) = _prep_layer(
            w_in[l], w_uq[l], w_ukv[l], w_out_a[l], w_out_b[l], w_o[l])
        shift, scale, gate = mod[l, 0], mod[l, 1], mod[l, 2]
        g = norm_g[l].reshape(1, D_MODEL)
        qkv0, qkv1, qkv2, a_z, mg = _in_proj_a(xs, g, scale, shift, w_a, w_g)
        qt, k, vt, mzt = _in_proj_m(xs, g, scale, shift, w_c, w_zt,
                                    q_norm_g[l].reshape(1, M_Q_LORA), kv_norm_g[l].reshape(1, M_KV_LORA),
                                    w_uqt, w_k, w_vt, cos_t, sin_t, cos_k, sin_k)
        outs = [_dilated_group(qkv, biases[gi], gi) for gi, qkv in enumerate((qkv0, qkv1, qkv2))]
        ymt = _mla_attention(qt, k, vt, mzt)
        xs = _out_proj(xs, [o for o, _ in outs], [ls for _, ls in outs], a_z, ymt, mg, gate,
                       wo_a, wo_b, wo, final_g, l == depth - 1)
    return xs[None]
```

```python
import functools
import math

import jax
import jax.numpy as jnp
from jax import lax
from jax.experimental import pallas as pl
from jax.experimental.pallas import tpu as pltpu

F32 = jnp.float32
BF16 = jnp.bfloat16

D_MODEL = 1024
A_PATTERNS = ((128, 1), (512, 4), (2048, 16))
A_GROUPS = 3
A_HEADS = 8
A_HEAD_DIM = 64
A_WIDTH = A_HEADS * A_HEAD_DIM
A_BLOCK = 128
A_QKV = 3 * A_GROUPS * A_WIDTH
M_HEADS = 16
M_Q_LORA = 256
M_KV_LORA = 128
M_NOPE = 64
M_ROPE = 32
M_V = 64
M_QK = M_NOPE + M_ROPE
M_QK_PAD = 128
M_WIDTH = M_HEADS * M_V
M_V_EXT = M_V + 16
ROPE_THETA = 10000.0
REL_BUCKETS = 32
REL_MAX_DIST = 2048
EPS = 1e-6
NEG_INF = -1e30
LANES = 128

ROW_TILE = 512
ATT_TILE = 512
MLA_Q_TILE = 2 * ATT_TILE
MLA_CHUNK = 256
DIL_BLOCKS = 4
VMEM_LIMIT = 56 * 1024 * 1024


def _cparams(*sem):
    return pltpu.CompilerParams(dimension_semantics=sem, vmem_limit_bytes=VMEM_LIMIT)


def _const_spec(shape):
    nd = len(shape)
    return pl.BlockSpec(shape, lambda *_: (0,) * nd, pipeline_mode=pl.Buffered(1))


def _nt_dot(a, b):
    return lax.dot_general(a, b, (((1,), (1,)), ((), ())), preferred_element_type=F32)


def _sigmoid(x):
    return 0.5 * jnp.tanh(0.5 * x) + 0.5


def _modulated_norm(x, g, scale, shift):
    y = x * lax.rsqrt(jnp.mean(x * x, axis=-1, keepdims=True) + EPS)
    return (y * g) * (1.0 + scale) + shift


def _mod_kernel(c_ref, w_ref, b_ref, o_ref):
    c = c_ref[...]
    c_act = c * _sigmoid(c)
    o_ref[0, 0] = jnp.sum(c_act * w_ref[0], axis=0, keepdims=True) + b_ref[0, 0]


def _modulation(c, w_ada, b_ada):
    depth = w_ada.shape[0]
    c_col = c.reshape(D_MODEL, 1)
    b4 = b_ada.reshape(depth, 3, 1, D_MODEL)
    return pl.pallas_call(
        _mod_kernel,
        grid=(depth, 3),
        in_specs=[pl.BlockSpec((D_MODEL, 1), lambda l, j: (0, 0)),
                  pl.BlockSpec((1, D_MODEL, D_MODEL), lambda l, j: (l, 0, j)),
                  pl.BlockSpec((1, 1, 1, D_MODEL), lambda l, j: (l, j, 0, 0))],
        out_specs=pl.BlockSpec((1, 1, 1, D_MODEL), lambda l, j: (l, j, 0, 0)),
        out_shape=jax.ShapeDtypeStruct((depth, 3, 1, D_MODEL), F32),
        compiler_params=_cparams("parallel", "parallel"),
        name="adaln_mod",
    )(c_col, w_ada, b4)


def _in_a_kernel(x_ref, g_ref, sc_ref, sh_ref, wa_ref, wg_ref, qkv0_ref, qkv1_ref, qkv2_ref, az_ref, mg_ref,
                 stage_sc):
    h = _modulated_norm(x_ref[...], g_ref[...], sc_ref[...], sh_ref[...]).astype(BF16)
    a = jnp.dot(h, wa_ref[...], preferred_element_type=F32)
    tm = a.shape[0]
    gw = 3 * A_WIDTH
    qkv0_ref[0] = a[:, :gw].astype(BF16)
    for g, out_ref in ((1, qkv1_ref), (2, qkv2_ref)):
        r = A_PATTERNS[g][1]
        for j in range(gw // LANES):
            stage_sc[j] = a[:, g * gw + j * LANES:g * gw + (j + 1) * LANES]
        for p in range(r):
            for j in range(gw // LANES):
                out_ref[p, :, j * LANES:(j + 1) * LANES] = stage_sc[j, pl.ds(p, tm // r, stride=r), :].astype(BF16)
    az_ref[...] = a[:, A_GROUPS * gw:]
    mg_ref[...] = jnp.dot(h, wg_ref[...], preferred_element_type=F32)


def _in_proj_a(x, g, scale, shift, w_a, w_g):
    s = x.shape[0]
    tm = ROW_TILE
    gw = 3 * A_WIDTH
    row = lambda i: (i, 0)
    phase_specs = [pl.BlockSpec((r, tm // r, gw), lambda i: (0, i, 0)) for _, r in A_PATTERNS]
    phase_shapes = [jax.ShapeDtypeStruct((r, s // r, gw), BF16) for _, r in A_PATTERNS]
    return pl.pallas_call(
        _in_a_kernel,
        grid=(s // tm,),
        in_specs=[pl.BlockSpec((tm, D_MODEL), row),
                  _const_spec((1, D_MODEL)), _const_spec((1, D_MODEL)), _const_spec((1, D_MODEL)),
                  _const_spec(w_a.shape), _const_spec(w_g.shape)],
        out_specs=phase_specs + [pl.BlockSpec((tm, A_WIDTH), row),
                                 pl.BlockSpec((tm, 2 * D_MODEL), row)],
        out_shape=phase_shapes + [jax.ShapeDtypeStruct((s, A_WIDTH), F32),
                                  jax.ShapeDtypeStruct((s, 2 * D_MODEL), F32)],
        scratch_shapes=[pltpu.VMEM((gw // LANES, tm, LANES), F32)],
        compiler_params=_cparams("parallel"),
        name="in_proj_a",
    )(x, g, scale, shift, w_a, w_g)


def _in_m_kernel(x_ref, g_ref, sc_ref, sh_ref, wc_ref, wzt_ref, qg_ref, kvg_ref, wuqt_ref, wk_ref, wvt_ref,
                 cost_ref, sint_ref, ck_ref, sk_ref, qt_ref, k_ref, vt_ref, mzt_ref):
    h = _modulated_norm(x_ref[...], g_ref[...], sc_ref[...], sh_ref[...]).astype(BF16)
    mzt_ref[...] = _nt_dot(wzt_ref[...], h)
    c = jnp.dot(h, wc_ref[...], preferred_element_type=F32)
    cq = c[:, :M_Q_LORA]
    ckv = c[:, M_Q_LORA:M_Q_LORA + M_KV_LORA]
    kr_a = c[:, M_Q_LORA + M_KV_LORA:M_Q_LORA + M_KV_LORA + M_QK_PAD]
    kr_b = c[:, M_Q_LORA + M_KV_LORA + M_QK_PAD:]
    cqn = (cq * lax.rsqrt(jnp.mean(cq * cq, axis=-1, keepdims=True) + EPS) * qg_ref[...]).astype(BF16)
    ckvn = (ckv * lax.rsqrt(jnp.mean(ckv * ckv, axis=-1, keepdims=True) + EPS) * kvg_ref[...]).astype(BF16)

    tm = cq.shape[0]
    half = M_ROPE // 2
    qt = _nt_dot(wuqt_ref[...], cqn).reshape(M_HEADS, M_QK_PAD, tm)
    cos = cost_ref[...][None]
    sin = sint_ref[...][None]
    t1 = qt[:, M_NOPE:M_NOPE + half]
    t2 = qt[:, M_NOPE + half:M_QK]
    qt_ref[:, :M_NOPE] = qt[:, :M_NOPE].astype(BF16)
    qt_ref[:, M_NOPE:M_NOPE + half] = (t1 * cos - t2 * sin).astype(BF16)
    qt_ref[:, M_NOPE + half:M_QK] = (t1 * sin + t2 * cos).astype(BF16)
    qt_ref[:, M_QK:] = qt[:, M_QK:].astype(BF16)

    kr = kr_a * ck_ref[...] + kr_b * sk_ref[...]
    k_all = jnp.dot(ckvn, wk_ref[...], preferred_element_type=F32)
    for hd in range(M_HEADS):
        k_ref[hd] = (k_all[:, hd * M_QK_PAD:(hd + 1) * M_QK_PAD] + kr).astype(BF16)

    vt = _nt_dot(wvt_ref[...], ckvn)
    vt_ref[:, 0, :M_V] = vt.reshape(M_HEADS, M_V, tm).astype(BF16)
    vt_ref[:, 0, M_V:] = jnp.ones((M_HEADS, M_V_EXT - M_V, tm), BF16)


def _in_proj_m(x, g, scale, shift, w_c, w_zt, qg, kvg, w_uqt, w_k, w_vt, cos_t, sin_t, cos_k, sin_k):
    s = x.shape[0]
    tm = ATT_TILE
    row = lambda i: (i, 0)
    col = lambda i: (0, i)
    return pl.pallas_call(
        _in_m_kernel,
        grid=(s // tm,),
        in_specs=[pl.BlockSpec((tm, D_MODEL), row),
                  _const_spec((1, D_MODEL)), _const_spec((1, D_MODEL)), _const_spec((1, D_MODEL)),
                  _const_spec(w_c.shape), _const_spec(w_zt.shape),
                  _const_spec(qg.shape), _const_spec(kvg.shape),
                  _const_spec(w_uqt.shape), _const_spec(w_k.shape), _const_spec(w_vt.shape),
                  pl.BlockSpec((M_ROPE // 2, tm), col), pl.BlockSpec((M_ROPE // 2, tm), col),
                  pl.BlockSpec((tm, M_QK_PAD), row), pl.BlockSpec((tm, M_QK_PAD), row)],
        out_specs=[pl.BlockSpec((M_HEADS, M_QK_PAD, tm), lambda i: (0, 0, i)),
                   pl.BlockSpec((M_HEADS, tm, M_QK_PAD), lambda i: (0, i, 0)),
                   pl.BlockSpec((M_HEADS, 1, M_V_EXT, tm), lambda i: (0, i, 0, 0)),
                   pl.BlockSpec((M_WIDTH, tm), col)],
        out_shape=[jax.ShapeDtypeStruct((M_HEADS, M_QK_PAD, s), BF16),
                   jax.ShapeDtypeStruct((M_HEADS, s, M_QK_PAD), BF16),
                   jax.ShapeDtypeStruct((M_HEADS, s // tm, M_V_EXT, tm), BF16),
                   jax.ShapeDtypeStruct((M_WIDTH, s), F32)],
        compiler_params=_cparams("parallel"),
        name="in_proj_m",
    )(x, g, scale, shift, w_c, w_zt, qg, kvg, w_uqt, w_k, w_vt, cos_t, sin_t, cos_k, sin_k)


def _dilated_kernel(q_ref, kp_ref, kc_ref, vp_ref, vc_ref, bias0_ref, bias_ref, o_ref, lse_ref):
    pairs = A_WIDTH // LANES
    lo = lax.broadcasted_iota(jnp.int32, (1, LANES), 1) < A_HEAD_DIM
    ones = jnp.ones((2 * A_BLOCK, LANES), BF16)
    zero = jnp.zeros((), BF16)

    for b in range(DIL_BLOCKS):
        rows = slice(b * A_BLOCK, (b + 1) * A_BLOCK)
        b_ref = bias0_ref if b == 0 else bias_ref
        q_all = q_ref[0, rows] * jnp.asarray(A_HEAD_DIM ** -0.5, BF16)

        def keys(prev_ref, cur_ref, sl):
            if b == 0:
                return jnp.concatenate([prev_ref[0, :, sl], cur_ref[0, :A_BLOCK, sl]], axis=0)
            return cur_ref[0, (b - 1) * A_BLOCK:(b + 1) * A_BLOCK, sl]

        scores, v_ext = [], []
        for pr in range(pairs):
            sl = slice(pr * LANES, (pr + 1) * LANES)
            k = keys(kp_ref, kc_ref, sl)
            v_ext.append(jnp.concatenate([keys(vp_ref, vc_ref, sl), ones], axis=1))
            q = q_all[:, sl]
            for half in range(2):
                qh = jnp.where(lo if half == 0 else jnp.logical_not(lo), q, zero)
                scores.append(_nt_dot(qh, k) + b_ref[0, 2 * pr + half])

        probs, maxes = [], []
        for s in scores:
            m = jnp.max(s, axis=-1, keepdims=True)
            probs.append(jnp.exp(s - m).astype(BF16))
            maxes.append(m)

        for pr in range(pairs):
            sl = slice(pr * LANES, (pr + 1) * LANES)
            outs, lses = [], []
            for half in range(2):
                ol = jnp.dot(probs[2 * pr + half], v_ext[pr], preferred_element_type=F32)
                l = ol[:, LANES:]
                outs.append(ol[:, :LANES] / l)
                lses.append(maxes[2 * pr + half] + jnp.log(l))
            o_ref[0, rows, sl] = jnp.where(lo, outs[0], outs[1])
            lse_ref[0, rows, sl] = jnp.where(lo, lses[0], lses[1])


def _dilated_group(qkv, bias, g):
    r, l_sub, _ = qkv.shape
    rows = DIL_BLOCKS * A_BLOCK
    blk = (1, rows, A_WIDTH)
    bias_blk = (1,) + bias.shape[1:]

    def cur(t):
        return pl.BlockSpec(blk, lambda p, n: (p, n, t))

    def prev(t):
        return pl.BlockSpec((1, A_BLOCK, A_WIDTH), lambda p, n: (p, jnp.maximum(DIL_BLOCKS * n - 1, 0), t))

    out_spec = pl.BlockSpec(blk, lambda p, n: (p, n, 0))
    return pl.pallas_call(
        _dilated_kernel,
        grid=(r, l_sub // rows),
        in_specs=[cur(0), prev(1), cur(1), prev(2), cur(2),
                  pl.BlockSpec(bias_blk, lambda p, n: (jnp.where(n == 0, 1, 0), 0, 0, 0)),
                  pl.BlockSpec(bias_blk, lambda p, n: (0, 0, 0, 0))],
        out_specs=[out_spec, out_spec],
        out_shape=[jax.ShapeDtypeStruct((r, l_sub, A_WIDTH), F32)] * 2,
        compiler_params=_cparams("parallel", "parallel"),
        name=f"dilated_g{g}",
    )(qkv, qkv, qkv, qkv, qkv, bias, bias)


def _t5_bucket(dist):
    exact = REL_BUCKETS // 2
    d = jnp.maximum(dist, 1).astype(F32)
    large = exact + (jnp.log(d / exact) / math.log(REL_MAX_DIST / exact) * (REL_BUCKETS - exact)).astype(jnp.int32)
    large = jnp.minimum(large, REL_BUCKETS - 1)
    return jnp.where(dist < exact, dist, large)


def _dilated_bias(rel_bias, g, window, r):
    qi = jnp.arange(A_BLOCK)[:, None]
    ki = jnp.arange(2 * A_BLOCK)[None, :]
    j = qi + A_BLOCK - ki
    band = (j >= 0) & (j <= window // r)
    tab = rel_bias[:, g * A_HEADS:(g + 1) * A_HEADS].astype(F32)
    pick = jax.nn.one_hot(_t5_bucket(jnp.maximum(j, 0) * r), REL_BUCKETS, dtype=F32)
    b = jnp.einsum("qkb,bh->hqk", pick, tab, precision=lax.Precision.HIGHEST)
    b = jnp.where(band[None], b, NEG_INF)
    first = jnp.where(ki[None] < A_BLOCK, NEG_INF, b)
    return jnp.stack([b, first])


def _mla_kernel(qt_ref, qn_ref, k_ref, vt_ref, mzt_ref, o_ref, s0_sc, s1_sc, mt0_sc, mt1_sc, m_sc, acc_sc):
    i = pl.program_id(1)
    last = pl.num_programs(1) - 1
    tk, tq, cw = ATT_TILE, MLA_Q_TILE, MLA_CHUNK
    bufs = ((s0_sc, mt0_sc), (s1_sc, mt1_sc))
    m_sc[...] = jnp.full(m_sc.shape, NEG_INF, F32)
    acc_sc[...] = jnp.zeros(acc_sc.shape, F32)

    def scores(kt, diagonal_offset, q_ref=qt_ref):
        k = k_ref[0, pl.ds(pl.multiple_of(kt * tk, tk), tk), :]
        s = jnp.dot(k, q_ref[0], preferred_element_type=F32)
        if diagonal_offset is not None:
            kpos = lax.broadcasted_iota(jnp.int32, (tk, tq), 0) + diagonal_offset
            qpos = lax.broadcasted_iota(jnp.int32, (tk, tq), 1)
            s = jnp.where(kpos <= qpos, s, NEG_INF)
        return s

    def fill(slot, s):
        s_ref, mt_ref = bufs[slot]
        s_ref[...] = s
        mt_ref[...] = jnp.max(s, axis=0, keepdims=True)

    def accumulate(kt, slot):
        s_ref, mt_ref = bufs[slot]
        for c in range(tq // cw):
            cols = slice(c * cw, (c + 1) * cw)
            m_prev = m_sc[:, cols]
            m_new = jnp.maximum(m_prev, mt_ref[:, cols])
            alpha = jnp.exp2(m_prev - m_new)
            p = jnp.exp2((s_ref[:, cols] - m_new).astype(BF16))
            acc_sc[:, cols] = alpha * acc_sc[:, cols] + jnp.dot(vt_ref[0, kt], p, preferred_element_type=F32)
            m_sc[:, cols] = m_new

    def stage(kt, slot, refill, diagonal_offset=None):
        if refill == "ahead":
            s_next = scores(kt + 2, diagonal_offset)
        elif refill == "next":
            s_next = scores(slot, None, qn_ref)
        accumulate(kt, slot)
        if refill is not None:
            fill(slot, s_next)

    def pair(pr, refill="ahead", offsets=(None, None)):
        stage(2 * pr, 0, refill, offsets[0])
        stage(2 * pr + 1, 1, refill, offsets[1])

    @pl.when(i == 0)
    def _():
        fill(0, scores(0, 0))
        fill(1, scores(1, tk))

    @pl.when(i > 0)
    def _():
        def body(quad, carry):
            pair(2 * quad)
            pair(2 * quad + 1)
            return carry

        lax.fori_loop(0, (i - 1) // 2, body, 0)

        @pl.when((i - 1) % 2 == 1)
        def _():
            pair(i - 2)

        pair(i - 1, offsets=(0, tk))

    @pl.when(i < last)
    def _():
        pair(i, refill="next")

    @pl.when(i == last)
    def _():
        pair(i, refill=None)

    mz = mzt_ref[...]
    gate = mz * _sigmoid(mz)
    acc = acc_sc[...]
    o_ref[...] = (acc[:M_V] / acc[M_V:M_V + 1] * gate).astype(BF16)


def _mla_attention(qt, k, vt, mzt):
    s = k.shape[1]
    tk, tq = ATT_TILE, MLA_Q_TILE
    nq = s // tq
    return pl.pallas_call(
        _mla_kernel,
        grid=(M_HEADS, nq),
        in_specs=[pl.BlockSpec((1, M_QK_PAD, tq), lambda h, i: (h, 0, i)),
                  pl.BlockSpec((1, M_QK_PAD, tq), lambda h, i: (h, 0, jnp.minimum(i + 1, nq - 1))),
                  pl.BlockSpec((1, s, M_QK_PAD), lambda h, i: (h, 0, 0)),
                  pl.BlockSpec((1, s // tk, M_V_EXT, tk), lambda h, i: (h, 0, 0, 0)),
                  pl.BlockSpec((M_V, tq), lambda h, i: (h, i))],
        out_specs=pl.BlockSpec((M_V, tq), lambda h, i: (h, i)),
        out_shape=jax.ShapeDtypeStruct((M_WIDTH, s), BF16),
        scratch_shapes=[pltpu.VMEM((tk, tq), F32), pltpu.VMEM((tk, tq), F32),
                        pltpu.VMEM((1, tq), F32), pltpu.VMEM((1, tq), F32),
                        pltpu.VMEM((1, tq), F32), pltpu.VMEM((M_V_EXT, tq), F32)],
        compiler_params=_cparams("arbitrary", "arbitrary"),
        name="mla_attention",
    )(qt, qt, k, vt, mzt)


def _out_kernel(final, x_ref, o0_ref, o1_ref, o2_ref, l0_ref, l1_ref, l2_ref, az_ref, ymt_ref, mg_ref, gate_ref,
                wa_ref, wb_ref, wo_ref, fg_ref, out_ref, o1_sc, o2_sc, l1_sc, l2_sc):
    tm = x_ref.shape[0]
    nj = A_WIDTH // LANES
    for src, dst in ((o1_ref, o1_sc), (o2_ref, o2_sc), (l1_ref, l1_sc), (l2_ref, l2_sc)):
        r = src.shape[0]
        for p in range(r):
            for j in range(nj):
                dst[j, pl.ds(p, tm // r, stride=r), :] = src[p, :, j * LANES:(j + 1) * LANES]

    def rows(sc):
        return jnp.concatenate([sc[j] for j in range(nj)], axis=1)

    l0, l1, l2 = l0_ref[0], rows(l1_sc), rows(l2_sc)
    mx = jnp.maximum(jnp.maximum(l0, l1), l2)
    e0, e1, e2 = jnp.exp(l0 - mx), jnp.exp(l1 - mx), jnp.exp(l2 - mx)
    mix = (o0_ref[0] * e0 + rows(o1_sc) * e1 + rows(o2_sc) * e2) / (e0 + e1 + e2)
    az = az_ref[...]
    y_a = (mix * (az * _sigmoid(az))).astype(BF16)
    t_a = jnp.dot(y_a, wa_ref[...], preferred_element_type=F32)
    t_b = lax.dot_general(ymt_ref[...], wb_ref[...], (((0,), (0,)), ((), ())), preferred_element_type=F32)
    mg = mg_ref[...]
    g_a = _sigmoid(mg[:, :D_MODEL])
    g_m = _sigmoid(mg[:, D_MODEL:])
    merged = (g_a * t_a + g_m * t_b).astype(BF16)
    y = x_ref[...] + gate_ref[...] * jnp.dot(merged, wo_ref[...], preferred_element_type=F32)
    if final:
        y = y * lax.rsqrt(jnp.mean(y * y, axis=-1, keepdims=True) + EPS) * fg_ref[...]
    out_ref[...] = y


def _out_proj(x, o_groups, lse_groups, a_z, ymt, mg, gate, w_a, w_b, w_o, final_g, final):
    s = x.shape[0]
    tm = ROW_TILE
    row = lambda i: (i, 0)
    phase_specs = [pl.BlockSpec((r, tm // r, A_WIDTH), lambda i: (0, i, 0)) for _, r in A_PATTERNS]
    return pl.pallas_call(
        functools.partial(_out_kernel, final),
        grid=(s // tm,),
        in_specs=[pl.BlockSpec((tm, D_MODEL), row)] + phase_specs * 2
                 + [pl.BlockSpec((tm, A_WIDTH), row),
                    pl.BlockSpec((M_WIDTH, tm), lambda i: (0, i)),
                    pl.BlockSpec((tm, 2 * D_MODEL), row),
                    _const_spec((1, D_MODEL)),
                    _const_spec(w_a.shape), _const_spec(w_b.shape), _const_spec(w_o.shape),
                    _const_spec((1, D_MODEL))],
        out_specs=pl.BlockSpec((tm, D_MODEL), row),
        out_shape=jax.ShapeDtypeStruct((s, D_MODEL), F32),
        scratch_shapes=[pltpu.VMEM((A_WIDTH // LANES, tm, LANES), F32)] * 4,
        compiler_params=_cparams("parallel"),
        name="out_proj",
    )(x, *o_groups, *lse_groups, a_z, ymt, mg, gate, w_a, w_b, w_o, final_g)


def _prep_layer(w_in, w_uq, w_ukv, w_out_a, w_out_b, w_o):
    o_az = A_QKV
    o_cq = o_az + A_WIDTH
    o_ckv = o_cq + M_Q_LORA
    o_kr = o_ckv + M_KV_LORA
    o_mz = o_kr + M_ROPE
    o_mg = o_mz + M_WIDTH
    half = M_ROPE // 2
    w_qkv = w_in[:, :A_QKV].reshape(D_MODEL, 3, A_GROUPS, A_WIDTH).transpose(0, 2, 1, 3).reshape(D_MODEL, A_QKV)
    w_a = jnp.concatenate([w_qkv, w_in[:, o_az:o_cq]], axis=1).astype(BF16)
    w_g = w_in[:, o_mg:].astype(BF16)
    w_kr = w_in[:, o_kr:o_mz]
    z_lo = jnp.zeros((D_MODEL, M_NOPE), F32)
    z_hi = jnp.zeros((D_MODEL, M_QK_PAD - M_QK), F32)
    kr_a = jnp.concatenate([z_lo, w_kr, z_hi], axis=1)
    kr_b = jnp.concatenate([z_lo, w_kr[:, half:], w_kr[:, :half], z_hi], axis=1)
    w_c = jnp.concatenate([w_in[:, o_cq:o_kr], kr_a, kr_b], axis=1).astype(BF16)
    w_zt = w_in[:, o_mz:o_mg].T.astype(BF16)
    scale = M_QK ** -0.5 * math.log2(math.e)
    uq = (w_uq * scale).reshape(M_Q_LORA, M_HEADS, M_QK)
    uq = jnp.pad(uq, ((0, 0), (0, 0), (0, M_QK_PAD - M_QK)))
    w_uqt = uq.reshape(M_Q_LORA, M_HEADS * M_QK_PAD).T.astype(BF16)
    ukv = w_ukv.reshape(M_KV_LORA, M_HEADS, M_NOPE + M_V)
    w_k = jnp.pad(ukv[:, :, :M_NOPE], ((0, 0), (0, 0), (0, M_QK_PAD - M_NOPE)))
    w_k = w_k.reshape(M_KV_LORA, M_HEADS * M_QK_PAD).astype(BF16)
    w_vt = ukv[:, :, M_NOPE:].reshape(M_KV_LORA, M_WIDTH).T.astype(BF16)
    return (w_a, w_g, w_c, w_zt, w_uqt, w_k, w_vt,
            w_out_a.astype(BF16), w_out_b.astype(BF16), w_o.astype(BF16))


def _rope_tables(positions):
    half = M_ROPE // 2
    inv_freq = 1.0 / (ROPE_THETA ** (jnp.arange(0, M_ROPE, 2, dtype=F32) / M_ROPE))
    ang = positions.astype(F32)[:, None] * inv_freq
    cos, sin = jnp.cos(ang), jnp.sin(ang)
    s = positions.shape[0]
    z_lo = jnp.zeros((s, M_NOPE), F32)
    z_hi = jnp.zeros((s, M_QK_PAD - M_QK), F32)
    cos_k = jnp.concatenate([z_lo, cos, cos, z_hi], axis=1)
    sin_k = jnp.concatenate([z_lo, -sin, sin, z_hi], axis=1)
    return cos.T, sin.T, cos_k, sin_k


def kernel(x, c, positions, w_ada, b_ada, norm_g, w_in, q_norm_g, w_uq, kv_norm_g, w_ukv, w_out_a, w_out_b, w_o,
           rel_bias, final_norm_g):
    batch, s, _ = x.shape
    assert batch == 1 and s % (DIL_BLOCKS * A_BLOCK * A_PATTERNS[-1][1]) == 0
    assert s % MLA_Q_TILE == 0 and s % ROW_TILE == 0
    depth = w_ada.shape[0]
    xs = x[0]
    mod = _modulation(c, w_ada, b_ada)
    cos_t, sin_t, cos_k, sin_k = _rope_tables(positions[0])
    biases = [_dilated_bias(rel_bias, g, w, r) for g, (w, r) in enumerate(A_PATTERNS)]
    final_g = final_norm_g.reshape(1, D_MODEL)
    for l in range(depth):
        (w_a, w_g, w_c, w_zt, w_uqt, w_k, w_vt, wo_a, wo_b, wo) = _prep_layer(
            w_in[l], w_uq[l], w_ukv[l], w_out_a[l], w_out_b[l], w_o[l])
        shift, scale, gate = mod[l, 0], mod[l, 1], mod[l, 2]
        g = norm_g[l].reshape(1, D_MODEL)
        qkv0, qkv1, qkv2, a_z, mg = _in_proj_a(xs, g, scale, shift, w_a, w_g)
        qt, k, vt, mzt = _in_proj_m(xs, g, scale, shift, w_c, w_zt,
                                    q_norm_g[l].reshape(1, M_Q_LORA), kv_norm_g[l].reshape(1, M_KV_LORA),
                                    w_uqt, w_k, w_vt, cos_t, sin_t, cos_k, sin_k)
        outs = [_dilated_group(qkv, biases[gi], gi) for gi, qkv in enumerate((qkv0, qkv1, qkv2))]
        ymt = _mla_attention(qt, k, vt, mzt)
        xs = _out_proj(xs, [o for o, _ in outs], [ls for _, ls in outs], a_z, ymt, mg, gate,
                       wo_a, wo_b, wo, final_g, l == depth - 1)
    return xs[None]
```

```python
import functools
import math

import jax
import jax.numpy as jnp
from jax import lax
from jax.experimental import pallas as pl
from jax.experimental.pallas import tpu as pltpu

F32 = jnp.float32
BF16 = jnp.bfloat16

D_MODEL = 1024
A_PATTERNS = ((128, 1), (512, 4), (2048, 16))
A_GROUPS = 3
A_HEADS = 8
A_HEAD_DIM = 64
A_WIDTH = A_HEADS * A_HEAD_DIM
A_BLOCK = 128
A_QKV = 3 * A_GROUPS * A_WIDTH
M_HEADS = 16
M_Q_LORA = 256
M_KV_LORA = 128
M_NOPE = 64
M_ROPE = 32
M_V = 64
M_QK = M_NOPE + M_ROPE
M_QK_PAD = 128
M_WIDTH = M_HEADS * M_V
M_V_EXT = M_V + 16
ROPE_THETA = 10000.0
REL_BUCKETS = 32
REL_MAX_DIST = 2048
EPS = 1e-6
NEG_INF = -1e30
LANES = 128

ROW_TILE = 512
ATT_TILE = 512
MLA_Q_TILE = 2 * ATT_TILE
MLA_CHUNK = 256
DIL_BLOCKS = 4
VMEM_LIMIT = 56 * 1024 * 1024


def _cparams(*sem):
    return pltpu.CompilerParams(dimension_semantics=sem, vmem_limit_bytes=VMEM_LIMIT)


def _const_spec(shape):
    nd = len(shape)
    return pl.BlockSpec(shape, lambda *_: (0,) * nd, pipeline_mode=pl.Buffered(1))


def _nt_dot(a, b):
    return lax.dot_general(a, b, (((1,), (1,)), ((), ())), preferred_element_type=F32)


def _sigmoid(x):
    return 0.5 * jnp.tanh(0.5 * x) + 0.5


def _modulated_norm(x, g, scale, shift):
    y = x * lax.rsqrt(jnp.mean(x * x, axis=-1, keepdims=True) + EPS)
    return (y * g) * (1.0 + scale) + shift


def _mod_kernel(c_ref, w_ref, b_ref, o_ref):
    c = c_ref[...]
    c_act = c * _sigmoid(c)
    o_ref[0, 0] = jnp.sum(c_act * w_ref[0], axis=0, keepdims=True) + b_ref[0, 0]


def _modulation(c, w_ada, b_ada):
    depth = w_ada.shape[0]
    c_col = c.reshape(D_MODEL, 1)
    b4 = b_ada.reshape(depth, 3, 1, D_MODEL)
    return pl.pallas_call(
        _mod_kernel,
        grid=(depth, 3),
        in_specs=[pl.BlockSpec((D_MODEL, 1), lambda l, j: (0, 0)),
                  pl.BlockSpec((1, D_MODEL, D_MODEL), lambda l, j: (l, 0, j)),
                  pl.BlockSpec((1, 1, 1, D_MODEL), lambda l, j: (l, j, 0, 0))],
        out_specs=pl.BlockSpec((1, 1, 1, D_MODEL), lambda l, j: (l, j, 0, 0)),
        out_shape=jax.ShapeDtypeStruct((depth, 3, 1, D_MODEL), F32),
        compiler_params=_cparams("parallel", "parallel"),
        name="adaln_mod",
    )(c_col, w_ada, b4)


def _in_a_kernel(x_ref, g_ref, sc_ref, sh_ref, wa_ref, wg_ref, qkv0_ref, qkv1_ref, qkv2_ref, az_ref, mg_ref,
                 stage_sc):
    h = _modulated_norm(x_ref[...], g_ref[...], sc_ref[...], sh_ref[...]).astype(BF16)
    a = jnp.dot(h, wa_ref[...], preferred_element_type=F32)
    tm = a.shape[0]
    gw = 3 * A_WIDTH
    qkv0_ref[0] = a[:, :gw].astype(BF16)
    for g, out_ref in ((1, qkv1_ref), (2, qkv2_ref)):
        r = A_PATTERNS[g][1]
        for j in range(gw // LANES):
            stage_sc[j] = a[:, g * gw + j * LANES:g * gw + (j + 1) * LANES]
        for p in range(r):
            for j in range(gw // LANES):
                out_ref[p, :, j * LANES:(j + 1) * LANES] = stage_sc[j, pl.ds(p, tm // r, stride=r), :].astype(BF16)
    az_ref[...] = a[:, A_GROUPS * gw:]
    mg_ref[...] = jnp.dot(h, wg_ref[...], preferred_element_type=F32)


def _in_proj_a(x, g, scale, shift, w_a, w_g):
    s = x.shape[0]
    tm = ROW_TILE
    gw = 3 * A_WIDTH
    row = lambda i: (i, 0)
    phase_specs = [pl.BlockSpec((r, tm // r, gw), lambda i: (0, i, 0)) for _, r in A_PATTERNS]
    phase_shapes = [jax.ShapeDtypeStruct((r, s // r, gw), BF16) for _, r in A_PATTERNS]
    return pl.pallas_call(
        _in_a_kernel,
        grid=(s // tm,),
        in_specs=[pl.BlockSpec((tm, D_MODEL), row),
                  _const_spec((1, D_MODEL)), _const_spec((1, D_MODEL)), _const_spec((1, D_MODEL)),
                  _const_spec(w_a.shape), _const_spec(w_g.shape)],
        out_specs=phase_specs + [pl.BlockSpec((tm, A_WIDTH), row),
                                 pl.BlockSpec((tm, 2 * D_MODEL), row)],
        out_shape=phase_shapes + [jax.ShapeDtypeStruct((s, A_WIDTH), F32),
                                  jax.ShapeDtypeStruct((s, 2 * D_MODEL), F32)],
        scratch_shapes=[pltpu.VMEM((gw // LANES, tm, LANES), F32)],
        compiler_params=_cparams("parallel"),
        name="in_proj_a",
    )(x, g, scale, shift, w_a, w_g)


def _in_m_kernel(x_ref, g_ref, sc_ref, sh_ref, wc_ref, wzt_ref, qg_ref, kvg_ref, wuqt_ref, wk_ref, wvt_ref,
                 cost_ref, sint_ref, ck_ref, sk_ref, qt_ref, k_ref, vt_ref, mzt_ref):
    h = _modulated_norm(x_ref[...], g_ref[...], sc_ref[...], sh_ref[...]).astype(BF16)
    mzt_ref[...] = _nt_dot(wzt_ref[...], h)
    c = jnp.dot(h, wc_ref[...], preferred_element_type=F32)
    cq = c[:, :M_Q_LORA]
    ckv = c[:, M_Q_LORA:M_Q_LORA + M_KV_LORA]
    kr_a = c[:, M_Q_LORA + M_KV_LORA:M_Q_LORA + M_KV_LORA + M_QK_PAD]
    kr_b = c[:, M_Q_LORA + M_KV_LORA + M_QK_PAD:]
    cqn = (cq * lax.rsqrt(jnp.mean(cq * cq, axis=-1, keepdims=True) + EPS) * qg_ref[...]).astype(BF16)
    ckvn = (ckv * lax.rsqrt(jnp.mean(ckv * ckv, axis=-1, keepdims=True) + EPS) * kvg_ref[...]).astype(BF16)

    tm = cq.shape[0]
    half = M_ROPE // 2
    qt = _nt_dot(wuqt_ref[...], cqn).reshape(M_HEADS, M_QK_PAD, tm)
    cos = cost_ref[...][None]
    sin = sint_ref[...][None]
    t1 = qt[:, M_NOPE:M_NOPE + half]
    t2 = qt[:, M_NOPE + half:M_QK]
    qt_ref[:, :M_NOPE] = qt[:, :M_NOPE].astype(BF16)
    qt_ref[:, M_NOPE:M_NOPE + half] = (t1 * cos - t2 * sin).astype(BF16)
    qt_ref[:, M_NOPE + half:M_QK] = (t1 * sin + t2 * cos).astype(BF16)
    qt_ref[:, M_QK:] = qt[:, M_QK:].astype(BF16)

    kr = kr_a * ck_ref[...] + kr_b * sk_ref[...]
    k_all = jnp.dot(ckvn, wk_ref[...], preferred_element_type=F32)
    for hd in range(M_HEADS):
        k_ref[hd] = (k_all[:, hd * M_QK_PAD:(hd + 1) * M_QK_PAD] + kr).astype(BF16)

    vt = _nt_dot(wvt_ref[...], ckvn)
    vt_ref[:, 0, :M_V] = vt.reshape(M_HEADS, M_V, tm).astype(BF16)
    vt_ref[:, 0, M_V:] = jnp.ones((M_HEADS, M_V_EXT - M_V, tm), BF16)


def _in_proj_m(x, g, scale, shift, w_c, w_zt, qg, kvg, w_uqt, w_k, w_vt, cos_t, sin_t, cos_k, sin_k):
    s = x.shape[0]
    tm = ATT_TILE
    row = lambda i: (i, 0)
    col = lambda i: (0, i)
    return pl.pallas_call(
        _in_m_kernel,
        grid=(s // tm,),
        in_specs=[pl.BlockSpec((tm, D_MODEL), row),
                  _const_spec((1, D_MODEL)), _const_spec((1, D_MODEL)), _const_spec((1, D_MODEL)),
                  _const_spec(w_c.shape), _const_spec(w_zt.shape),
                  _const_spec(qg.shape), _const_spec(kvg.shape),
                  _const_spec(w_uqt.shape), _const_spec(w_k.shape), _const_spec(w_vt.shape),
                  pl.BlockSpec((M_ROPE // 2, tm), col), pl.BlockSpec((M_ROPE // 2, tm), col),
                  pl.BlockSpec((tm, M_QK_PAD), row), pl.BlockSpec((tm, M_QK_PAD), row)],
        out_specs=[pl.BlockSpec((M_HEADS, M_QK_PAD, tm), lambda i: (0, 0, i)),
                   pl.BlockSpec((M_HEADS, tm, M_QK_PAD), lambda i: (0, i, 0)),
                   pl.BlockSpec((M_HEADS, 1, M_V_EXT, tm), lambda i: (0, i, 0, 0)),
                   pl.BlockSpec((M_WIDTH, tm), col)],
        out_shape=[jax.ShapeDtypeStruct((M_HEADS, M_QK_PAD, s), BF16),
                   jax.ShapeDtypeStruct((M_HEADS, s, M_QK_PAD), BF16),
                   jax.ShapeDtypeStruct((M_HEADS, s // tm, M_V_EXT, tm), BF16),
                   jax.ShapeDtypeStruct((M_WIDTH, s), F32)],
        compiler_params=_cparams("parallel"),
        name="in_proj_m",
    )(x, g, scale, shift, w_c, w_zt, qg, kvg, w_uqt, w_k, w_vt, cos_t, sin_t, cos_k, sin_k)


def _dilated_kernel(q_ref, kp_ref, kc_ref, vp_ref, vc_ref, bias0_ref, bias_ref, o_ref, lse_ref):
    pairs = A_WIDTH // LANES
    lo = lax.broadcasted_iota(jnp.int32, (1, LANES), 1) < A_HEAD_DIM
    ones = jnp.ones((2 * A_BLOCK, LANES), BF16)
    zero = jnp.zeros((), BF16)

    for b in range(DIL_BLOCKS):
        rows = slice(b * A_BLOCK, (b + 1) * A_BLOCK)
        b_ref = bias0_ref if b == 0 else bias_ref
        q_all = q_ref[0, rows] * jnp.asarray(A_HEAD_DIM ** -0.5, BF16)

        def keys(prev_ref, cur_ref, sl):
            if b == 0:
                return jnp.concatenate([prev_ref[0, :, sl], cur_ref[0, :A_BLOCK, sl]], axis=0)
            return cur_ref[0, (b - 1) * A_BLOCK:(b + 1) * A_BLOCK, sl]

        scores, v_ext = [], []
        for pr in range(pairs):
            sl = slice(pr * LANES, (pr + 1) * LANES)
            k = keys(kp_ref, kc_ref, sl)
            v_ext.append(jnp.concatenate([keys(vp_ref, vc_ref, sl), ones], axis=1))
            q = q_all[:, sl]
            for half in range(2):
                qh = jnp.where(lo if half == 0 else jnp.logical_not(lo), q, zero)
                scores.append(_nt_dot(qh, k) + b_ref[0, 2 * pr + half])

        probs, maxes = [], []
        for s in scores:
            m = jnp.max(s, axis=-1, keepdims=True)
            probs.append(jnp.exp(s - m).astype(BF16))
            maxes.append(m)

        for pr in range(pairs):
            sl = slice(pr * LANES, (pr + 1) * LANES)
            outs, lses = [], []
            for half in range(2):
                ol = jnp.dot(probs[2 * pr + half], v_ext[pr], preferred_element_type=F32)
                l = ol[:, LANES:]
                outs.append(ol[:, :LANES] / l)
                lses.append(maxes[2 * pr + half] + jnp.log(l))
            o_ref[0, rows, sl] = jnp.where(lo, outs[0], outs[1])
            lse_ref[0, rows, sl] = jnp.where(lo, lses[0], lses[1])


def _dilated_group(qkv, bias, g):
    r, l_sub, _ = qkv.shape
    rows = DIL_BLOCKS * A_BLOCK
    blk = (1, rows, A_WIDTH)
    bias_blk = (1,) + bias.shape[1:]

    def cur(t):
        return pl.BlockSpec(blk, lambda p, n: (p, n, t))

    def prev(t):
        return pl.BlockSpec((1, A_BLOCK, A_WIDTH), lambda p, n: (p, jnp.maximum(DIL_BLOCKS * n - 1, 0), t))

    out_spec = pl.BlockSpec(blk, lambda p, n: (p, n, 0))
    return pl.pallas_call(
        _dilated_kernel,
        grid=(r, l_sub // rows),
        in_specs=[cur(0), prev(1), cur(1), prev(2), cur(2),
                  pl.BlockSpec(bias_blk, lambda p, n: (jnp.where(n == 0, 1, 0), 0, 0, 0)),
                  pl.BlockSpec(bias_blk, lambda p, n: (0, 0, 0, 0))],
        out_specs=[out_spec, out_spec],
        out_shape=[jax.ShapeDtypeStruct((r, l_sub, A_WIDTH), F32)] * 2,
        compiler_params=_cparams("parallel", "parallel"),
        name=f"dilated_g{g}",
    )(qkv, qkv, qkv, qkv, qkv, bias, bias)


def _t5_bucket(dist):
    exact = REL_BUCKETS // 2
    d = jnp.maximum(dist, 1).astype(F32)
    large = exact + (jnp.log(d / exact) / math.log(REL_MAX_DIST / exact) * (REL_BUCKETS - exact)).astype(jnp.int32)
    large = jnp.minimum(large, REL_BUCKETS - 1)
    return jnp.where(dist < exact, dist, large)


def _dilated_bias(rel_bias, g, window, r):
    qi = jnp.arange(A_BLOCK)[:, None]
    ki = jnp.arange(2 * A_BLOCK)[None, :]
    j = qi + A_BLOCK - ki
    band = (j >= 0) & (j <= window // r)
    tab = rel_bias[:, g * A_HEADS:(g + 1) * A_HEADS].astype(F32)
    pick = jax.nn.one_hot(_t5_bucket(jnp.maximum(j, 0) * r), REL_BUCKETS, dtype=F32)
    b = jnp.einsum("qkb,bh->hqk", pick, tab, precision=lax.Precision.HIGHEST)
    b = jnp.where(band[None], b, NEG_INF)
    first = jnp.where(ki[None] < A_BLOCK, NEG_INF, b)
    return jnp.stack([b, first])


def _mla_kernel(qt_ref, qn_ref, k_ref, vt_ref, mzt_ref, o_ref, s0_sc, s1_sc, mt0_sc, mt1_sc, p0_sc, p1_sc,
                a0_sc, a1_sc, m_sc, acc_sc):
    i = pl.program_id(1)
    last = pl.num_programs(1) - 1
    tk, tq, cw = ATT_TILE, MLA_Q_TILE, MLA_CHUNK
    bufs = ((s0_sc, mt0_sc, p0_sc, a0_sc), (s1_sc, mt1_sc, p1_sc, a1_sc))
    m_sc[...] = jnp.full(m_sc.shape, NEG_INF, F32)
    acc_sc[...] = jnp.zeros(acc_sc.shape, F32)
    p1_sc[...] = jnp.zeros(p1_sc.shape, BF16)
    a1_sc[...] = jnp.ones(a1_sc.shape, F32)

    def scores(kt, diagonal_offset, q_ref=qt_ref):
        k = k_ref[0, pl.ds(pl.multiple_of(kt * tk, tk), tk), :]
        s = jnp.dot(k, q_ref[0], preferred_element_type=F32)
        if diagonal_offset is not None:
            kpos = lax.broadcasted_iota(jnp.int32, (tk, tq), 0) + diagonal_offset
            qpos = lax.broadcasted_iota(jnp.int32, (tk, tq), 1)
            s = jnp.where(kpos <= qpos, s, NEG_INF)
        return s

    def fill(slot, s):
        s_ref, mt_ref, _, _ = bufs[slot]
        s_ref[...] = s
        mt_ref[...] = jnp.max(s, axis=0, keepdims=True)

    def softmax(slot):
        s_ref, mt_ref, p_ref, a_ref = bufs[slot]
        for c in range(tq // cw):
            cols = slice(c * cw, (c + 1) * cw)
            m_prev = m_sc[:, cols]
            m_new = jnp.maximum(m_prev, mt_ref[:, cols])
            a_ref[:, cols] = jnp.exp2(m_prev - m_new)
            p_ref[:, cols] = jnp.exp2((s_ref[:, cols] - m_new).astype(BF16))
            m_sc[:, cols] = m_new

    def value_matmul(kt, slot):
        _, _, p_ref, a_ref = bufs[slot]
        for c in range(tq // cw):
            cols = slice(c * cw, (c + 1) * cw)
            pv = jnp.dot(vt_ref[0, kt], p_ref[:, cols], preferred_element_type=F32)
            acc_sc[:, cols] = a_ref[:, cols] * acc_sc[:, cols] + pv

    def stage(kt, slot, refill, diagonal_offset=None):
        if refill == "ahead":
            s_next = scores(kt + 2, diagonal_offset)
        elif refill == "next":
            s_next = scores(slot, None, qn_ref)
        value_matmul(jnp.maximum(kt - 1, 0), 1 - slot)
        softmax(slot)
        if refill is not None:
            fill(slot, s_next)

    def pair(pr, refill="ahead", offsets=(None, None)):
        stage(2 * pr, 0, refill, offsets[0])
        stage(2 * pr + 1, 1, refill, offsets[1])

    @pl.when(i == 0)
    def _():
        fill(0, scores(0, 0))
        fill(1, scores(1, tk))

    @pl.when(i > 0)
    def _():
        def body(pr, carry):
            pair(pr)
            return carry

        lax.fori_loop(0, i - 1, body, 0)
        pair(i - 1, offsets=(0, tk))

    @pl.when(i < last)
    def _():
        pair(i, refill="next")

    @pl.when(i == last)
    def _():
        pair(i, refill=None)

    value_matmul(2 * i + 1, 1)
    mz = mzt_ref[...]
    gate = mz * _sigmoid(mz)
    acc = acc_sc[...]
    o_ref[...] = (acc[:M_V] / acc[M_V:M_V + 1] * gate).astype(BF16)


def _mla_attention(qt, k, vt, mzt):
    s = k.shape[1]
    tk, tq = ATT_TILE, MLA_Q_TILE
    nq = s // tq
    return pl.pallas_call(
        _mla_kernel,
        grid=(M_HEADS, nq),
        in_specs=[pl.BlockSpec((1, M_QK_PAD, tq), lambda h, i: (h, 0, i)),
                  pl.BlockSpec((1, M_QK_PAD, tq), lambda h, i: (h, 0, jnp.minimum(i + 1, nq - 1))),
                  pl.BlockSpec((1, s, M_QK_PAD), lambda h, i: (h, 0, 0)),
                  pl.BlockSpec((1, s // tk, M_V_EXT, tk), lambda h, i: (h, 0, 0, 0)),
                  pl.BlockSpec((M_V, tq), lambda h, i: (h, i))],
        out_specs=pl.BlockSpec((M_V, tq), lambda h, i: (h, i)),
        out_shape=jax.ShapeDtypeStruct((M_WIDTH, s), BF16),
        scratch_shapes=[pltpu.VMEM((tk, tq), F32), pltpu.VMEM((tk, tq), F32),
                        pltpu.VMEM((1, tq), F32), pltpu.VMEM((1, tq), F32),
                        pltpu.VMEM((tk, tq), BF16), pltpu.VMEM((tk, tq), BF16),
                        pltpu.VMEM((1, tq), F32), pltpu.VMEM((1, tq), F32),
                        pltpu.VMEM((1, tq), F32), pltpu.VMEM((M_V_EXT, tq), F32)],
        compiler_params=_cparams("arbitrary", "arbitrary"),
        name="mla_attention",
    )(qt, qt, k, vt, mzt)


def _out_kernel(final, x_ref, o0_ref, o1_ref, o2_ref, l0_ref, l1_ref, l2_ref, az_ref, ymt_ref, mg_ref, gate_ref,
                wa_ref, wb_ref, wo_ref, fg_ref, out_ref, o1_sc, o2_sc, l1_sc, l2_sc):
    tm = x_ref.shape[0]
    nj = A_WIDTH // LANES
    for src, dst in ((o1_ref, o1_sc), (o2_ref, o2_sc), (l1_ref, l1_sc), (l2_ref, l2_sc)):
        r = src.shape[0]
        for p in range(r):
            for j in range(nj):
                dst[j, pl.ds(p, tm // r, stride=r), :] = src[p, :, j * LANES:(j + 1) * LANES]

    def rows(sc):
        return jnp.concatenate([sc[j] for j in range(nj)], axis=1)

    l0, l1, l2 = l0_ref[0], rows(l1_sc), rows(l2_sc)
    mx = jnp.maximum(jnp.maximum(l0, l1), l2)
    e0, e1, e2 = jnp.exp(l0 - mx), jnp.exp(l1 - mx), jnp.exp(l2 - mx)
    mix = (o0_ref[0] * e0 + rows(o1_sc) * e1 + rows(o2_sc) * e2) / (e0 + e1 + e2)
    az = az_ref[...]
    y_a = (mix * (az * _sigmoid(az))).astype(BF16)
    t_a = jnp.dot(y_a, wa_ref[...], preferred_element_type=F32)
    t_b = lax.dot_general(ymt_ref[...], wb_ref[...], (((0,), (0,)), ((), ())), preferred_element_type=F32)
    mg = mg_ref[...]
    g_a = _sigmoid(mg[:, :D_MODEL])
    g_m = _sigmoid(mg[:, D_MODEL:])
    merged = (g_a * t_a + g_m * t_b).astype(BF16)
    y = x_ref[...] + gate_ref[...] * jnp.dot(merged, wo_ref[...], preferred_element_type=F32)
    if final:
        y = y * lax.rsqrt(jnp.mean(y * y, axis=-1, keepdims=True) + EPS) * fg_ref[...]
    out_ref[...] = y


def _out_proj(x, o_groups, lse_groups, a_z, ymt, mg, gate, w_a, w_b, w_o, final_g, final):
    s = x.shape[0]
    tm = ROW_TILE
    row = lambda i: (i, 0)
    phase_specs = [pl.BlockSpec((r, tm // r, A_WIDTH), lambda i: (0, i, 0)) for _, r in A_PATTERNS]
    return pl.pallas_call(
        functools.partial(_out_kernel, final),
        grid=(s // tm,),
        in_specs=[pl.BlockSpec((tm, D_MODEL), row)] + phase_specs * 2
                 + [pl.BlockSpec((tm, A_WIDTH), row),
                    pl.BlockSpec((M_WIDTH, tm), lambda i: (0, i)),
                    pl.BlockSpec((tm, 2 * D_MODEL), row),
                    _const_spec((1, D_MODEL)),
                    _const_spec(w_a.shape), _const_spec(w_b.shape), _const_spec(w_o.shape),
                    _const_spec((1, D_MODEL))],
        out_specs=pl.BlockSpec((tm, D_MODEL), row),
        out_shape=jax.ShapeDtypeStruct((s, D_MODEL), F32),
        scratch_shapes=[pltpu.VMEM((A_WIDTH // LANES, tm, LANES), F32)] * 4,
        compiler_params=_cparams("parallel"),
        name="out_proj",
    )(x, *o_groups, *lse_groups, a_z, ymt, mg, gate, w_a, w_b, w_o, final_g)


def _prep_layer(w_in, w_uq, w_ukv, w_out_a, w_out_b, w_o):
    o_az = A_QKV
    o_cq = o_az + A_WIDTH
    o_ckv = o_cq + M_Q_LORA
    o_kr = o_ckv + M_KV_LORA
    o_mz = o_kr + M_ROPE
    o_mg = o_mz + M_WIDTH
    half = M_ROPE // 2
    w_qkv = w_in[:, :A_QKV].reshape(D_MODEL, 3, A_GROUPS, A_WIDTH).transpose(0, 2, 1, 3).reshape(D_MODEL, A_QKV)
    w_a = jnp.concatenate([w_qkv, w_in[:, o_az:o_cq]], axis=1).astype(BF16)
    w_g = w_in[:, o_mg:].astype(BF16)
    w_kr = w_in[:, o_kr:o_mz]
    z_lo = jnp.zeros((D_MODEL, M_NOPE), F32)
    z_hi = jnp.zeros((D_MODEL, M_QK_PAD - M_QK), F32)
    kr_a = jnp.concatenate([z_lo, w_kr, z_hi], axis=1)
    kr_b = jnp.concatenate([z_lo, w_kr[:, half:], w_kr[:, :half], z_hi], axis=1)
    w_c = jnp.concatenate([w_in[:, o_cq:o_kr], kr_a, kr_b], axis=1).astype(BF16)
    w_zt = w_in[:, o_mz:o_mg].T.astype(BF16)
    scale = M_QK ** -0.5 * math.log2(math.e)
    uq = (w_uq * scale).reshape(M_Q_LORA, M_HEADS, M_QK)
    uq = jnp.pad(uq, ((0, 0), (0, 0), (0, M_QK_PAD - M_QK)))
    w_uqt = uq.reshape(M_Q_LORA, M_HEADS * M_QK_PAD).T.astype(BF16)
    ukv = w_ukv.reshape(M_KV_LORA, M_HEADS, M_NOPE + M_V)
    w_k = jnp.pad(ukv[:, :, :M_NOPE], ((0, 0), (0, 0), (0, M_QK_PAD - M_NOPE)))
    w_k = w_k.reshape(M_KV_LORA, M_HEADS * M_QK_PAD).astype(BF16)
    w_vt = ukv[:, :, M_NOPE:].reshape(M_KV_LORA, M_WIDTH).T.astype(BF16)
    return (w_a, w_g, w_c, w_zt, w_uqt, w_k, w_vt,
            w_out_a.astype(BF16), w_out_b.astype(BF16), w_o.astype(BF16))


def _rope_tables(positions):
    half = M_ROPE // 2
    inv_freq = 1.0 / (ROPE_THETA ** (jnp.arange(0, M_ROPE, 2, dtype=F32) / M_ROPE))
    ang = positions.astype(F32)[:, None] * inv_freq
    cos, sin = jnp.cos(ang), jnp.sin(ang)
    s = positions.shape[0]
    z_lo = jnp.zeros((s, M_NOPE), F32)
    z_hi = jnp.zeros((s, M_QK_PAD - M_QK), F32)
    cos_k = jnp.concatenate([z_lo, cos, cos, z_hi], axis=1)
    sin_k = jnp.concatenate([z_lo, -sin, sin, z_hi], axis=1)
    return cos.T, sin.T, cos_k, sin_k


def kernel(x, c, positions, w_ada, b_ada, norm_g, w_in, q_norm_g, w_uq, kv_norm_g, w_ukv, w_out_a, w_out_b, w_o,
           rel_bias, final_norm_g):
    batch, s, _ = x.shape
    assert batch == 1 and s % (DIL_BLOCKS * A_BLOCK * A_PATTERNS[-1][1]) == 0
    assert s % MLA_Q_TILE == 0 and s % ROW_TILE == 0
    depth = w_ada.shape[0]
    xs = x[0]
    mod = _modulation(c, w_ada, b_ada)
    cos_t, sin_t, cos_k, sin_k = _rope_tables(positions[0])
    biases = [_dilated_bias(rel_bias, g, w, r) for g, (w, r) in enumerate(A_PATTERNS)]
    final_g = final_norm_g.reshape(1, D_MODEL)
    for l in range(depth):
        (w_a, w_g, w_c, w_zt, w_uqt, w_k, w_vt, wo_a, wo_b, wo) = _prep_layer(
            w_in[l], w_uq[l], w_ukv[l], w_out_a[l], w_out_b[l], w_o[l])
        shift, scale, gate = mod[l, 0], mod[l, 1], mod[l, 2]
        g = norm_g[l].reshape(1, D_MODEL)
        qkv0, qkv1, qkv2, a_z, mg = _in_proj_a(xs, g, scale, shift, w_a, w_g)
        qt, k, vt, mzt = _in_proj_m(xs, g, scale, shift, w_c, w_zt,
                                    q_norm_g[l].reshape(1, M_Q_LORA), kv_norm_g[l].reshape(1, M_KV_LORA),
                                    w_uqt, w_k, w_vt, cos_t, sin_t, cos_k, sin_k)
        outs = [_dilated_group(qkv, biases[gi], gi) for gi, qkv in enumerate((qkv0, qkv1, qkv2))]
        ymt = _mla_attention(qt, k, vt, mzt)
        xs = _out_proj(xs, [o for o, _ in outs], [ls for _, ls in outs], a_z, ymt, mg, gate,
                       wo_a, wo_b, wo, final_g, l == depth - 1)
    return xs[None]
```

```python
import functools
import math

import jax
import jax.numpy as jnp
from jax import lax
from jax.experimental import pallas as pl
from jax.experimental.pallas import tpu as pltpu

F32 = jnp.float32
BF16 = jnp.bfloat16

D_MODEL = 1024
A_PATTERNS = ((128, 1), (512, 4), (2048, 16))
A_GROUPS = 3
A_HEADS = 8
A_HEAD_DIM = 64
A_WIDTH = A_HEADS * A_HEAD_DIM
A_BLOCK = 128
A_QKV = 3 * A_GROUPS * A_WIDTH
M_HEADS = 16
M_Q_LORA = 256
M_KV_LORA = 128
M_NOPE = 64
M_ROPE = 32
M_V = 64
M_QK = M_NOPE + M_ROPE
M_QK_PAD = 128
M_WIDTH = M_HEADS * M_V
M_V_EXT = M_V + 16
ROPE_THETA = 10000.0
REL_BUCKETS = 32
REL_MAX_DIST = 2048
EPS = 1e-6
NEG_INF = -1e30
LANES = 128

ROW_TILE = 512
ATT_TILE = 512
MLA_Q_TILE = 2 * ATT_TILE
MLA_CHUNK = 256
DIL_BLOCKS = 4
VMEM_LIMIT = 56 * 1024 * 1024


def _cparams(*sem):
    return pltpu.CompilerParams(dimension_semantics=sem, vmem_limit_bytes=VMEM_LIMIT)


def _const_spec(shape):
    nd = len(shape)
    return pl.BlockSpec(shape, lambda *_: (0,) * nd, pipeline_mode=pl.Buffered(1))


def _nt_dot(a, b):
    return lax.dot_general(a, b, (((1,), (1,)), ((), ())), preferred_element_type=F32)


def _sigmoid(x):
    return 0.5 * jnp.tanh(0.5 * x) + 0.5


def _modulated_norm(x, g, scale, shift):
    y = x * lax.rsqrt(jnp.mean(x * x, axis=-1, keepdims=True) + EPS)
    return (y * g) * (1.0 + scale) + shift


def _mod_kernel(c_ref, w_ref, b_ref, o_ref):
    c = c_ref[...]
    c_act = c * _sigmoid(c)
    o_ref[0, 0] = jnp.sum(c_act * w_ref[0], axis=0, keepdims=True) + b_ref[0, 0]


def _modulation(c, w_ada, b_ada):
    depth = w_ada.shape[0]
    c_col = c.reshape(D_MODEL, 1)
    b4 = b_ada.reshape(depth, 3, 1, D_MODEL)
    return pl.pallas_call(
        _mod_kernel,
        grid=(depth, 3),
        in_specs=[pl.BlockSpec((D_MODEL, 1), lambda l, j: (0, 0)),
                  pl.BlockSpec((1, D_MODEL, D_MODEL), lambda l, j: (l, 0, j)),
                  pl.BlockSpec((1, 1, 1, D_MODEL), lambda l, j: (l, j, 0, 0))],
        out_specs=pl.BlockSpec((1, 1, 1, D_MODEL), lambda l, j: (l, j, 0, 0)),
        out_shape=jax.ShapeDtypeStruct((depth, 3, 1, D_MODEL), F32),
        compiler_params=_cparams("parallel", "parallel"),
        name="adaln_mod",
    )(c_col, w_ada, b4)


def _in_a_kernel(x_ref, g_ref, sc_ref, sh_ref, wa_ref, wg_ref, qkv0_ref, qkv1_ref, qkv2_ref, az_ref, mg_ref,
                 stage_sc):
    h = _modulated_norm(x_ref[...], g_ref[...], sc_ref[...], sh_ref[...]).astype(BF16)
    a = jnp.dot(h, wa_ref[...], preferred_element_type=F32)
    tm = a.shape[0]
    gw = 3 * A_WIDTH
    qkv0_ref[0] = a[:, :gw].astype(BF16)
    for g, out_ref in ((1, qkv1_ref), (2, qkv2_ref)):
        r = A_PATTERNS[g][1]
        for j in range(gw // LANES):
            stage_sc[j] = a[:, g * gw + j * LANES:g * gw + (j + 1) * LANES]
        for p in range(r):
            for j in range(gw // LANES):
                out_ref[p, :, j * LANES:(j + 1) * LANES] = stage_sc[j, pl.ds(p, tm // r, stride=r), :].astype(BF16)
    az_ref[...] = a[:, A_GROUPS * gw:]
    mg_ref[...] = jnp.dot(h, wg_ref[...], preferred_element_type=F32)


def _in_proj_a(x, g, scale, shift, w_a, w_g):
    s = x.shape[0]
    tm = ROW_TILE
    gw = 3 * A_WIDTH
    row = lambda i: (i, 0)
    phase_specs = [pl.BlockSpec((r, tm // r, gw), lambda i: (0, i, 0)) for _, r in A_PATTERNS]
    phase_shapes = [jax.ShapeDtypeStruct((r, s // r, gw), BF16) for _, r in A_PATTERNS]
    return pl.pallas_call(
        _in_a_kernel,
        grid=(s // tm,),
        in_specs=[pl.BlockSpec((tm, D_MODEL), row),
                  _const_spec((1, D_MODEL)), _const_spec((1, D_MODEL)), _const_spec((1, D_MODEL)),
                  _const_spec(w_a.shape), _const_spec(w_g.shape)],
        out_specs=phase_specs + [pl.BlockSpec((tm, A_WIDTH), row),
                                 pl.BlockSpec((tm, 2 * D_MODEL), row)],
        out_shape=phase_shapes + [jax.ShapeDtypeStruct((s, A_WIDTH), F32),
                                  jax.ShapeDtypeStruct((s, 2 * D_MODEL), F32)],
        scratch_shapes=[pltpu.VMEM((gw // LANES, tm, LANES), F32)],
        compiler_params=_cparams("parallel"),
        name="in_proj_a",
    )(x, g, scale, shift, w_a, w_g)


def _in_m_kernel(x_ref, g_ref, sc_ref, sh_ref, wc_ref, wzt_ref, qg_ref, kvg_ref, wuqt_ref, wk_ref, wvt_ref,
                 cost_ref, sint_ref, ck_ref, sk_ref, qt_ref, k_ref, vt_ref, mzt_ref):
    h = _modulated_norm(x_ref[...], g_ref[...], sc_ref[...], sh_ref[...]).astype(BF16)
    mzt_ref[...] = _nt_dot(wzt_ref[...], h)
    c = jnp.dot(h, wc_ref[...], preferred_element_type=F32)
    cq = c[:, :M_Q_LORA]
    ckv = c[:, M_Q_LORA:M_Q_LORA + M_KV_LORA]
    kr_a = c[:, M_Q_LORA + M_KV_LORA:M_Q_LORA + M_KV_LORA + M_QK_PAD]
    kr_b = c[:, M_Q_LORA + M_KV_LORA + M_QK_PAD:]
    cqn = (cq * lax.rsqrt(jnp.mean(cq * cq, axis=-1, keepdims=True) + EPS) * qg_ref[...]).astype(BF16)
    ckvn = (ckv * lax.rsqrt(jnp.mean(ckv * ckv, axis=-1, keepdims=True) + EPS) * kvg_ref[...]).astype(BF16)

    tm = cq.shape[0]
    half = M_ROPE // 2
    qt = _nt_dot(wuqt_ref[...], cqn).reshape(M_HEADS, M_QK_PAD, tm)
    cos = cost_ref[...][None]
    sin = sint_ref[...][None]
    t1 = qt[:, M_NOPE:M_NOPE + half]
    t2 = qt[:, M_NOPE + half:M_QK]
    qt_ref[:, :M_NOPE] = qt[:, :M_NOPE].astype(BF16)
    qt_ref[:, M_NOPE:M_NOPE + half] = (t1 * cos - t2 * sin).astype(BF16)
    qt_ref[:, M_NOPE + half:M_QK] = (t1 * sin + t2 * cos).astype(BF16)
    qt_ref[:, M_QK:] = qt[:, M_QK:].astype(BF16)

    kr = kr_a * ck_ref[...] + kr_b * sk_ref[...]
    k_all = jnp.dot(ckvn, wk_ref[...], preferred_element_type=F32)
    for hd in range(M_HEADS):
        k_ref[hd] = (k_all[:, hd * M_QK_PAD:(hd + 1) * M_QK_PAD] + kr).astype(BF16)

    vt = _nt_dot(wvt_ref[...], ckvn)
    vt_ref[:, 0, :M_V] = vt.reshape(M_HEADS, M_V, tm).astype(BF16)
    vt_ref[:, 0, M_V:] = jnp.ones((M_HEADS, M_V_EXT - M_V, tm), BF16)


def _in_proj_m(x, g, scale, shift, w_c, w_zt, qg, kvg, w_uqt, w_k, w_vt, cos_t, sin_t, cos_k, sin_k):
    s = x.shape[0]
    tm = ATT_TILE
    row = lambda i: (i, 0)
    col = lambda i: (0, i)
    return pl.pallas_call(
        _in_m_kernel,
        grid=(s // tm,),
        in_specs=[pl.BlockSpec((tm, D_MODEL), row),
                  _const_spec((1, D_MODEL)), _const_spec((1, D_MODEL)), _const_spec((1, D_MODEL)),
                  _const_spec(w_c.shape), _const_spec(w_zt.shape),
                  _const_spec(qg.shape), _const_spec(kvg.shape),
                  _const_spec(w_uqt.shape), _const_spec(w_k.shape), _const_spec(w_vt.shape),
                  pl.BlockSpec((M_ROPE // 2, tm), col), pl.BlockSpec((M_ROPE // 2, tm), col),
                  pl.BlockSpec((tm, M_QK_PAD), row), pl.BlockSpec((tm, M_QK_PAD), row)],
        out_specs=[pl.BlockSpec((M_HEADS, M_QK_PAD, tm), lambda i: (0, 0, i)),
                   pl.BlockSpec((M_HEADS, tm, M_QK_PAD), lambda i: (0, i, 0)),
                   pl.BlockSpec((M_HEADS, 1, M_V_EXT, tm), lambda i: (0, i, 0, 0)),
                   pl.BlockSpec((M_WIDTH, tm), col)],
        out_shape=[jax.ShapeDtypeStruct((M_HEADS, M_QK_PAD, s), BF16),
                   jax.ShapeDtypeStruct((M_HEADS, s, M_QK_PAD), BF16),
                   jax.ShapeDtypeStruct((M_HEADS, s // tm, M_V_EXT, tm), BF16),
                   jax.ShapeDtypeStruct((M_WIDTH, s), F32)],
        compiler_params=_cparams("parallel"),
        name="in_proj_m",
    )(x, g, scale, shift, w_c, w_zt, qg, kvg, w_uqt, w_k, w_vt, cos_t, sin_t, cos_k, sin_k)


def _dilated_kernel(q_ref, kp_ref, kc_ref, vp_ref, vc_ref, bias0_ref, bias_ref, o_ref, lse_ref):
    pairs = A_WIDTH // LANES
    lo = lax.broadcasted_iota(jnp.int32, (1, LANES), 1) < A_HEAD_DIM
    ones = jnp.ones((2 * A_BLOCK, LANES), BF16)
    zero = jnp.zeros((), BF16)

    for b in range(DIL_BLOCKS):
        rows = slice(b * A_BLOCK, (b + 1) * A_BLOCK)
        b_ref = bias0_ref if b == 0 else bias_ref
        q_all = q_ref[0, rows] * jnp.asarray(A_HEAD_DIM ** -0.5, BF16)

        def keys(prev_ref, cur_ref, sl):
            if b == 0:
                return jnp.concatenate([prev_ref[0, :, sl], cur_ref[0, :A_BLOCK, sl]], axis=0)
            return cur_ref[0, (b - 1) * A_BLOCK:(b + 1) * A_BLOCK, sl]

        scores, v_ext = [], []
        for pr in range(pairs):
            sl = slice(pr * LANES, (pr + 1) * LANES)
            k = keys(kp_ref, kc_ref, sl)
            v_ext.append(jnp.concatenate([keys(vp_ref, vc_ref, sl), ones], axis=1))
            q = q_all[:, sl]
            for half in range(2):
                qh = jnp.where(lo if half == 0 else jnp.logical_not(lo), q, zero)
                scores.append(_nt_dot(qh, k) + b_ref[0, 2 * pr + half])

        probs, maxes = [], []
        for s in scores:
            m = jnp.max(s, axis=-1, keepdims=True)
            probs.append(jnp.exp(s - m).astype(BF16))
            maxes.append(m)

        for pr in range(pairs):
            sl = slice(pr * LANES, (pr + 1) * LANES)
            outs, lses = [], []
            for half in range(2):
                ol = jnp.dot(probs[2 * pr + half], v_ext[pr], preferred_element_type=F32)
                l = ol[:, LANES:]
                outs.append(ol[:, :LANES] / l)
                lses.append(maxes[2 * pr + half] + jnp.log(l))
            o_ref[0, rows, sl] = jnp.where(lo, outs[0], outs[1])
            lse_ref[0, rows, sl] = jnp.where(lo, lses[0], lses[1])


def _dilated_group(qkv, bias, g):
    r, l_sub, _ = qkv.shape
    rows = DIL_BLOCKS * A_BLOCK
    blk = (1, rows, A_WIDTH)
    bias_blk = (1,) + bias.shape[1:]

    def cur(t):
        return pl.BlockSpec(blk, lambda p, n: (p, n, t))

    def prev(t):
        return pl.BlockSpec((1, A_BLOCK, A_WIDTH), lambda p, n: (p, jnp.maximum(DIL_BLOCKS * n - 1, 0), t))

    out_spec = pl.BlockSpec(blk, lambda p, n: (p, n, 0))
    return pl.pallas_call(
        _dilated_kernel,
        grid=(r, l_sub // rows),
        in_specs=[cur(0), prev(1), cur(1), prev(2), cur(2),
                  pl.BlockSpec(bias_blk, lambda p, n: (jnp.where(n == 0, 1, 0), 0, 0, 0)),
                  pl.BlockSpec(bias_blk, lambda p, n: (0, 0, 0, 0))],
        out_specs=[out_spec, out_spec],
        out_shape=[jax.ShapeDtypeStruct((r, l_sub, A_WIDTH), F32)] * 2,
        compiler_params=_cparams("parallel", "parallel"),
        name=f"dilated_g{g}",
    )(qkv, qkv, qkv, qkv, qkv, bias, bias)


def _t5_bucket(dist):
    exact = REL_BUCKETS // 2
    d = jnp.maximum(dist, 1).astype(F32)
    large = exact + (jnp.log(d / exact) / math.log(REL_MAX_DIST / exact) * (REL_BUCKETS - exact)).astype(jnp.int32)
    large = jnp.minimum(large, REL_BUCKETS - 1)
    return jnp.where(dist < exact, dist, large)


def _dilated_bias(rel_bias, g, window, r):
    qi = jnp.arange(A_BLOCK)[:, None]
    ki = jnp.arange(2 * A_BLOCK)[None, :]
    j = qi + A_BLOCK - ki
    band = (j >= 0) & (j <= window // r)
    tab = rel_bias[:, g * A_HEADS:(g + 1) * A_HEADS].astype(F32)
    pick = jax.nn.one_hot(_t5_bucket(jnp.maximum(j, 0) * r), REL_BUCKETS, dtype=F32)
    b = jnp.einsum("qkb,bh->hqk", pick, tab, precision=lax.Precision.HIGHEST)
    b = jnp.where(band[None], b, NEG_INF)
    first = jnp.where(ki[None] < A_BLOCK, NEG_INF, b)
    return jnp.stack([b, first])


def _mla_kernel(qt_ref, qn_ref, k_ref, vt_ref, mzt_ref, o_ref, s0_sc, s1_sc, mt0_sc, mt1_sc, m_sc, acc_sc):
    i = pl.program_id(1)
    last = pl.num_programs(1) - 1
    tk, tq, cw = ATT_TILE, MLA_Q_TILE, MLA_CHUNK
    bufs = ((s0_sc, mt0_sc), (s1_sc, mt1_sc))
    m_sc[...] = jnp.full(m_sc.shape, NEG_INF, F32)
    acc_sc[...] = jnp.zeros(acc_sc.shape, F32)

    def scores(kt, diagonal_offset, q_ref=qt_ref):
        k = k_ref[0, pl.ds(pl.multiple_of(kt * tk, tk), tk), :]
        s = jnp.dot(k, q_ref[0], preferred_element_type=F32)
        if diagonal_offset is not None:
            kpos = lax.broadcasted_iota(jnp.int32, (tk, tq), 0) + diagonal_offset
            qpos = lax.broadcasted_iota(jnp.int32, (tk, tq), 1)
            s = jnp.where(kpos <= qpos, s, NEG_INF)
        return s

    def fill(slot, s):
        s_ref, mt_ref = bufs[slot]
        s_ref[...] = s
        mt_ref[...] = jnp.max(s, axis=0, keepdims=True)

    def accumulate(kt, slot):
        s_ref, mt_ref = bufs[slot]
        for c in range(tq // cw):
            cols = slice(c * cw, (c + 1) * cw)
            m_prev = m_sc[:, cols]
            m_new = jnp.maximum(m_prev, mt_ref[:, cols])
            alpha = jnp.exp2(m_prev - m_new)
            p = jnp.exp2((s_ref[:, cols] - m_new).astype(BF16))
            acc_sc[:, cols] = alpha * acc_sc[:, cols] + jnp.dot(vt_ref[0, kt], p, preferred_element_type=F32)
            m_sc[:, cols] = m_new

    def stage(kt, slot, refill, diagonal_offset=None):
        if refill == "ahead":
            s_next = scores(kt + 2, diagonal_offset)
        elif refill == "next":
            s_next = scores(slot, None, qn_ref)
        accumulate(kt, slot)
        if refill is not None:
            fill(slot, s_next)

    def pair(pr, refill="ahead", offsets=(None, None)):
        stage(2 * pr, 0, refill, offsets[0])
        stage(2 * pr + 1, 1, refill, offsets[1])

    @pl.when(i == 0)
    def _():
        fill(0, scores(0, 0))
        fill(1, scores(1, tk))

    @pl.when(i > 0)
    def _():
        def body(pr, carry):
            pair(pr)
            return carry

        lax.fori_loop(0, i - 1, body, 0)
        pair(i - 1, offsets=(0, tk))

    @pl.when(i < last)
    def _():
        pair(i, refill="next")

    @pl.when(i == last)
    def _():
        pair(i, refill=None)

    mz = mzt_ref[...]
    gate = mz * _sigmoid(mz)
    acc = acc_sc[...]
    o_ref[...] = (acc[:M_V] / acc[M_V:M_V + 1] * gate).astype(BF16)


def _mla_attention(qt, k, vt, mzt):
    s = k.shape[1]
    tk, tq = ATT_TILE, MLA_Q_TILE
    nq = s // tq
    return pl.pallas_call(
        _mla_kernel,
        grid=(M_HEADS, nq),
        in_specs=[pl.BlockSpec((1, M_QK_PAD, tq), lambda h, i: (h, 0, i)),
                  pl.BlockSpec((1, M_QK_PAD, tq), lambda h, i: (h, 0, jnp.minimum(i + 1, nq - 1))),
                  pl.BlockSpec((1, s, M_QK_PAD), lambda h, i: (h, 0, 0)),
                  pl.BlockSpec((1, s // tk, M_V_EXT, tk), lambda h, i: (h, 0, 0, 0)),
                  pl.BlockSpec((M_V, tq), lambda h, i: (h, i))],
        out_specs=pl.BlockSpec((M_V, tq), lambda h, i: (h, i)),
        out_shape=jax.ShapeDtypeStruct((M_WIDTH, s), BF16),
        scratch_shapes=[pltpu.VMEM((tk, tq), F32), pltpu.VMEM((tk, tq), F32),
                        pltpu.VMEM((1, tq), F32), pltpu.VMEM((1, tq), F32),
                        pltpu.VMEM((1, tq), F32), pltpu.VMEM((M_V_EXT, tq), F32)],
        compiler_params=_cparams("arbitrary", "arbitrary"),
        name="mla_attention",
    )(qt, qt, k, vt, mzt)


def _out_kernel(final, x_ref, o0_ref, o1_ref, o2_ref, l0_ref, l1_ref, l2_ref, az_ref, ymt_ref, mg_ref, gate_ref,
                wa_ref, wb_ref, wo_ref, fg_ref, out_ref, o1_sc, o2_sc, l1_sc, l2_sc):
    tm = x_ref.shape[0]
    nj = A_WIDTH // LANES
    for src, dst in ((o1_ref, o1_sc), (o2_ref, o2_sc), (l1_ref, l1_sc), (l2_ref, l2_sc)):
        r = src.shape[0]
        for p in range(r):
            for j in range(nj):
                dst[j, pl.ds(p, tm // r, stride=r), :] = src[p, :, j * LANES:(j + 1) * LANES]

    def rows(sc):
        return jnp.concatenate([sc[j] for j in range(nj)], axis=1)

    l0, l1, l2 = l0_ref[0], rows(l1_sc), rows(l2_sc)
    mx = jnp.maximum(jnp.maximum(l0, l1), l2)
    e0, e1, e2 = jnp.exp(l0 - mx), jnp.exp(l1 - mx), jnp.exp(l2 - mx)
    mix = (o0_ref[0] * e0 + rows(o1_sc) * e1 + rows(o2_sc) * e2) / (e0 + e1 + e2)
    az = az_ref[...]
    y_a = (mix * (az * _sigmoid(az))).astype(BF16)
    t_a = jnp.dot(y_a, wa_ref[...], preferred_element_type=F32)
    t_b = lax.dot_general(ymt_ref[...], wb_ref[...], (((0,), (0,)), ((), ())), preferred_element_type=F32)
    mg = mg_ref[...]
    g_a = _sigmoid(mg[:, :D_MODEL])
    g_m = _sigmoid(mg[:, D_MODEL:])
    merged = (g_a * t_a + g_m * t_b).astype(BF16)
    y = x_ref[...] + gate_ref[...] * jnp.dot(merged, wo_ref[...], preferred_element_type=F32)
    if final:
        y = y * lax.rsqrt(jnp.mean(y * y, axis=-1, keepdims=True) + EPS) * fg_ref[...]
    out_ref[...] = y


def _out_proj(x, o_groups, lse_groups, a_z, ymt, mg, gate, w_a, w_b, w_o, final_g, final):
    s = x.shape[0]
    tm = ROW_TILE
    row = lambda i: (i, 0)
    phase_specs = [pl.BlockSpec((r, tm // r, A_WIDTH), lambda i: (0, i, 0)) for _, r in A_PATTERNS]
    return pl.pallas_call(
        functools.partial(_out_kernel, final),
        grid=(s // tm,),
        in_specs=[pl.BlockSpec((tm, D_MODEL), row)] + phase_specs * 2
                 + [pl.BlockSpec((tm, A_WIDTH), row),
                    pl.BlockSpec((M_WIDTH, tm), lambda i: (0, i)),
                    pl.BlockSpec((tm, 2 * D_MODEL), row),
                    _const_spec((1, D_MODEL)),
                    _const_spec(w_a.shape), _const_spec(w_b.shape), _const_spec(w_o.shape),
                    _const_spec((1, D_MODEL))],
        out_specs=pl.BlockSpec((tm, D_MODEL), row),
        out_shape=jax.ShapeDtypeStruct((s, D_MODEL), F32),
        scratch_shapes=[pltpu.VMEM((A_WIDTH // LANES, tm, LANES), F32)] * 4,
        compiler_params=_cparams("parallel"),
        name="out_proj",
    )(x, *o_groups, *lse_groups, a_z, ymt, mg, gate, w_a, w_b, w_o, final_g)


def _prep_layer(w_in, w_uq, w_ukv, w_out_a, w_out_b, w_o):
    o_az = A_QKV
    o_cq = o_az + A_WIDTH
    o_ckv = o_cq + M_Q_LORA
    o_kr = o_ckv + M_KV_LORA
    o_mz = o_kr + M_ROPE
    o_mg = o_mz + M_WIDTH
    half = M_ROPE // 2
    w_qkv = w_in[:, :A_QKV].reshape(D_MODEL, 3, A_GROUPS, A_WIDTH).transpose(0, 2, 1, 3).reshape(D_MODEL, A_QKV)
    w_a = jnp.concatenate([w_qkv, w_in[:, o_az:o_cq]], axis=1).astype(BF16)
    w_g = w_in[:, o_mg:].astype(BF16)
    w_kr = w_in[:, o_kr:o_mz]
    z_lo = jnp.zeros((D_MODEL, M_NOPE), F32)
    z_hi = jnp.zeros((D_MODEL, M_QK_PAD - M_QK), F32)
    kr_a = jnp.concatenate([z_lo, w_kr, z_hi], axis=1)
    kr_b = jnp.concatenate([z_lo, w_kr[:, half:], w_kr[:, :half], z_hi], axis=1)
    w_c = jnp.concatenate([w_in[:, o_cq:o_kr], kr_a, kr_b], axis=1).astype(BF16)
    w_zt = w_in[:, o_mz:o_mg].T.astype(BF16)
    scale = M_QK ** -0.5 * math.log2(math.e)
    uq = (w_uq * scale).reshape(M_Q_LORA, M_HEADS, M_QK)
    uq = jnp.pad(uq, ((0, 0), (0, 0), (0, M_QK_PAD - M_QK)))
    w_uqt = uq.reshape(M_Q_LORA, M_HEADS * M_QK_PAD).T.astype(BF16)
    ukv = w_ukv.reshape(M_KV_LORA, M_HEADS, M_NOPE + M_V)
    w_k = jnp.pad(ukv[:, :, :M_NOPE], ((0, 0), (0, 0), (0, M_QK_PAD - M_NOPE)))
    w_k = w_k.reshape(M_KV_LORA, M_HEADS * M_QK_PAD).astype(BF16)
    w_vt = ukv[:, :, M_NOPE:].reshape(M_KV_LORA, M_WIDTH).T.astype(BF16)
    return (w_a, w_g, w_c, w_zt, w_uqt, w_k, w_vt,
            w_out_a.astype(BF16), w_out_b.astype(BF16), w_o.astype(BF16))


def _rope_tables(positions):
    half = M_ROPE // 2
    inv_freq = 1.0 / (ROPE_THETA ** (jnp.arange(0, M_ROPE, 2, dtype=F32) / M_ROPE))
    ang = positions.astype(F32)[:, None] * inv_freq
    cos, sin = jnp.cos(ang), jnp.sin(ang)
    s = positions.shape[0]
    z_lo = jnp.zeros((s, M_NOPE), F32)
    z_hi = jnp.zeros((s, M_QK_PAD - M_QK), F32)
    cos_k = jnp.concatenate([z_lo, cos, cos, z_hi], axis=1)
    sin_k = jnp.concatenate([z_lo, -sin, sin, z_hi], axis=1)
    return cos.T, sin.T, cos_k, sin_k


def kernel(x, c, positions, w_ada, b_ada, norm_g, w_in, q_norm_g, w_uq, kv_norm_g, w_ukv, w_out_a, w_out_b, w_o,
           rel_bias, final_norm_g):
    batch, s, _ = x.shape
    assert batch == 1 and s % (DIL_BLOCKS * A_BLOCK * A_PATTERNS[-1][1]) == 0
    assert s % MLA_Q_TILE == 0 and s % ROW_TILE == 0
    depth = w_ada.shape[0]
    xs = x[0]
    mod = _modulation(c, w_ada, b_ada)
    cos_t, sin_t, cos_k, sin_k = _rope_tables(positions[0])
    biases = [_dilated_bias(rel_bias, g, w, r) for g, (w, r) in enumerate(A_PATTERNS)]
    final_g = final_norm_g.reshape(1, D_MODEL)
    for l in range(depth):
        (w_a, w_g, w_c, w_zt, w_uqt, w_k, w_vt, wo_a, wo_b, wo) = _prep_layer(
            w_in[l], w_uq[l], w_ukv[l], w_out_a[l], w_out_b[l], w_o[l])
        shift, scale, gate = mod[l, 0], mod[l, 1], mod[l, 2]
        g = norm_g[l].reshape(1, D_MODEL)
        qkv0, qkv1, qkv2, a_z, mg = _in_proj_a(xs, g, scale, shift, w_a, w_g)
        qt, k, vt, mzt = _in_proj_m(xs, g, scale, shift, w_c, w_zt,
                                    q_norm_g[l].reshape(1, M_Q_LORA), kv_norm_g[l].reshape(1, M_KV_LORA),
                                    w_uqt, w_k, w_vt, cos_t, sin_t, cos_k, sin_k)
        outs = [_dilated_group(qkv, biases[gi], gi) for gi, qkv in enumerate((qkv0, qkv1, qkv2))]
        ymt = _mla_attention(qt, k, vt, mzt)
        xs = _out_proj(xs, [o for o, _ in outs], [ls for _, ls in outs], a_z, ymt, mg, gate,
                       wo_a, wo_b, wo, final_g, l == depth - 1)
    return xs[None]
```

```python
import functools
import math

import jax
import jax.numpy as jnp
from jax import lax
from jax.experimental import pallas as pl
from jax.experimental.pallas import tpu as pltpu

F32 = jnp.float32
BF16 = jnp.bfloat16

D_MODEL = 1024
A_PATTERNS = ((128, 1), (512, 4), (2048, 16))
A_GROUPS = 3
A_HEADS = 8
A_HEAD_DIM = 64
A_WIDTH = A_HEADS * A_HEAD_DIM
A_BLOCK = 128
A_QKV = 3 * A_GROUPS * A_WIDTH
M_HEADS = 16
M_Q_LORA = 256
M_KV_LORA = 128
M_NOPE = 64
M_ROPE = 32
M_V = 64
M_QK = M_NOPE + M_ROPE
M_QK_PAD = 128
M_WIDTH = M_HEADS * M_V
M_V_EXT = M_V + 16
ROPE_THETA = 10000.0
REL_BUCKETS = 32
REL_MAX_DIST = 2048
EPS = 1e-6
NEG_INF = -1e30
LANES = 128

ROW_TILE = 512
ATT_TILE = 512
MLA_Q_TILE = 2 * ATT_TILE
MLA_CHUNK = 256
DIL_BLOCKS = 4
VMEM_LIMIT = 56 * 1024 * 1024


def _cparams(*sem):
    return pltpu.CompilerParams(dimension_semantics=sem, vmem_limit_bytes=VMEM_LIMIT)


def _const_spec(shape):
    nd = len(shape)
    return pl.BlockSpec(shape, lambda *_: (0,) * nd, pipeline_mode=pl.Buffered(1))


def _nt_dot(a, b):
    return lax.dot_general(a, b, (((1,), (1,)), ((), ())), preferred_element_type=F32)


def _sigmoid(x):
    return 0.5 * jnp.tanh(0.5 * x) + 0.5


def _modulated_norm(x, g, scale, shift):
    y = x * lax.rsqrt(jnp.mean(x * x, axis=-1, keepdims=True) + EPS)
    return (y * g) * (1.0 + scale) + shift


def _mod_kernel(c_ref, w_ref, b_ref, o_ref):
    c = c_ref[...]
    c_act = c * _sigmoid(c)
    o_ref[0, 0] = jnp.sum(c_act * w_ref[0], axis=0, keepdims=True) + b_ref[0, 0]


def _modulation(c, w_ada, b_ada):
    depth = w_ada.shape[0]
    c_col = c.reshape(D_MODEL, 1)
    b4 = b_ada.reshape(depth, 3, 1, D_MODEL)
    return pl.pallas_call(
        _mod_kernel,
        grid=(depth, 3),
        in_specs=[pl.BlockSpec((D_MODEL, 1), lambda l, j: (0, 0)),
                  pl.BlockSpec((1, D_MODEL, D_MODEL), lambda l, j: (l, 0, j)),
                  pl.BlockSpec((1, 1, 1, D_MODEL), lambda l, j: (l, j, 0, 0))],
        out_specs=pl.BlockSpec((1, 1, 1, D_MODEL), lambda l, j: (l, j, 0, 0)),
        out_shape=jax.ShapeDtypeStruct((depth, 3, 1, D_MODEL), F32),
        compiler_params=_cparams("parallel", "parallel"),
        name="adaln_mod",
    )(c_col, w_ada, b4)


def _in_a_kernel(x_ref, g_ref, sc_ref, sh_ref, wa_ref, wg_ref, qkv0_ref, qkv1_ref, qkv2_ref, az_ref, mg_ref,
                 stage_sc):
    h = _modulated_norm(x_ref[...], g_ref[...], sc_ref[...], sh_ref[...]).astype(BF16)
    a = jnp.dot(h, wa_ref[...], preferred_element_type=F32)
    tm = a.shape[0]
    gw = 3 * A_WIDTH
    qkv0_ref[0] = a[:, :gw].astype(BF16)
    for g, out_ref in ((1, qkv1_ref), (2, qkv2_ref)):
        r = A_PATTERNS[g][1]
        for j in range(gw // LANES):
            stage_sc[j] = a[:, g * gw + j * LANES:g * gw + (j + 1) * LANES]
        for p in range(r):
            for j in range(gw // LANES):
                out_ref[p, :, j * LANES:(j + 1) * LANES] = stage_sc[j, pl.ds(p, tm // r, stride=r), :].astype(BF16)
    az_ref[...] = a[:, A_GROUPS * gw:]
    mg_ref[...] = jnp.dot(h, wg_ref[...], preferred_element_type=F32)


def _in_proj_a(x, g, scale, shift, w_a, w_g):
    s = x.shape[0]
    tm = ROW_TILE
    gw = 3 * A_WIDTH
    row = lambda i: (i, 0)
    phase_specs = [pl.BlockSpec((r, tm // r, gw), lambda i: (0, i, 0)) for _, r in A_PATTERNS]
    phase_shapes = [jax.ShapeDtypeStruct((r, s // r, gw), BF16) for _, r in A_PATTERNS]
    return pl.pallas_call(
        _in_a_kernel,
        grid=(s // tm,),
        in_specs=[pl.BlockSpec((tm, D_MODEL), row),
                  _const_spec((1, D_MODEL)), _const_spec((1, D_MODEL)), _const_spec((1, D_MODEL)),
                  _const_spec(w_a.shape), _const_spec(w_g.shape)],
        out_specs=phase_specs + [pl.BlockSpec((tm, A_WIDTH), row),
                                 pl.BlockSpec((tm, 2 * D_MODEL), row)],
        out_shape=phase_shapes + [jax.ShapeDtypeStruct((s, A_WIDTH), F32),
                                  jax.ShapeDtypeStruct((s, 2 * D_MODEL), F32)],
        scratch_shapes=[pltpu.VMEM((gw // LANES, tm, LANES), F32)],
        compiler_params=_cparams("parallel"),
        name="in_proj_a",
    )(x, g, scale, shift, w_a, w_g)


def _in_m_kernel(x_ref, g_ref, sc_ref, sh_ref, wc_ref, wzt_ref, qg_ref, kvg_ref, wuqt_ref, wk_ref, wvt_ref,
                 cost_ref, sint_ref, ck_ref, sk_ref, qt_ref, k_ref, vt_ref, mzt_ref):
    h = _modulated_norm(x_ref[...], g_ref[...], sc_ref[...], sh_ref[...]).astype(BF16)
    mzt_ref[...] = _nt_dot(wzt_ref[...], h)
    c = jnp.dot(h, wc_ref[...], preferred_element_type=F32)
    cq = c[:, :M_Q_LORA]
    ckv = c[:, M_Q_LORA:M_Q_LORA + M_KV_LORA]
    kr_a = c[:, M_Q_LORA + M_KV_LORA:M_Q_LORA + M_KV_LORA + M_QK_PAD]
    kr_b = c[:, M_Q_LORA + M_KV_LORA + M_QK_PAD:]
    cqn = (cq * lax.rsqrt(jnp.mean(cq * cq, axis=-1, keepdims=True) + EPS) * qg_ref[...]).astype(BF16)
    ckvn = (ckv * lax.rsqrt(jnp.mean(ckv * ckv, axis=-1, keepdims=True) + EPS) * kvg_ref[...]).astype(BF16)

    tm = cq.shape[0]
    half = M_ROPE // 2
    qt = _nt_dot(wuqt_ref[...], cqn).reshape(M_HEADS, M_QK_PAD, tm)
    cos = cost_ref[...][None]
    sin = sint_ref[...][None]
    t1 = qt[:, M_NOPE:M_NOPE + half]
    t2 = qt[:, M_NOPE + half:M_QK]
    qt_ref[:, :M_NOPE] = qt[:, :M_NOPE].astype(BF16)
    qt_ref[:, M_NOPE:M_NOPE + half] = (t1 * cos - t2 * sin).astype(BF16)
    qt_ref[:, M_NOPE + half:M_QK] = (t1 * sin + t2 * cos).astype(BF16)
    qt_ref[:, M_QK:] = qt[:, M_QK:].astype(BF16)

    kr = kr_a * ck_ref[...] + kr_b * sk_ref[...]
    k_all = jnp.dot(ckvn, wk_ref[...], preferred_element_type=F32)
    for hd in range(M_HEADS):
        k_ref[hd] = (k_all[:, hd * M_QK_PAD:(hd + 1) * M_QK_PAD] + kr).astype(BF16)

    vt = _nt_dot(wvt_ref[...], ckvn)
    vt_ref[:, 0, :M_V] = vt.reshape(M_HEADS, M_V, tm).astype(BF16)
    vt_ref[:, 0, M_V:] = jnp.ones((M_HEADS, M_V_EXT - M_V, tm), BF16)


def _in_proj_m(x, g, scale, shift, w_c, w_zt, qg, kvg, w_uqt, w_k, w_vt, cos_t, sin_t, cos_k, sin_k):
    s = x.shape[0]
    tm = ATT_TILE
    row = lambda i: (i, 0)
    col = lambda i: (0, i)
    return pl.pallas_call(
        _in_m_kernel,
        grid=(s // tm,),
        in_specs=[pl.BlockSpec((tm, D_MODEL), row),
                  _const_spec((1, D_MODEL)), _const_spec((1, D_MODEL)), _const_spec((1, D_MODEL)),
                  _const_spec(w_c.shape), _const_spec(w_zt.shape),
                  _const_spec(qg.shape), _const_spec(kvg.shape),
                  _const_spec(w_uqt.shape), _const_spec(w_k.shape), _const_spec(w_vt.shape),
                  pl.BlockSpec((M_ROPE // 2, tm), col), pl.BlockSpec((M_ROPE // 2, tm), col),
                  pl.BlockSpec((tm, M_QK_PAD), row), pl.BlockSpec((tm, M_QK_PAD), row)],
        out_specs=[pl.BlockSpec((M_HEADS, M_QK_PAD, tm), lambda i: (0, 0, i)),
                   pl.BlockSpec((M_HEADS, tm, M_QK_PAD), lambda i: (0, i, 0)),
                   pl.BlockSpec((M_HEADS, 1, M_V_EXT, tm), lambda i: (0, i, 0, 0)),
                   pl.BlockSpec((M_WIDTH, tm), col)],
        out_shape=[jax.ShapeDtypeStruct((M_HEADS, M_QK_PAD, s), BF16),
                   jax.ShapeDtypeStruct((M_HEADS, s, M_QK_PAD), BF16),
                   jax.ShapeDtypeStruct((M_HEADS, s // tm, M_V_EXT, tm), BF16),
                   jax.ShapeDtypeStruct((M_WIDTH, s), F32)],
        compiler_params=_cparams("parallel"),
        name="in_proj_m",
    )(x, g, scale, shift, w_c, w_zt, qg, kvg, w_uqt, w_k, w_vt, cos_t, sin_t, cos_k, sin_k)


def _dilated_kernel(q_ref, kp_ref, kc_ref, vp_ref, vc_ref, bias0_ref, bias_ref, o_ref, lse_ref):
    pairs = A_WIDTH // LANES
    lo = lax.broadcasted_iota(jnp.int32, (1, LANES), 1) < A_HEAD_DIM
    ones = jnp.ones((2 * A_BLOCK, LANES), BF16)
    zero = jnp.zeros((), BF16)

    for b in range(DIL_BLOCKS):
        rows = slice(b * A_BLOCK, (b + 1) * A_BLOCK)
        b_ref = bias0_ref if b == 0 else bias_ref
        q_all = q_ref[0, rows] * jnp.asarray(A_HEAD_DIM ** -0.5, BF16)

        def keys(prev_ref, cur_ref, sl):
            if b == 0:
                return jnp.concatenate([prev_ref[0, :, sl], cur_ref[0, :A_BLOCK, sl]], axis=0)
            return cur_ref[0, (b - 1) * A_BLOCK:(b + 1) * A_BLOCK, sl]

        scores, v_ext = [], []
        for pr in range(pairs):
            sl = slice(pr * LANES, (pr + 1) * LANES)
            k = keys(kp_ref, kc_ref, sl)
            v_ext.append(jnp.concatenate([keys(vp_ref, vc_ref, sl), ones], axis=1))
            q = q_all[:, sl]
            for half in range(2):
                qh = jnp.where(lo if half == 0 else jnp.logical_not(lo), q, zero)
                scores.append(_nt_dot(qh, k) + b_ref[0, 2 * pr + half])

        probs, maxes = [], []
        for s in scores:
            m = jnp.max(s, axis=-1, keepdims=True)
            probs.append(jnp.exp(s - m).astype(BF16))
            maxes.append(m)

        for pr in range(pairs):
            sl = slice(pr * LANES, (pr + 1) * LANES)
            outs, lses = [], []
            for half in range(2):
                ol = jnp.dot(probs[2 * pr + half], v_ext[pr], preferred_element_type=F32)
                l = ol[:, LANES:]
                outs.append(ol[:, :LANES] / l)
                lses.append(maxes[2 * pr + half] + jnp.log(l))
            o_ref[0, rows, sl] = jnp.where(lo, outs[0], outs[1])
            lse_ref[0, rows, sl] = jnp.where(lo, lses[0], lses[1])


def _dilated_group(qkv, bias, g):
    r, l_sub, _ = qkv.shape
    rows = DIL_BLOCKS * A_BLOCK
    blk = (1, rows, A_WIDTH)
    bias_blk = (1,) + bias.shape[1:]

    def cur(t):
        return pl.BlockSpec(blk, lambda p, n: (p, n, t))

    def prev(t):
        return pl.BlockSpec((1, A_BLOCK, A_WIDTH), lambda p, n: (p, jnp.maximum(DIL_BLOCKS * n - 1, 0), t))

    out_spec = pl.BlockSpec(blk, lambda p, n: (p, n, 0))
    return pl.pallas_call(
        _dilated_kernel,
        grid=(r, l_sub // rows),
        in_specs=[cur(0), prev(1), cur(1), prev(2), cur(2),
                  pl.BlockSpec(bias_blk, lambda p, n: (jnp.where(n == 0, 1, 0), 0, 0, 0)),
                  pl.BlockSpec(bias_blk, lambda p, n: (0, 0, 0, 0))],
        out_specs=[out_spec, out_spec],
        out_shape=[jax.ShapeDtypeStruct((r, l_sub, A_WIDTH), F32)] * 2,
        compiler_params=_cparams("parallel", "parallel"),
        name=f"dilated_g{g}",
    )(qkv, qkv, qkv, qkv, qkv, bias, bias)


def _t5_bucket(dist):
    exact = REL_BUCKETS // 2
    d = jnp.maximum(dist, 1).astype(F32)
    large = exact + (jnp.log(d / exact) / math.log(REL_MAX_DIST / exact) * (REL_BUCKETS - exact)).astype(jnp.int32)
    large = jnp.minimum(large, REL_BUCKETS - 1)
    return jnp.where(dist < exact, dist, large)


def _dilated_bias(rel_bias, g, window, r):
    qi = jnp.arange(A_BLOCK)[:, None]
    ki = jnp.arange(2 * A_BLOCK)[None, :]
    j = qi + A_BLOCK - ki
    band = (j >= 0) & (j <= window // r)
    tab = rel_bias[:, g * A_HEADS:(g + 1) * A_HEADS].astype(F32)
    pick = jax.nn.one_hot(_t5_bucket(jnp.maximum(j, 0) * r), REL_BUCKETS, dtype=F32)
    b = jnp.einsum("qkb,bh->hqk", pick, tab, precision=lax.Precision.HIGHEST)
    b = jnp.where(band[None], b, NEG_INF)
    first = jnp.where(ki[None] < A_BLOCK, NEG_INF, b)
    return jnp.stack([b, first])


def _mla_kernel(qt_ref, qn_ref, k_ref, vt_ref, mzt_ref, o_ref, s0_sc, s1_sc, mt0_sc, mt1_sc, m_sc, acc_sc):
    i = pl.program_id(1)
    last = pl.num_programs(1) - 1
    tk, tq, cw = ATT_TILE, MLA_Q_TILE, MLA_CHUNK
    bufs = ((s0_sc, mt0_sc), (s1_sc, mt1_sc))
    m_sc[...] = jnp.full(m_sc.shape, NEG_INF, F32)
    acc_sc[...] = jnp.zeros(acc_sc.shape, F32)

    def scores(kt, diagonal_offset, q_ref=qt_ref):
        k = k_ref[0, pl.ds(pl.multiple_of(kt * tk, tk), tk), :]
        s = jnp.dot(k, q_ref[0], preferred_element_type=F32)
        if diagonal_offset is not None:
            kpos = lax.broadcasted_iota(jnp.int32, (tk, tq), 0) + diagonal_offset
            qpos = lax.broadcasted_iota(jnp.int32, (tk, tq), 1)
            s = jnp.where(kpos <= qpos, s, NEG_INF)
        return s

    def fill(slot, s):
        s_ref, mt_ref = bufs[slot]
        s_ref[...] = s
        mt_ref[...] = jnp.max(s, axis=0, keepdims=True)

    def accumulate(kt, slot):
        s_ref, mt_ref = bufs[slot]
        for c in range(tq // cw):
            cols = slice(c * cw, (c + 1) * cw)
            m_prev = m_sc[:, cols]
            m_new = jnp.maximum(m_prev, mt_ref[:, cols])
            alpha = jnp.exp2(m_prev - m_new)
            p = jnp.exp2(s_ref[:, cols] - m_new).astype(BF16)
            acc_sc[:, cols] = alpha * acc_sc[:, cols] + jnp.dot(vt_ref[0, kt], p, preferred_element_type=F32)
            m_sc[:, cols] = m_new

    def stage(kt, slot, refill, diagonal_offset=None):
        if refill == "ahead":
            s_next = scores(kt + 2, diagonal_offset)
        elif refill == "next":
            s_next = scores(slot, None, qn_ref)
        accumulate(kt, slot)
        if refill is not None:
            fill(slot, s_next)

    def pair(pr, refill="ahead", offsets=(None, None)):
        stage(2 * pr, 0, refill, offsets[0])
        stage(2 * pr + 1, 1, refill, offsets[1])

    @pl.when(i == 0)
    def _():
        fill(0, scores(0, 0))
        fill(1, scores(1, tk))

    @pl.when(i > 0)
    def _():
        def body(quad, carry):
            pair(2 * quad)
            pair(2 * quad + 1)
            return carry

        lax.fori_loop(0, (i - 1) // 2, body, 0)

        @pl.when((i - 1) % 2 == 1)
        def _():
            pair(i - 2)

        pair(i - 1, offsets=(0, tk))

    @pl.when(i < last)
    def _():
        pair(i, refill="next")

    @pl.when(i == last)
    def _():
        pair(i, refill=None)

    mz = mzt_ref[...]
    gate = mz * _sigmoid(mz)
    acc = acc_sc[...]
    o_ref[...] = (acc[:M_V] / acc[M_V:M_V + 1] * gate).astype(BF16)


def _mla_attention(qt, k, vt, mzt):
    s = k.shape[1]
    tk, tq = ATT_TILE, MLA_Q_TILE
    nq = s // tq
    return pl.pallas_call(
        _mla_kernel,
        grid=(M_HEADS, nq),
        in_specs=[pl.BlockSpec((1, M_QK_PAD, tq), lambda h, i: (h, 0, i)),
                  pl.BlockSpec((1, M_QK_PAD, tq), lambda h, i: (h, 0, jnp.minimum(i + 1, nq - 1))),
                  pl.BlockSpec((1, s, M_QK_PAD), lambda h, i: (h, 0, 0)),
                  pl.BlockSpec((1, s // tk, M_V_EXT, tk), lambda h, i: (h, 0, 0, 0)),
                  pl.BlockSpec((M_V, tq), lambda h, i: (h, i))],
        out_specs=pl.BlockSpec((M_V, tq), lambda h, i: (h, i)),
        out_shape=jax.ShapeDtypeStruct((M_WIDTH, s), BF16),
        scratch_shapes=[pltpu.VMEM((tk, tq), F32), pltpu.VMEM((tk, tq), F32),
                        pltpu.VMEM((1, tq), F32), pltpu.VMEM((1, tq), F32),
                        pltpu.VMEM((1, tq), F32), pltpu.VMEM((M_V_EXT, tq), F32)],
        compiler_params=_cparams("arbitrary", "arbitrary"),
        name="mla_attention",
    )(qt, qt, k, vt, mzt)


def _out_kernel(final, x_ref, o0_ref, o1_ref, o2_ref, l0_ref, l1_ref, l2_ref, az_ref, ymt_ref, mg_ref, gate_ref,
                wa_ref, wb_ref, wo_ref, fg_ref, out_ref, o1_sc, o2_sc, l1_sc, l2_sc):
    tm = x_ref.shape[0]
    nj = A_WIDTH // LANES
    for src, dst in ((o1_ref, o1_sc), (o2_ref, o2_sc), (l1_ref, l1_sc), (l2_ref, l2_sc)):
        r = src.shape[0]
        for p in range(r):
            for j in range(nj):
                dst[j, pl.ds(p, tm // r, stride=r), :] = src[p, :, j * LANES:(j + 1) * LANES]

    def rows(sc):
        return jnp.concatenate([sc[j] for j in range(nj)], axis=1)

    l0, l1, l2 = l0_ref[0], rows(l1_sc), rows(l2_sc)
    mx = jnp.maximum(jnp.maximum(l0, l1), l2)
    e0, e1, e2 = jnp.exp(l0 - mx), jnp.exp(l1 - mx), jnp.exp(l2 - mx)
    mix = (o0_ref[0] * e0 + rows(o1_sc) * e1 + rows(o2_sc) * e2) / (e0 + e1 + e2)
    az = az_ref[...]
    y_a = (mix * (az * _sigmoid(az))).astype(BF16)
    t_a = jnp.dot(y_a, wa_ref[...], preferred_element_type=F32)
    t_b = lax.dot_general(ymt_ref[...], wb_ref[...], (((0,), (0,)), ((), ())), preferred_element_type=F32)
    mg = mg_ref[...]
    g_a = _sigmoid(mg[:, :D_MODEL])
    g_m = _sigmoid(mg[:, D_MODEL:])
    merged = (g_a * t_a + g_m * t_b).astype(BF16)
    y = x_ref[...] + gate_ref[...] * jnp.dot(merged, wo_ref[...], preferred_element_type=F32)
    if final:
        y = y * lax.rsqrt(jnp.mean(y * y, axis=-1, keepdims=True) + EPS) * fg_ref[...]
    out_ref[...] = y


def _out_proj(x, o_groups, lse_groups, a_z, ymt, mg, gate, w_a, w_b, w_o, final_g, final):
    s = x.shape[0]
    tm = ROW_TILE
    row = lambda i: (i, 0)
    phase_specs = [pl.BlockSpec((r, tm // r, A_WIDTH), lambda i: (0, i, 0)) for _, r in A_PATTERNS]
    return pl.pallas_call(
        functools.partial(_out_kernel, final),
        grid=(s // tm,),
        in_specs=[pl.BlockSpec((tm, D_MODEL), row)] + phase_specs * 2
                 + [pl.BlockSpec((tm, A_WIDTH), row),
                    pl.BlockSpec((M_WIDTH, tm), lambda i: (0, i)),
                    pl.BlockSpec((tm, 2 * D_MODEL), row),
                    _const_spec((1, D_MODEL)),
                    _const_spec(w_a.shape), _const_spec(w_b.shape), _const_spec(w_o.shape),
                    _const_spec((1, D_MODEL))],
        out_specs=pl.BlockSpec((tm, D_MODEL), row),
        out_shape=jax.ShapeDtypeStruct((s, D_MODEL), F32),
        scratch_shapes=[pltpu.VMEM((A_WIDTH // LANES, tm, LANES), F32)] * 4,
        compiler_params=_cparams("parallel"),
        name="out_proj",
    )(x, *o_groups, *lse_groups, a_z, ymt, mg, gate, w_a, w_b, w_o, final_g)


def _prep_layer(w_in, w_uq, w_ukv, w_out_a, w_out_b, w_o):
    o_az = A_QKV
    o_cq = o_az + A_WIDTH
    o_ckv = o_cq + M_Q_LORA
    o_kr = o_ckv + M_KV_LORA
    o_mz = o_kr + M_ROPE
    o_mg = o_mz + M_WIDTH
    half = M_ROPE // 2
    w_qkv = w_in[:, :A_QKV].reshape(D_MODEL, 3, A_GROUPS, A_WIDTH).transpose(0, 2, 1, 3).reshape(D_MODEL, A_QKV)
    w_a = jnp.concatenate([w_qkv, w_in[:, o_az:o_cq]], axis=1).astype(BF16)
    w_g = w_in[:, o_mg:].astype(BF16)
    w_kr = w_in[:, o_kr:o_mz]
    z_lo = jnp.zeros((D_MODEL, M_NOPE), F32)
    z_hi = jnp.zeros((D_MODEL, M_QK_PAD - M_QK), F32)
    kr_a = jnp.concatenate([z_lo, w_kr, z_hi], axis=1)
    kr_b = jnp.concatenate([z_lo, w_kr[:, half:], w_kr[:, :half], z_hi], axis=1)
    w_c = jnp.concatenate([w_in[:, o_cq:o_kr], kr_a, kr_b], axis=1).astype(BF16)
    w_zt = w_in[:, o_mz:o_mg].T.astype(BF16)
    scale = M_QK ** -0.5 * math.log2(math.e)
    uq = (w_uq * scale).reshape(M_Q_LORA, M_HEADS, M_QK)
    uq = jnp.pad(uq, ((0, 0), (0, 0), (0, M_QK_PAD - M_QK)))
    w_uqt = uq.reshape(M_Q_LORA, M_HEADS * M_QK_PAD).T.astype(BF16)
    ukv = w_ukv.reshape(M_KV_LORA, M_HEADS, M_NOPE + M_V)
    w_k = jnp.pad(ukv[:, :, :M_NOPE], ((0, 0), (0, 0), (0, M_QK_PAD - M_NOPE)))
    w_k = w_k.reshape(M_KV_LORA, M_HEADS * M_QK_PAD).astype(BF16)
    w_vt = ukv[:, :, M_NOPE:].reshape(M_KV_LORA, M_WIDTH).T.astype(BF16)
    return (w_a, w_g, w_c, w_zt, w_uqt, w_k, w_vt,
            w_out_a.astype(BF16), w_out_b.astype(BF16), w_o.astype(BF16))


def _rope_tables(positions):
    half = M_ROPE // 2
    inv_freq = 1.0 / (ROPE_THETA ** (jnp.arange(0, M_ROPE, 2, dtype=F32) / M_ROPE))
    ang = positions.astype(F32)[:, None] * inv_freq
    cos, sin = jnp.cos(ang), jnp.sin(ang)
    s = positions.shape[0]
    z_lo = jnp.zeros((s, M_NOPE), F32)
    z_hi = jnp.zeros((s, M_QK_PAD - M_QK), F32)
    cos_k = jnp.concatenate([z_lo, cos, cos, z_hi], axis=1)
    sin_k = jnp.concatenate([z_lo, -sin, sin, z_hi], axis=1)
    return cos.T, sin.T, cos_k, sin_k


def kernel(x, c, positions, w_ada, b_ada, norm_g, w_in, q_norm_g, w_uq, kv_norm_g, w_ukv, w_out_a, w_out_b, w_o,
           rel_bias, final_norm_g):
    batch, s, _ = x.shape
    assert batch == 1 and s % (DIL_BLOCKS * A_BLOCK * A_PATTERNS[-1][1]) == 0
    assert s % MLA_Q_TILE == 0 and s % ROW_TILE == 0
    depth = w_ada.shape[0]
    xs = x[0]
    mod = _modulation(c, w_ada, b_ada)
    cos_t, sin_t, cos_k, sin_k = _rope_tables(positions[0])
    biases = [_dilated_bias(rel_bias, g, w, r) for g, (w, r) in enumerate(A_PATTERNS)]
    final_g = final_norm_g.reshape(1, D_MODEL)
    for l in range(depth):
        (w_a, w_g, w_c, w_zt, w_uqt, w_k, w_vt, wo_a, wo_b, wo) = _prep_layer(
            w_in[l], w_uq[l], w_ukv[l], w_out_a[l], w_out_b[l], w_o[l])
        shift, scale, gate = mod[l, 0], mod[l, 1], mod[l, 2]
        g = norm_g[l].reshape(1, D_MODEL)
        qkv0, qkv1, qkv2, a_z, mg = _in_proj_a(xs, g, scale, shift, w_a, w_g)
        qt, k, vt, mzt = _in_proj_m(xs, g, scale, shift, w_c, w_zt,
                                    q_norm_g[l].reshape(1, M_Q_LORA), kv_norm_g[l].reshape(1, M_KV_LORA),
                                    w_uqt, w_k, w_vt, cos_t, sin_t, cos_k, sin_k)
        outs = [_dilated_group(qkv, biases[gi], gi) for gi, qkv in enumerate((qkv0, qkv1, qkv2))]
        ymt = _mla_attention(qt, k, vt, mzt)
        xs = _out_proj(xs, [o for o, _ in outs], [ls for _, ls in outs], a_z, ymt, mg, gate,
                       wo_a, wo_b, wo, final_g, l == depth - 1)
    return xs[None]
```

```python
import functools
import math

import jax
import jax.numpy as jnp
from jax import lax
from jax.experimental import pallas as pl
from jax.experimental.pallas import tpu as pltpu

F32 = jnp.float32
BF16 = jnp.bfloat16

D_MODEL = 1024
A_PATTERNS = ((128, 1), (512, 4), (2048, 16))
A_GROUPS = 3
A_HEADS = 8
A_HEAD_DIM = 64
A_WIDTH = A_HEADS * A_HEAD_DIM
A_BLOCK = 128
A_QKV = 3 * A_GROUPS * A_WIDTH
M_HEADS = 16
M_Q_LORA = 256
M_KV_LORA = 128
M_NOPE = 64
M_ROPE = 32
M_V = 64
M_QK = M_NOPE + M_ROPE
M_QK_PAD = 128
M_WIDTH = M_HEADS * M_V
M_V_EXT = M_V + 16
ROPE_THETA = 10000.0
REL_BUCKETS = 32
REL_MAX_DIST = 2048
EPS = 1e-6
NEG_INF = -1e30
LANES = 128

ROW_TILE = 512
ATT_TILE = 512
MLA_Q_TILE = 2 * ATT_TILE
MLA_CHUNK = 256
DIL_BLOCKS = 4
VMEM_LIMIT = 56 * 1024 * 1024


def _cparams(*sem):
    return pltpu.CompilerParams(dimension_semantics=sem, vmem_limit_bytes=VMEM_LIMIT)


def _const_spec(shape):
    nd = len(shape)
    return pl.BlockSpec(shape, lambda *_: (0,) * nd, pipeline_mode=pl.Buffered(1))


def _nt_dot(a, b):
    return lax.dot_general(a, b, (((1,), (1,)), ((), ())), preferred_element_type=F32)


def _sigmoid(x):
    return 0.5 * jnp.tanh(0.5 * x) + 0.5


def _modulated_norm(x, g, scale, shift):
    y = x * lax.rsqrt(jnp.mean(x * x, axis=-1, keepdims=True) + EPS)
    return (y * g) * (1.0 + scale) + shift


def _mod_kernel(c_ref, w_ref, b_ref, o_ref):
    c = c_ref[...]
    c_act = c * _sigmoid(c)
    o_ref[0, 0] = jnp.sum(c_act * w_ref[0], axis=0, keepdims=True) + b_ref[0, 0]


def _modulation(c, w_ada, b_ada):
    depth = w_ada.shape[0]
    c_col = c.reshape(D_MODEL, 1)
    b4 = b_ada.reshape(depth, 3, 1, D_MODEL)
    return pl.pallas_call(
        _mod_kernel,
        grid=(depth, 3),
        in_specs=[pl.BlockSpec((D_MODEL, 1), lambda l, j: (0, 0)),
                  pl.BlockSpec((1, D_MODEL, D_MODEL), lambda l, j: (l, 0, j)),
                  pl.BlockSpec((1, 1, 1, D_MODEL), lambda l, j: (l, j, 0, 0))],
        out_specs=pl.BlockSpec((1, 1, 1, D_MODEL), lambda l, j: (l, j, 0, 0)),
        out_shape=jax.ShapeDtypeStruct((depth, 3, 1, D_MODEL), F32),
        compiler_params=_cparams("parallel", "parallel"),
        name="adaln_mod",
    )(c_col, w_ada, b4)


def _in_a_kernel(x_ref, g_ref, sc_ref, sh_ref, wa_ref, wg_ref, qkv0_ref, qkv1_ref, qkv2_ref, az_ref, mg_ref,
                 stage_sc):
    h = _modulated_norm(x_ref[...], g_ref[...], sc_ref[...], sh_ref[...]).astype(BF16)
    a = jnp.dot(h, wa_ref[...], preferred_element_type=F32)
    tm = a.shape[0]
    gw = 3 * A_WIDTH
    qkv0_ref[0] = a[:, :gw].astype(BF16)
    for g, out_ref in ((1, qkv1_ref), (2, qkv2_ref)):
        r = A_PATTERNS[g][1]
        for j in range(gw // LANES):
            stage_sc[j] = a[:, g * gw + j * LANES:g * gw + (j + 1) * LANES]
        for p in range(r):
            for j in range(gw // LANES):
                out_ref[p, :, j * LANES:(j + 1) * LANES] = stage_sc[j, pl.ds(p, tm // r, stride=r), :].astype(BF16)
    az_ref[...] = a[:, A_GROUPS * gw:]
    mg_ref[...] = jnp.dot(h, wg_ref[...], preferred_element_type=F32)


def _in_proj_a(x, g, scale, shift, w_a, w_g):
    s = x.shape[0]
    tm = ROW_TILE
    gw = 3 * A_WIDTH
    row = lambda i: (i, 0)
    phase_specs = [pl.BlockSpec((r, tm // r, gw), lambda i: (0, i, 0)) for _, r in A_PATTERNS]
    phase_shapes = [jax.ShapeDtypeStruct((r, s // r, gw), BF16) for _, r in A_PATTERNS]
    return pl.pallas_call(
        _in_a_kernel,
        grid=(s // tm,),
        in_specs=[pl.BlockSpec((tm, D_MODEL), row),
                  _const_spec((1, D_MODEL)), _const_spec((1, D_MODEL)), _const_spec((1, D_MODEL)),
                  _const_spec(w_a.shape), _const_spec(w_g.shape)],
        out_specs=phase_specs + [pl.BlockSpec((tm, A_WIDTH), row),
                                 pl.BlockSpec((tm, 2 * D_MODEL), row)],
        out_shape=phase_shapes + [jax.ShapeDtypeStruct((s, A_WIDTH), F32),
                                  jax.ShapeDtypeStruct((s, 2 * D_MODEL), F32)],
        scratch_shapes=[pltpu.VMEM((gw // LANES, tm, LANES), F32)],
        compiler_params=_cparams("parallel"),
        name="in_proj_a",
    )(x, g, scale, shift, w_a, w_g)


def _in_m_kernel(x_ref, g_ref, sc_ref, sh_ref, wc_ref, wzt_ref, qg_ref, kvg_ref, wuqt_ref, wk_ref, wvt_ref,
                 cost_ref, sint_ref, ck_ref, sk_ref, qt_ref, k_ref, vt_ref, mzt_ref):
    h = _modulated_norm(x_ref[...], g_ref[...], sc_ref[...], sh_ref[...]).astype(BF16)
    mzt_ref[...] = _nt_dot(wzt_ref[...], h)
    c = jnp.dot(h, wc_ref[...], preferred_element_type=F32)
    cq = c[:, :M_Q_LORA]
    ckv = c[:, M_Q_LORA:M_Q_LORA + M_KV_LORA]
    kr_a = c[:, M_Q_LORA + M_KV_LORA:M_Q_LORA + M_KV_LORA + M_QK_PAD]
    kr_b = c[:, M_Q_LORA + M_KV_LORA + M_QK_PAD:]
    cqn = (cq * lax.rsqrt(jnp.mean(cq * cq, axis=-1, keepdims=True) + EPS) * qg_ref[...]).astype(BF16)
    ckvn = (ckv * lax.rsqrt(jnp.mean(ckv * ckv, axis=-1, keepdims=True) + EPS) * kvg_ref[...]).astype(BF16)

    tm = cq.shape[0]
    half = M_ROPE // 2
    qt = _nt_dot(wuqt_ref[...], cqn).reshape(M_HEADS, M_QK_PAD, tm)
    cos = cost_ref[...][None]
    sin = sint_ref[...][None]
    t1 = qt[:, M_NOPE:M_NOPE + half]
    t2 = qt[:, M_NOPE + half:M_QK]
    qt_ref[:, :M_NOPE] = qt[:, :M_NOPE].astype(BF16)
    qt_ref[:, M_NOPE:M_NOPE + half] = (t1 * cos - t2 * sin).astype(BF16)
    qt_ref[:, M_NOPE + half:M_QK] = (t1 * sin + t2 * cos).astype(BF16)
    qt_ref[:, M_QK:] = qt[:, M_QK:].astype(BF16)

    kr = kr_a * ck_ref[...] + kr_b * sk_ref[...]
    k_all = jnp.dot(ckvn, wk_ref[...], preferred_element_type=F32)
    for hd in range(M_HEADS):
        k_ref[hd] = (k_all[:, hd * M_QK_PAD:(hd + 1) * M_QK_PAD] + kr).astype(BF16)

    vt = _nt_dot(wvt_ref[...], ckvn)
    vt_ref[:, 0, :M_V] = vt.reshape(M_HEADS, M_V, tm).astype(BF16)
    vt_ref[:, 0, M_V:] = jnp.ones((M_HEADS, M_V_EXT - M_V, tm), BF16)


def _in_proj_m(x, g, scale, shift, w_c, w_zt, qg, kvg, w_uqt, w_k, w_vt, cos_t, sin_t, cos_k, sin_k):
    s = x.shape[0]
    tm = ATT_TILE
    row = lambda i: (i, 0)
    col = lambda i: (0, i)
    return pl.pallas_call(
        _in_m_kernel,
        grid=(s // tm,),
        in_specs=[pl.BlockSpec((tm, D_MODEL), row),
                  _const_spec((1, D_MODEL)), _const_spec((1, D_MODEL)), _const_spec((1, D_MODEL)),
                  _const_spec(w_c.shape), _const_spec(w_zt.shape),
                  _const_spec(qg.shape), _const_spec(kvg.shape),
                  _const_spec(w_uqt.shape), _const_spec(w_k.shape), _const_spec(w_vt.shape),
                  pl.BlockSpec((M_ROPE // 2, tm), col), pl.BlockSpec((M_ROPE // 2, tm), col),
                  pl.BlockSpec((tm, M_QK_PAD), row), pl.BlockSpec((tm, M_QK_PAD), row)],
        out_specs=[pl.BlockSpec((M_HEADS, M_QK_PAD, tm), lambda i: (0, 0, i)),
                   pl.BlockSpec((M_HEADS, tm, M_QK_PAD), lambda i: (0, i, 0)),
                   pl.BlockSpec((M_HEADS, 1, M_V_EXT, tm), lambda i: (0, i, 0, 0)),
                   pl.BlockSpec((M_WIDTH, tm), col)],
        out_shape=[jax.ShapeDtypeStruct((M_HEADS, M_QK_PAD, s), BF16),
                   jax.ShapeDtypeStruct((M_HEADS, s, M_QK_PAD), BF16),
                   jax.ShapeDtypeStruct((M_HEADS, s // tm, M_V_EXT, tm), BF16),
                   jax.ShapeDtypeStruct((M_WIDTH, s), F32)],
        compiler_params=_cparams("parallel"),
        name="in_proj_m",
    )(x, g, scale, shift, w_c, w_zt, qg, kvg, w_uqt, w_k, w_vt, cos_t, sin_t, cos_k, sin_k)


def _dilated_kernel(q_ref, kp_ref, kc_ref, vp_ref, vc_ref, bias0_ref, bias_ref, o_ref, lse_ref):
    pairs = A_WIDTH // LANES
    lo = lax.broadcasted_iota(jnp.int32, (1, LANES), 1) < A_HEAD_DIM
    ones = jnp.ones((2 * A_BLOCK, LANES), BF16)
    zero = jnp.zeros((), BF16)

    for b in range(DIL_BLOCKS):
        rows = slice(b * A_BLOCK, (b + 1) * A_BLOCK)
        b_ref = bias0_ref if b == 0 else bias_ref
        q_all = q_ref[0, rows] * jnp.asarray(A_HEAD_DIM ** -0.5, BF16)

        def keys(prev_ref, cur_ref, sl):
            if b == 0:
                return jnp.concatenate([prev_ref[0, :, sl], cur_ref[0, :A_BLOCK, sl]], axis=0)
            return cur_ref[0, (b - 1) * A_BLOCK:(b + 1) * A_BLOCK, sl]

        scores, v_ext = [], []
        for pr in range(pairs):
            sl = slice(pr * LANES, (pr + 1) * LANES)
            k = keys(kp_ref, kc_ref, sl)
            v_ext.append(jnp.concatenate([keys(vp_ref, vc_ref, sl), ones], axis=1))
            q = q_all[:, sl]
            for half in range(2):
                qh = jnp.where(lo if half == 0 else jnp.logical_not(lo), q, zero)
                scores.append(_nt_dot(qh, k) + b_ref[0, 2 * pr + half])

        probs, maxes = [], []
        for s in scores:
            m = jnp.max(s, axis=-1, keepdims=True)
            probs.append(jnp.exp(s - m).astype(BF16))
            maxes.append(m)

        for pr in range(pairs):
            sl = slice(pr * LANES, (pr + 1) * LANES)
            outs, lses = [], []
            for half in range(2):
                ol = jnp.dot(probs[2 * pr + half], v_ext[pr], preferred_element_type=F32)
                l = ol[:, LANES:]
                outs.append(ol[:, :LANES] / l)
                lses.append(maxes[2 * pr + half] + jnp.log(l))
            o_ref[0, rows, sl] = jnp.where(lo, outs[0], outs[1])
            lse_ref[0, rows, sl] = jnp.where(lo, lses[0], lses[1])


def _dilated_group(qkv, bias, g):
    r, l_sub, _ = qkv.shape
    rows = DIL_BLOCKS * A_BLOCK
    blk = (1, rows, A_WIDTH)
    bias_blk = (1,) + bias.shape[1:]

    def cur(t):
        return pl.BlockSpec(blk, lambda p, n: (p, n, t))

    def prev(t):
        return pl.BlockSpec((1, A_BLOCK, A_WIDTH), lambda p, n: (p, jnp.maximum(DIL_BLOCKS * n - 1, 0), t))

    out_spec = pl.BlockSpec(blk, lambda p, n: (p, n, 0))
    return pl.pallas_call(
        _dilated_kernel,
        grid=(r, l_sub // rows),
        in_specs=[cur(0), prev(1), cur(1), prev(2), cur(2),
                  pl.BlockSpec(bias_blk, lambda p, n: (jnp.where(n == 0, 1, 0), 0, 0, 0)),
                  pl.BlockSpec(bias_blk, lambda p, n: (0, 0, 0, 0))],
        out_specs=[out_spec, out_spec],
        out_shape=[jax.ShapeDtypeStruct((r, l_sub, A_WIDTH), F32)] * 2,
        compiler_params=_cparams("parallel", "parallel"),
        name=f"dilated_g{g}",
    )(qkv, qkv, qkv, qkv, qkv, bias, bias)


def _t5_bucket(dist):
    exact = REL_BUCKETS // 2
    d = jnp.maximum(dist, 1).astype(F32)
    large = exact + (jnp.log(d / exact) / math.log(REL_MAX_DIST / exact) * (REL_BUCKETS - exact)).astype(jnp.int32)
    large = jnp.minimum(large, REL_BUCKETS - 1)
    return jnp.where(dist < exact, dist, large)


def _dilated_bias(rel_bias, g, window, r):
    qi = jnp.arange(A_BLOCK)[:, None]
    ki = jnp.arange(2 * A_BLOCK)[None, :]
    j = qi + A_BLOCK - ki
    band = (j >= 0) & (j <= window // r)
    tab = rel_bias[:, g * A_HEADS:(g + 1) * A_HEADS].astype(F32)
    pick = jax.nn.one_hot(_t5_bucket(jnp.maximum(j, 0) * r), REL_BUCKETS, dtype=F32)
    b = jnp.einsum("qkb,bh->hqk", pick, tab, precision=lax.Precision.HIGHEST)
    b = jnp.where(band[None], b, NEG_INF)
    first = jnp.where(ki[None] < A_BLOCK, NEG_INF, b)
    return jnp.stack([b, first])


def _mla_kernel(nq, qt_ref, qn_ref, k_ref, vt_ref, mzt_ref, o_ref, s0_sc, s1_sc, mt0_sc, mt1_sc, m_sc, acc_sc):
    i = pl.program_id(1)
    tk, tq, cw = ATT_TILE, MLA_Q_TILE, MLA_CHUNK
    bufs = ((s0_sc, mt0_sc), (s1_sc, mt1_sc))
    m_sc[...] = jnp.full(m_sc.shape, NEG_INF, F32)
    acc_sc[...] = jnp.zeros(acc_sc.shape, F32)

    def scores(kt, diagonal=False, q_ref=qt_ref, col0=0):
        k = k_ref[0, pl.ds(pl.multiple_of(kt * tk, tk), tk), :]
        s = jnp.dot(k, q_ref[0, :, col0:], preferred_element_type=F32)
        if diagonal:
            kpos = lax.broadcasted_iota(jnp.int32, s.shape, 0)
            qpos = lax.broadcasted_iota(jnp.int32, s.shape, 1)
            s = jnp.where(kpos <= qpos, s, NEG_INF)
        return s

    def fill(slot, s, col0=0):
        s_ref, mt_ref = bufs[slot]
        s_ref[:, col0:] = s
        mt_ref[:, col0:] = jnp.max(s, axis=0, keepdims=True)

    def accumulate(kt, slot, col0=0):
        s_ref, mt_ref = bufs[slot]
        for c in range(col0 // cw, tq // cw):
            cols = slice(c * cw, (c + 1) * cw)
            m_prev = m_sc[:, cols]
            m_new = jnp.maximum(m_prev, mt_ref[:, cols])
            alpha = jnp.exp2(m_prev - m_new)
            p = jnp.exp2(s_ref[:, cols] - m_new).astype(BF16)
            acc_sc[:, cols] = alpha * acc_sc[:, cols] + jnp.dot(vt_ref[0, kt], p, preferred_element_type=F32)
            m_sc[:, cols] = m_new

    def stage(kt, slot, refill, col0=0):
        if refill == "ahead":
            s_next, fill0 = scores(kt + 2), 0
        elif refill == "diagonal":
            fill0 = slot * tk
            s_next = scores(kt + 2, True, col0=fill0)
        elif refill == "next":
            s_next, fill0 = scores(slot, q_ref=qn_ref), 0
        accumulate(kt, slot, col0)
        if refill is not None:
            fill(slot, s_next, fill0)

    def pair(pr, refill="ahead", diagonal=False):
        stage(2 * pr, 0, refill)
        stage(2 * pr + 1, 1, refill, col0=tk if diagonal else 0)

    last_refill = "next" if nq > 1 else None

    @pl.when(i == 0)
    def _():
        fill(0, scores(0, True))
        fill(1, scores(1, True, col0=tk), tk)
        pair(0, refill=last_refill, diagonal=True)

    @pl.when(i > 0)
    def _():
        def body(quad, carry):
            pair(2 * quad)
            pair(2 * quad + 1)
            return carry

        lax.fori_loop(0, (i - 1) // 2, body, 0)

        @pl.when((i - 1) % 2 == 1)
        def _():
            pair(i - 2)

        @pl.when(i < nq - 1)
        def _():
            pair(i - 1, refill="diagonal")
            pair(i, refill="next", diagonal=True)

        @pl.when(i == nq - 1)
        def _():
            pair(i - 1, refill="diagonal")
            pair(i, refill=None, diagonal=True)

    mz = mzt_ref[...]
    gate = mz * _sigmoid(mz)
    acc = acc_sc[...]
    o_ref[...] = (acc[:M_V] / acc[M_V:M_V + 1] * gate).astype(BF16)


def _mla_attention(qt, k, vt, mzt):
    s = k.shape[1]
    tk, tq = ATT_TILE, MLA_Q_TILE
    nq = s // tq
    return pl.pallas_call(
        functools.partial(_mla_kernel, nq),
        grid=(M_HEADS, nq),
        in_specs=[pl.BlockSpec((1, M_QK_PAD, tq), lambda h, i: (h, 0, i)),
                  pl.BlockSpec((1, M_QK_PAD, tq), lambda h, i: (h, 0, jnp.minimum(i + 1, nq - 1))),
                  pl.BlockSpec((1, s, M_QK_PAD), lambda h, i: (h, 0, 0)),
                  pl.BlockSpec((1, s // tk, M_V_EXT, tk), lambda h, i: (h, 0, 0, 0)),
                  pl.BlockSpec((M_V, tq), lambda h, i: (h, i))],
        out_specs=pl.BlockSpec((M_V, tq), lambda h, i: (h, i)),
        out_shape=jax.ShapeDtypeStruct((M_WIDTH, s), BF16),
        scratch_shapes=[pltpu.VMEM((tk, tq), F32), pltpu.VMEM((tk, tq), F32),
                        pltpu.VMEM((1, tq), F32), pltpu.VMEM((1, tq), F32),
                        pltpu.VMEM((1, tq), F32), pltpu.VMEM((M_V_EXT, tq), F32)],
        compiler_params=_cparams("arbitrary", "arbitrary"),
        name="mla_attention",
    )(qt, qt, k, vt, mzt)


def _out_kernel(final, x_ref, o0_ref, o1_ref, o2_ref, l0_ref, l1_ref, l2_ref, az_ref, ymt_ref, mg_ref, gate_ref,
                wa_ref, wb_ref, wo_ref, fg_ref, out_ref, o1_sc, o2_sc, l1_sc, l2_sc):
    tm = x_ref.shape[0]
    nj = A_WIDTH // LANES
    for src, dst in ((o1_ref, o1_sc), (o2_ref, o2_sc), (l1_ref, l1_sc), (l2_ref, l2_sc)):
        r = src.shape[0]
        for p in range(r):
            for j in range(nj):
                dst[j, pl.ds(p, tm // r, stride=r), :] = src[p, :, j * LANES:(j + 1) * LANES]

    def rows(sc):
        return jnp.concatenate([sc[j] for j in range(nj)], axis=1)

    l0, l1, l2 = l0_ref[0], rows(l1_sc), rows(l2_sc)
    mx = jnp.maximum(jnp.maximum(l0, l1), l2)
    e0, e1, e2 = jnp.exp(l0 - mx), jnp.exp(l1 - mx), jnp.exp(l2 - mx)
    mix = (o0_ref[0] * e0 + rows(o1_sc) * e1 + rows(o2_sc) * e2) / (e0 + e1 + e2)
    az = az_ref[...]
    y_a = (mix * (az * _sigmoid(az))).astype(BF16)
    t_a = jnp.dot(y_a, wa_ref[...], preferred_element_type=F32)
    t_b = lax.dot_general(ymt_ref[...], wb_ref[...], (((0,), (0,)), ((), ())), preferred_element_type=F32)
    mg = mg_ref[...]
    g_a = _sigmoid(mg[:, :D_MODEL])
    g_m = _sigmoid(mg[:, D_MODEL:])
    merged = (g_a * t_a + g_m * t_b).astype(BF16)
    y = x_ref[...] + gate_ref[...] * jnp.dot(merged, wo_ref[...], preferred_element_type=F32)
    if final:
        y = y * lax.rsqrt(jnp.mean(y * y, axis=-1, keepdims=True) + EPS) * fg_ref[...]
    out_ref[...] = y


def _out_proj(x, o_groups, lse_groups, a_z, ymt, mg, gate, w_a, w_b, w_o, final_g, final):
    s = x.shape[0]
    tm = ROW_TILE
    row = lambda i: (i, 0)
    phase_specs = [pl.BlockSpec((r, tm // r, A_WIDTH), lambda i: (0, i, 0)) for _, r in A_PATTERNS]
    return pl.pallas_call(
        functools.partial(_out_kernel, final),
        grid=(s // tm,),
        in_specs=[pl.BlockSpec((tm, D_MODEL), row)] + phase_specs * 2
                 + [pl.BlockSpec((tm, A_WIDTH), row),
                    pl.BlockSpec((M_WIDTH, tm), lambda i: (0, i)),
                    pl.BlockSpec((tm, 2 * D_MODEL), row),
                    _const_spec((1, D_MODEL)),
                    _const_spec(w_a.shape), _const_spec(w_b.shape), _const_spec(w_o.shape),
                    _const_spec((1, D_MODEL))],
        out_specs=pl.BlockSpec((tm, D_MODEL), row),
        out_shape=jax.ShapeDtypeStruct((s, D_MODEL), F32),
        scratch_shapes=[pltpu.VMEM((A_WIDTH // LANES, tm, LANES), F32)] * 4,
        compiler_params=_cparams("parallel"),
        name="out_proj",
    )(x, *o_groups, *lse_groups, a_z, ymt, mg, gate, w_a, w_b, w_o, final_g)


def _prep_layer(w_in, w_uq, w_ukv, w_out_a, w_out_b, w_o):
    o_az = A_QKV
    o_cq = o_az + A_WIDTH
    o_ckv = o_cq + M_Q_LORA
    o_kr = o_ckv + M_KV_LORA
    o_mz = o_kr + M_ROPE
    o_mg = o_mz + M_WIDTH
    half = M_ROPE // 2
    w_qkv = w_in[:, :A_QKV].reshape(D_MODEL, 3, A_GROUPS, A_WIDTH).transpose(0, 2, 1, 3).reshape(D_MODEL, A_QKV)
    w_a = jnp.concatenate([w_qkv, w_in[:, o_az:o_cq]], axis=1).astype(BF16)
    w_g = w_in[:, o_mg:].astype(BF16)
    w_kr = w_in[:, o_kr:o_mz]
    z_lo = jnp.zeros((D_MODEL, M_NOPE), F32)
    z_hi = jnp.zeros((D_MODEL, M_QK_PAD - M_QK), F32)
    kr_a = jnp.concatenate([z_lo, w_kr, z_hi], axis=1)
    kr_b = jnp.concatenate([z_lo, w_kr[:, half:], w_kr[:, :half], z_hi], axis=1)
    w_c = jnp.concatenate([w_in[:, o_cq:o_kr], kr_a, kr_b], axis=1).astype(BF16)
    w_zt = w_in[:, o_mz:o_mg].T.astype(BF16)
    scale = M_QK ** -0.5 * math.log2(math.e)
    uq = (w_uq * scale).reshape(M_Q_LORA, M_HEADS, M_QK)
    uq = jnp.pad(uq, ((0, 0), (0, 0), (0, M_QK_PAD - M_QK)))
    w_uqt = uq.reshape(M_Q_LORA, M_HEADS * M_QK_PAD).T.astype(BF16)
    ukv = w_ukv.reshape(M_KV_LORA, M_HEADS, M_NOPE + M_V)
    w_k = jnp.pad(ukv[:, :, :M_NOPE], ((0, 0), (0, 0), (0, M_QK_PAD - M_NOPE)))
    w_k = w_k.reshape(M_KV_LORA, M_HEADS * M_QK_PAD).astype(BF16)
    w_vt = ukv[:, :, M_NOPE:].reshape(M_KV_LORA, M_WIDTH).T.astype(BF16)
    return (w_a, w_g, w_c, w_zt, w_uqt, w_k, w_vt,
            w_out_a.astype(BF16), w_out_b.astype(BF16), w_o.astype(BF16))


def _rope_tables(positions):
    half = M_ROPE // 2
    inv_freq = 1.0 / (ROPE_THETA ** (jnp.arange(0, M_ROPE, 2, dtype=F32) / M_ROPE))
    ang = positions.astype(F32)[:, None] * inv_freq
    cos, sin = jnp.cos(ang), jnp.sin(ang)
    s = positions.shape[0]
    z_lo = jnp.zeros((s, M_NOPE), F32)
    z_hi = jnp.zeros((s, M_QK_PAD - M_QK), F32)
    cos_k = jnp.concatenate([z_lo, cos, cos, z_hi], axis=1)
    sin_k = jnp.concatenate([z_lo, -sin, sin, z_hi], axis=1)
    return cos.T, sin.T, cos_k, sin_k


def kernel(x, c, positions, w_ada, b_ada, norm_g, w_in, q_norm_g, w_uq, kv_norm_g, w_ukv, w_out_a, w_out_b, w_o,
           rel_bias, final_norm_g):
    batch, s, _ = x.shape
    assert batch == 1 and s % (DIL_BLOCKS * A_BLOCK * A_PATTERNS[-1][1]) == 0
    assert s % MLA_Q_TILE == 0 and s % ROW_TILE == 0
    depth = w_ada.shape[0]
    xs = x[0]
    mod = _modulation(c, w_ada, b_ada)
    cos_t, sin_t, cos_k, sin_k = _rope_tables(positions[0])
    biases = [_dilated_bias(rel_bias, g, w, r) for g, (w, r) in enumerate(A_PATTERNS)]
    final_g = final_norm_g.reshape(1, D_MODEL)
    for l in range(depth):
        (w_a, w_g, w_c, w_zt, w_uqt, w_k, w_vt, wo_a, wo_b, wo) = _prep_layer(
            w_in[l], w_uq[l], w_ukv[l], w_out_a[l], w_out_b[l], w_o[l])
        shift, scale, gate = mod[l, 0], mod[l, 1], mod[l, 2]
        g = norm_g[l].reshape(1, D_MODEL)
        qkv0, qkv1, qkv2, a_z, mg = _in_proj_a(xs, g, scale, shift, w_a, w_g)
        qt, k, vt, mzt = _in_proj_m(xs, g, scale, shift, w_c, w_zt,
                                    q_norm_g[l].reshape(1, M_Q_LORA), kv_norm_g[l].reshape(1, M_KV_LORA),
                                    w_uqt, w_k, w_vt, cos_t, sin_t, cos_k, sin_k)
        outs = [_dilated_group(qkv, biases[gi], gi) for gi, qkv in enumerate((qkv0, qkv1, qkv2))]
        ymt = _mla_attention(qt, k, vt, mzt)
        xs = _out_proj(xs, [o for o, _ in outs], [ls for _, ls in outs], a_z, ymt, mg, gate,
                       wo_a, wo_b, wo, final_g, l == depth - 1)
    return xs[None]
```

```python
import functools
import math

import jax
import jax.numpy as jnp
from jax import lax
from jax.experimental import pallas as pl
from jax.experimental.pallas import tpu as pltpu

F32 = jnp.float32
BF16 = jnp.bfloat16

D_MODEL = 1024
A_PATTERNS = ((128, 1), (512, 4), (2048, 16))
A_GROUPS = 3
A_HEADS = 8
A_HEAD_DIM = 64
A_WIDTH = A_HEADS * A_HEAD_DIM
A_BLOCK = 128
A_QKV = 3 * A_GROUPS * A_WIDTH
M_HEADS = 16
M_Q_LORA = 256
M_KV_LORA = 128
M_NOPE = 64
M_ROPE = 32
M_V = 64
M_QK = M_NOPE + M_ROPE
M_QK_PAD = 128
M_WIDTH = M_HEADS * M_V
M_V_EXT = M_V + 16
ROPE_THETA = 10000.0
REL_BUCKETS = 32
REL_MAX_DIST = 2048
EPS = 1e-6
NEG_INF = -1e30
LANES = 128

ROW_TILE = 512
ATT_TILE = 512
MLA_Q_TILE = 2 * ATT_TILE
MLA_CHUNK = 256
MLA_UNROLL = 4
DIL_BLOCKS = 4
VMEM_LIMIT = 56 * 1024 * 1024


def _cparams(*sem):
    return pltpu.CompilerParams(dimension_semantics=sem, vmem_limit_bytes=VMEM_LIMIT)


def _const_spec(shape):
    nd = len(shape)
    return pl.BlockSpec(shape, lambda *_: (0,) * nd, pipeline_mode=pl.Buffered(1))


def _nt_dot(a, b):
    return lax.dot_general(a, b, (((1,), (1,)), ((), ())), preferred_element_type=F32)


def _sigmoid(x):
    return 0.5 * jnp.tanh(0.5 * x) + 0.5


def _modulated_norm(x, g, scale, shift):
    y = x * lax.rsqrt(jnp.mean(x * x, axis=-1, keepdims=True) + EPS)
    return (y * g) * (1.0 + scale) + shift


def _mod_kernel(c_ref, w_ref, b_ref, o_ref):
    c = c_ref[...]
    c_act = c * _sigmoid(c)
    o_ref[0, 0] = jnp.sum(c_act * w_ref[0], axis=0, keepdims=True) + b_ref[0, 0]


def _modulation(c, w_ada, b_ada):
    depth = w_ada.shape[0]
    c_col = c.reshape(D_MODEL, 1)
    b4 = b_ada.reshape(depth, 3, 1, D_MODEL)
    return pl.pallas_call(
        _mod_kernel,
        grid=(depth, 3),
        in_specs=[pl.BlockSpec((D_MODEL, 1), lambda l, j: (0, 0)),
                  pl.BlockSpec((1, D_MODEL, D_MODEL), lambda l, j: (l, 0, j)),
                  pl.BlockSpec((1, 1, 1, D_MODEL), lambda l, j: (l, j, 0, 0))],
        out_specs=pl.BlockSpec((1, 1, 1, D_MODEL), lambda l, j: (l, j, 0, 0)),
        out_shape=jax.ShapeDtypeStruct((depth, 3, 1, D_MODEL), F32),
        compiler_params=_cparams("parallel", "parallel"),
        name="adaln_mod",
    )(c_col, w_ada, b4)


def _in_a_kernel(x_ref, g_ref, sc_ref, sh_ref, wa_ref, wg_ref, qkv0_ref, qkv1_ref, qkv2_ref, az_ref, mg_ref,
                 stage_sc):
    h = _modulated_norm(x_ref[...], g_ref[...], sc_ref[...], sh_ref[...]).astype(BF16)
    a = jnp.dot(h, wa_ref[...], preferred_element_type=F32)
    tm = a.shape[0]
    gw = 3 * A_WIDTH
    qkv0_ref[0] = a[:, :gw].astype(BF16)
    for g, out_ref in ((1, qkv1_ref), (2, qkv2_ref)):
        r = A_PATTERNS[g][1]
        for j in range(gw // LANES):
            stage_sc[j] = a[:, g * gw + j * LANES:g * gw + (j + 1) * LANES]
        for p in range(r):
            for j in range(gw // LANES):
                out_ref[p, :, j * LANES:(j + 1) * LANES] = stage_sc[j, pl.ds(p, tm // r, stride=r), :].astype(BF16)
    az_ref[...] = a[:, A_GROUPS * gw:]
    mg_ref[...] = jnp.dot(h, wg_ref[...], preferred_element_type=F32)


def _in_proj_a(x, g, scale, shift, w_a, w_g):
    s = x.shape[0]
    tm = ROW_TILE
    gw = 3 * A_WIDTH
    row = lambda i: (i, 0)
    phase_specs = [pl.BlockSpec((r, tm // r, gw), lambda i: (0, i, 0)) for _, r in A_PATTERNS]
    phase_shapes = [jax.ShapeDtypeStruct((r, s // r, gw), BF16) for _, r in A_PATTERNS]
    return pl.pallas_call(
        _in_a_kernel,
        grid=(s // tm,),
        in_specs=[pl.BlockSpec((tm, D_MODEL), row),
                  _const_spec((1, D_MODEL)), _const_spec((1, D_MODEL)), _const_spec((1, D_MODEL)),
                  _const_spec(w_a.shape), _const_spec(w_g.shape)],
        out_specs=phase_specs + [pl.BlockSpec((tm, A_WIDTH), row),
                                 pl.BlockSpec((tm, 2 * D_MODEL), row)],
        out_shape=phase_shapes + [jax.ShapeDtypeStruct((s, A_WIDTH), F32),
                                  jax.ShapeDtypeStruct((s, 2 * D_MODEL), F32)],
        scratch_shapes=[pltpu.VMEM((gw // LANES, tm, LANES), F32)],
        compiler_params=_cparams("parallel"),
        name="in_proj_a",
    )(x, g, scale, shift, w_a, w_g)


def _in_m_kernel(x_ref, g_ref, sc_ref, sh_ref, wc_ref, wzt_ref, qg_ref, kvg_ref, wuqt_ref, wk_ref, wvt_ref,
                 cost_ref, sint_ref, ck_ref, sk_ref, qt_ref, k_ref, vt_ref, mzt_ref):
    h = _modulated_norm(x_ref[...], g_ref[...], sc_ref[...], sh_ref[...]).astype(BF16)
    mzt_ref[...] = _nt_dot(wzt_ref[...], h)
    c = jnp.dot(h, wc_ref[...], preferred_element_type=F32)
    cq = c[:, :M_Q_LORA]
    ckv = c[:, M_Q_LORA:M_Q_LORA + M_KV_LORA]
    kr_a = c[:, M_Q_LORA + M_KV_LORA:M_Q_LORA + M_KV_LORA + M_QK_PAD]
    kr_b = c[:, M_Q_LORA + M_KV_LORA + M_QK_PAD:]
    cqn = (cq * lax.rsqrt(jnp.mean(cq * cq, axis=-1, keepdims=True) + EPS) * qg_ref[...]).astype(BF16)
    ckvn = (ckv * lax.rsqrt(jnp.mean(ckv * ckv, axis=-1, keepdims=True) + EPS) * kvg_ref[...]).astype(BF16)

    tm = cq.shape[0]
    half = M_ROPE // 2
    qt = _nt_dot(wuqt_ref[...], cqn).reshape(M_HEADS, M_QK_PAD, tm)
    cos = cost_ref[...][None]
    sin = sint_ref[...][None]
    t1 = qt[:, M_NOPE:M_NOPE + half]
    t2 = qt[:, M_NOPE + half:M_QK]
    qt_ref[:, :M_NOPE] = qt[:, :M_NOPE].astype(BF16)
    qt_ref[:, M_NOPE:M_NOPE + half] = (t1 * cos - t2 * sin).astype(BF16)
    qt_ref[:, M_NOPE + half:M_QK] = (t1 * sin + t2 * cos).astype(BF16)
    qt_ref[:, M_QK:] = qt[:, M_QK:].astype(BF16)

    kr = kr_a * ck_ref[...] + kr_b * sk_ref[...]
    k_all = jnp.dot(ckvn, wk_ref[...], preferred_element_type=F32)
    for hd in range(M_HEADS):
        k_ref[hd] = (k_all[:, hd * M_QK_PAD:(hd + 1) * M_QK_PAD] + kr).astype(BF16)

    vt = _nt_dot(wvt_ref[...], ckvn)
    vt_ref[:, 0, :M_V] = vt.reshape(M_HEADS, M_V, tm).astype(BF16)
    vt_ref[:, 0, M_V:] = jnp.ones((M_HEADS, M_V_EXT - M_V, tm), BF16)


def _in_proj_m(x, g, scale, shift, w_c, w_zt, qg, kvg, w_uqt, w_k, w_vt, cos_t, sin_t, cos_k, sin_k):
    s = x.shape[0]
    tm = ATT_TILE
    row = lambda i: (i, 0)
    col = lambda i: (0, i)
    return pl.pallas_call(
        _in_m_kernel,
        grid=(s // tm,),
        in_specs=[pl.BlockSpec((tm, D_MODEL), row),
                  _const_spec((1, D_MODEL)), _const_spec((1, D_MODEL)), _const_spec((1, D_MODEL)),
                  _const_spec(w_c.shape), _const_spec(w_zt.shape),
                  _const_spec(qg.shape), _const_spec(kvg.shape),
                  _const_spec(w_uqt.shape), _const_spec(w_k.shape), _const_spec(w_vt.shape),
                  pl.BlockSpec((M_ROPE // 2, tm), col), pl.BlockSpec((M_ROPE // 2, tm), col),
                  pl.BlockSpec((tm, M_QK_PAD), row), pl.BlockSpec((tm, M_QK_PAD), row)],
        out_specs=[pl.BlockSpec((M_HEADS, M_QK_PAD, tm), lambda i: (0, 0, i)),
                   pl.BlockSpec((M_HEADS, tm, M_QK_PAD), lambda i: (0, i, 0)),
                   pl.BlockSpec((M_HEADS, 1, M_V_EXT, tm), lambda i: (0, i, 0, 0)),
                   pl.BlockSpec((M_WIDTH, tm), col)],
        out_shape=[jax.ShapeDtypeStruct((M_HEADS, M_QK_PAD, s), BF16),
                   jax.ShapeDtypeStruct((M_HEADS, s, M_QK_PAD), BF16),
                   jax.ShapeDtypeStruct((M_HEADS, s // tm, M_V_EXT, tm), BF16),
                   jax.ShapeDtypeStruct((M_WIDTH, s), F32)],
        compiler_params=_cparams("parallel"),
        name="in_proj_m",
    )(x, g, scale, shift, w_c, w_zt, qg, kvg, w_uqt, w_k, w_vt, cos_t, sin_t, cos_k, sin_k)


def _dilated_kernel(q_ref, kp_ref, kc_ref, vp_ref, vc_ref, bias0_ref, bias_ref, o_ref, lse_ref):
    pairs = A_WIDTH // LANES
    lo = lax.broadcasted_iota(jnp.int32, (1, LANES), 1) < A_HEAD_DIM
    ones = jnp.ones((2 * A_BLOCK, LANES), BF16)
    zero = jnp.zeros((), BF16)

    for b in range(DIL_BLOCKS):
        rows = slice(b * A_BLOCK, (b + 1) * A_BLOCK)
        b_ref = bias0_ref if b == 0 else bias_ref
        q_all = q_ref[0, rows] * jnp.asarray(A_HEAD_DIM ** -0.5, BF16)

        def keys(prev_ref, cur_ref, sl):
            if b == 0:
                return jnp.concatenate([prev_ref[0, :, sl], cur_ref[0, :A_BLOCK, sl]], axis=0)
            return cur_ref[0, (b - 1) * A_BLOCK:(b + 1) * A_BLOCK, sl]

        scores, v_ext = [], []
        for pr in range(pairs):
            sl = slice(pr * LANES, (pr + 1) * LANES)
            k = keys(kp_ref, kc_ref, sl)
            v_ext.append(jnp.concatenate([keys(vp_ref, vc_ref, sl), ones], axis=1))
            q = q_all[:, sl]
            for half in range(2):
                qh = jnp.where(lo if half == 0 else jnp.logical_not(lo), q, zero)
                scores.append(_nt_dot(qh, k) + b_ref[0, 2 * pr + half])

        probs, maxes = [], []
        for s in scores:
            m = jnp.max(s, axis=-1, keepdims=True)
            probs.append(jnp.exp(s - m).astype(BF16))
            maxes.append(m)

        for pr in range(pairs):
            sl = slice(pr * LANES, (pr + 1) * LANES)
            outs, lses = [], []
            for half in range(2):
                ol = jnp.dot(probs[2 * pr + half], v_ext[pr], preferred_element_type=F32)
                l = ol[:, LANES:]
                outs.append(ol[:, :LANES] / l)
                lses.append(maxes[2 * pr + half] + jnp.log(l))
            o_ref[0, rows, sl] = jnp.where(lo, outs[0], outs[1])
            lse_ref[0, rows, sl] = jnp.where(lo, lses[0], lses[1])


def _dilated_group(qkv, bias, g):
    r, l_sub, _ = qkv.shape
    rows = DIL_BLOCKS * A_BLOCK
    blk = (1, rows, A_WIDTH)
    bias_blk = (1,) + bias.shape[1:]

    def cur(t):
        return pl.BlockSpec(blk, lambda p, n: (p, n, t))

    def prev(t):
        return pl.BlockSpec((1, A_BLOCK, A_WIDTH), lambda p, n: (p, jnp.maximum(DIL_BLOCKS * n - 1, 0), t))

    out_spec = pl.BlockSpec(blk, lambda p, n: (p, n, 0))
    return pl.pallas_call(
        _dilated_kernel,
        grid=(r, l_sub // rows),
        in_specs=[cur(0), prev(1), cur(1), prev(2), cur(2),
                  pl.BlockSpec(bias_blk, lambda p, n: (jnp.where(n == 0, 1, 0), 0, 0, 0)),
                  pl.BlockSpec(bias_blk, lambda p, n: (0, 0, 0, 0))],
        out_specs=[out_spec, out_spec],
        out_shape=[jax.ShapeDtypeStruct((r, l_sub, A_WIDTH), F32)] * 2,
        compiler_params=_cparams("parallel", "parallel"),
        name=f"dilated_g{g}",
    )(qkv, qkv, qkv, qkv, qkv, bias, bias)


def _t5_bucket(dist):
    exact = REL_BUCKETS // 2
    d = jnp.maximum(dist, 1).astype(F32)
    large = exact + (jnp.log(d / exact) / math.log(REL_MAX_DIST / exact) * (REL_BUCKETS - exact)).astype(jnp.int32)
    large = jnp.minimum(large, REL_BUCKETS - 1)
    return jnp.where(dist < exact, dist, large)


def _dilated_bias(rel_bias, g, window, r):
    qi = jnp.arange(A_BLOCK)[:, None]
    ki = jnp.arange(2 * A_BLOCK)[None, :]
    j = qi + A_BLOCK - ki
    band = (j >= 0) & (j <= window // r)
    tab = rel_bias[:, g * A_HEADS:(g + 1) * A_HEADS].astype(F32)
    pick = jax.nn.one_hot(_t5_bucket(jnp.maximum(j, 0) * r), REL_BUCKETS, dtype=F32)
    b = jnp.einsum("qkb,bh->hqk", pick, tab, precision=lax.Precision.HIGHEST)
    b = jnp.where(band[None], b, NEG_INF)
    first = jnp.where(ki[None] < A_BLOCK, NEG_INF, b)
    return jnp.stack([b, first])


def _mla_kernel(nq, qt_ref, qn_ref, k_ref, vt_ref, mzt_ref, o_ref, s0_sc, s1_sc, mt0_sc, mt1_sc, m_sc, acc_sc):
    i = pl.program_id(1)
    tk, tq, cw = ATT_TILE, MLA_Q_TILE, MLA_CHUNK
    bufs = ((s0_sc, mt0_sc), (s1_sc, mt1_sc))
    m_sc[...] = jnp.full(m_sc.shape, NEG_INF, F32)
    acc_sc[...] = jnp.zeros(acc_sc.shape, F32)

    def scores(kt, diagonal=False, q_ref=qt_ref, col0=0):
        k = k_ref[0, pl.ds(pl.multiple_of(kt * tk, tk), tk), :]
        s = jnp.dot(k, q_ref[0, :, col0:], preferred_element_type=F32)
        if diagonal:
            kpos = lax.broadcasted_iota(jnp.int32, s.shape, 0)
            qpos = lax.broadcasted_iota(jnp.int32, s.shape, 1)
            s = jnp.where(kpos <= qpos, s, NEG_INF)
        return s

    def fill(slot, s, col0=0):
        s_ref, mt_ref = bufs[slot]
        s_ref[:, col0:] = s
        mt_ref[:, col0:] = jnp.max(s, axis=0, keepdims=True)

    def accumulate(kt, slot, col0=0):
        s_ref, mt_ref = bufs[slot]
        for c in range(col0 // cw, tq // cw):
            cols = slice(c * cw, (c + 1) * cw)
            m_prev = m_sc[:, cols]
            m_new = jnp.maximum(m_prev, mt_ref[:, cols])
            alpha = jnp.exp2(m_prev - m_new)
            p = jnp.exp2(s_ref[:, cols] - m_new).astype(BF16)
            acc_sc[:, cols] = alpha * acc_sc[:, cols] + jnp.dot(vt_ref[0, kt], p, preferred_element_type=F32)
            m_sc[:, cols] = m_new

    def stage(kt, slot, refill, col0=0):
        if refill == "ahead":
            s_next, fill0 = scores(kt + 2), 0
        elif refill == "diagonal":
            fill0 = slot * tk
            s_next = scores(kt + 2, True, col0=fill0)
        elif refill == "next":
            s_next, fill0 = scores(slot, q_ref=qn_ref), 0
        accumulate(kt, slot, col0)
        if refill is not None:
            fill(slot, s_next, fill0)

    def pair(pr, refill="ahead", diagonal=False):
        stage(2 * pr, 0, refill)
        stage(2 * pr + 1, 1, refill, col0=tk if diagonal else 0)

    last_refill = "next" if nq > 1 else None

    @pl.when(i == 0)
    def _():
        fill(0, scores(0, True))
        fill(1, scores(1, True, col0=tk), tk)
        pair(0, refill=last_refill, diagonal=True)

    @pl.when(i > 0)
    def _():
        full_pairs = i - 1

        def body(trip, carry):
            for u in range(MLA_UNROLL):
                pair(MLA_UNROLL * trip + u)
            return carry

        lax.fori_loop(0, full_pairs // MLA_UNROLL, body, 0)
        done = full_pairs - full_pairs % MLA_UNROLL
        width = MLA_UNROLL // 2
        while width >= 1:
            @pl.when((full_pairs % (2 * width)) >= width)
            def _(done=done, width=width):
                for u in range(width):
                    pair(done + u)

            done = done + jnp.where((full_pairs % (2 * width)) >= width, width, 0)
            width //= 2

        @pl.when(i < nq - 1)
        def _():
            pair(i - 1, refill="diagonal")
            pair(i, refill="next", diagonal=True)

        @pl.when(i == nq - 1)
        def _():
            pair(i - 1, refill="diagonal")
            pair(i, refill=None, diagonal=True)

    mz = mzt_ref[...]
    gate = mz * _sigmoid(mz)
    acc = acc_sc[...]
    o_ref[...] = (acc[:M_V] / acc[M_V:M_V + 1] * gate).astype(BF16)


def _mla_attention(qt, k, vt, mzt):
    s = k.shape[1]
    tk, tq = ATT_TILE, MLA_Q_TILE
    nq = s // tq
    return pl.pallas_call(
        functools.partial(_mla_kernel, nq),
        grid=(M_HEADS, nq),
        in_specs=[pl.BlockSpec((1, M_QK_PAD, tq), lambda h, i: (h, 0, i)),
                  pl.BlockSpec((1, M_QK_PAD, tq), lambda h, i: (h, 0, jnp.minimum(i + 1, nq - 1))),
                  pl.BlockSpec((1, s, M_QK_PAD), lambda h, i: (h, 0, 0)),
                  pl.BlockSpec((1, s // tk, M_V_EXT, tk), lambda h, i: (h, 0, 0, 0)),
                  pl.BlockSpec((M_V, tq), lambda h, i: (h, i))],
        out_specs=pl.BlockSpec((M_V, tq), lambda h, i: (h, i)),
        out_shape=jax.ShapeDtypeStruct((M_WIDTH, s), BF16),
        scratch_shapes=[pltpu.VMEM((tk, tq), F32), pltpu.VMEM((tk, tq), F32),
                        pltpu.VMEM((1, tq), F32), pltpu.VMEM((1, tq), F32),
                        pltpu.VMEM((1, tq), F32), pltpu.VMEM((M_V_EXT, tq), F32)],
        compiler_params=_cparams("arbitrary", "arbitrary"),
        name="mla_attention",
    )(qt, qt, k, vt, mzt)


def _out_kernel(final, x_ref, o0_ref, o1_ref, o2_ref, l0_ref, l1_ref, l2_ref, az_ref, ymt_ref, mg_ref, gate_ref,
                wa_ref, wb_ref, wo_ref, fg_ref, out_ref, o1_sc, o2_sc, l1_sc, l2_sc):
    tm = x_ref.shape[0]
    nj = A_WIDTH // LANES
    for src, dst in ((o1_ref, o1_sc), (o2_ref, o2_sc), (l1_ref, l1_sc), (l2_ref, l2_sc)):
        r = src.shape[0]
        for p in range(r):
            for j in range(nj):
                dst[j, pl.ds(p, tm // r, stride=r), :] = src[p, :, j * LANES:(j + 1) * LANES]

    def rows(sc):
        return jnp.concatenate([sc[j] for j in range(nj)], axis=1)

    l0, l1, l2 = l0_ref[0], rows(l1_sc), rows(l2_sc)
    mx = jnp.maximum(jnp.maximum(l0, l1), l2)
    e0, e1, e2 = jnp.exp(l0 - mx), jnp.exp(l1 - mx), jnp.exp(l2 - mx)
    mix = (o0_ref[0] * e0 + rows(o1_sc) * e1 + rows(o2_sc) * e2) / (e0 + e1 + e2)
    az = az_ref[...]
    y_a = (mix * (az * _sigmoid(az))).astype(BF16)
    t_a = jnp.dot(y_a, wa_ref[...], preferred_element_type=F32)
    t_b = lax.dot_general(ymt_ref[...], wb_ref[...], (((0,), (0,)), ((), ())), preferred_element_type=F32)
    mg = mg_ref[...]
    g_a = _sigmoid(mg[:, :D_MODEL])
    g_m = _sigmoid(mg[:, D_MODEL:])
    merged = (g_a * t_a + g_m * t_b).astype(BF16)
    y = x_ref[...] + gate_ref[...] * jnp.dot(merged, wo_ref[...], preferred_element_type=F32)
    if final:
        y = y * lax.rsqrt(jnp.mean(y * y, axis=-1, keepdims=True) + EPS) * fg_ref[...]
    out_ref[...] = y


def _out_proj(x, o_groups, lse_groups, a_z, ymt, mg, gate, w_a, w_b, w_o, final_g, final):
    s = x.shape[0]
    tm = ROW_TILE
    row = lambda i: (i, 0)
    phase_specs = [pl.BlockSpec((r, tm // r, A_WIDTH), lambda i: (0, i, 0)) for _, r in A_PATTERNS]
    return pl.pallas_call(
        functools.partial(_out_kernel, final),
        grid=(s // tm,),
        in_specs=[pl.BlockSpec((tm, D_MODEL), row)] + phase_specs * 2
                 + [pl.BlockSpec((tm, A_WIDTH), row),
                    pl.BlockSpec((M_WIDTH, tm), lambda i: (0, i)),
                    pl.BlockSpec((tm, 2 * D_MODEL), row),
                    _const_spec((1, D_MODEL)),
                    _const_spec(w_a.shape), _const_spec(w_b.shape), _const_spec(w_o.shape),
                    _const_spec((1, D_MODEL))],
        out_specs=pl.BlockSpec((tm, D_MODEL), row),
        out_shape=jax.ShapeDtypeStruct((s, D_MODEL), F32),
        scratch_shapes=[pltpu.VMEM((A_WIDTH // LANES, tm, LANES), F32)] * 4,
        compiler_params=_cparams("parallel"),
        name="out_proj",
    )(x, *o_groups, *lse_groups, a_z, ymt, mg, gate, w_a, w_b, w_o, final_g)


def _prep_layer(w_in, w_uq, w_ukv, w_out_a, w_out_b, w_o):
    o_az = A_QKV
    o_cq = o_az + A_WIDTH
    o_ckv = o_cq + M_Q_LORA
    o_kr = o_ckv + M_KV_LORA
    o_mz = o_kr + M_ROPE
    o_mg = o_mz + M_WIDTH
    half = M_ROPE // 2
    w_qkv = w_in[:, :A_QKV].reshape(D_MODEL, 3, A_GROUPS, A_WIDTH).transpose(0, 2, 1, 3).reshape(D_MODEL, A_QKV)
    w_a = jnp.concatenate([w_qkv, w_in[:, o_az:o_cq]], axis=1).astype(BF16)
    w_g = w_in[:, o_mg:].astype(BF16)
    w_kr = w_in[:, o_kr:o_mz]
    z_lo = jnp.zeros((D_MODEL, M_NOPE), F32)
    z_hi = jnp.zeros((D_MODEL, M_QK_PAD - M_QK), F32)
    kr_a = jnp.concatenate([z_lo, w_kr, z_hi], axis=1)
    kr_b = jnp.concatenate([z_lo, w_kr[:, half:], w_kr[:, :half], z_hi], axis=1)
    w_c = jnp.concatenate([w_in[:, o_cq:o_kr], kr_a, kr_b], axis=1).astype(BF16)
    w_zt = w_in[:, o_mz:o_mg].T.astype(BF16)
    scale = M_QK ** -0.5 * math.log2(math.e)
    uq = (w_uq * scale).reshape(M_Q_LORA, M_HEADS, M_QK)
    uq = jnp.pad(uq, ((0, 0), (0, 0), (0, M_QK_PAD - M_QK)))
    w_uqt = uq.reshape(M_Q_LORA, M_HEADS * M_QK_PAD).T.astype(BF16)
    ukv = w_ukv.reshape(M_KV_LORA, M_HEADS, M_NOPE + M_V)
    w_k = jnp.pad(ukv[:, :, :M_NOPE], ((0, 0), (0, 0), (0, M_QK_PAD - M_NOPE)))
    w_k = w_k.reshape(M_KV_LORA, M_HEADS * M_QK_PAD).astype(BF16)
    w_vt = ukv[:, :, M_NOPE:].reshape(M_KV_LORA, M_WIDTH).T.astype(BF16)
    return (w_a, w_g, w_c, w_zt, w_uqt, w_k, w_vt,
            w_out_a.astype(BF16), w_out_b.astype(BF16), w_o.astype(BF16))


def _rope_tables(positions):
    half = M_ROPE // 2
    inv_freq = 1.0 / (ROPE_THETA ** (jnp.arange(0, M_ROPE, 2, dtype=F32) / M_ROPE))
    ang = positions.astype(F32)[:, None] * inv_freq
    cos, sin = jnp.cos(ang), jnp.sin(ang)
    s = positions.shape[0]
    z_lo = jnp.zeros((s, M_NOPE), F32)
    z_hi = jnp.zeros((s, M_QK_PAD - M_QK), F32)
    cos_k = jnp.concatenate([z_lo, cos, cos, z_hi], axis=1)
    sin_k = jnp.concatenate([z_lo, -sin, sin, z_hi], axis=1)
    return cos.T, sin.T, cos_k, sin_k


def kernel(x, c, positions, w_ada, b_ada, norm_g, w_in, q_norm_g, w_uq, kv_norm_g, w_ukv, w_out_a, w_out_b, w_o,
           rel_bias, final_norm_g):
    batch, s, _ = x.shape
    assert batch == 1 and s % (DIL_BLOCKS * A_BLOCK * A_PATTERNS[-1][1]) == 0
    assert s % MLA_Q_TILE == 0 and s % ROW_TILE == 0
    depth = w_ada.shape[0]
    xs = x[0]
    mod = _modulation(c, w_ada, b_ada)
    cos_t, sin_t, cos_k, sin_k = _rope_tables(positions[0])
    biases = [_dilated_bias(rel_bias, g, w, r) for g, (w, r) in enumerate(A_PATTERNS)]
    final_g = final_norm_g.reshape(1, D_MODEL)
    for l in range(depth):
        (w_a, w_g, w_c, w_zt, w_uqt, w_k, w_vt, wo_a, wo_b, wo) = _prep_layer(
            w_in[l], w_uq[l], w_ukv[l], w_out_a[l], w_out_b[l], w_o[l])
        shift, scale, gate = mod[l, 0], mod[l, 1], mod[l, 2]
        g = norm_g[l].reshape(1, D_MODEL)
        qkv0, qkv1, qkv2, a_z, mg = _in_proj_a(xs, g, scale, shift, w_a, w_g)
        qt, k, vt, mzt = _in_proj_m(xs, g, scale, shift, w_c, w_zt,
                                    q_norm_g[l].reshape(1, M_Q_LORA), kv_norm_g[l].reshape(1, M_KV_LORA),
                                    w_uqt, w_k, w_vt, cos_t, sin_t, cos_k, sin_k)
        outs = [_dilated_group(qkv, biases[gi], gi) for gi, qkv in enumerate((qkv0, qkv1, qkv2))]
        ymt = _mla_attention(qt, k, vt, mzt)
        xs = _out_proj(xs, [o for o, _ in outs], [ls for _, ls in outs], a_z, ymt, mg, gate,
                       wo_a, wo_b, wo, final_g, l == depth - 1)
    return xs[None]
```

```python
import functools
import math

import jax
import jax.numpy as jnp
from jax import lax
from jax.experimental import pallas as pl
from jax.experimental.pallas import tpu as pltpu

F32 = jnp.float32
BF16 = jnp.bfloat16

D_MODEL = 1024
A_PATTERNS = ((128, 1), (512, 4), (2048, 16))
A_GROUPS = 3
A_HEADS = 8
A_HEAD_DIM = 64
A_WIDTH = A_HEADS * A_HEAD_DIM
A_BLOCK = 128
A_QKV = 3 * A_GROUPS * A_WIDTH
M_HEADS = 16
M_Q_LORA = 256
M_KV_LORA = 128
M_NOPE = 64
M_ROPE = 32
M_V = 64
M_QK = M_NOPE + M_ROPE
M_QK_PAD = 128
M_WIDTH = M_HEADS * M_V
M_V_EXT = M_V + 16
ROPE_THETA = 10000.0
REL_BUCKETS = 32
REL_MAX_DIST = 2048
EPS = 1e-6
NEG_INF = -1e30
LANES = 128
PHASE_STEP = 4

ROW_TILE = 512
ATT_TILE = 512
MLA_Q_TILE = 2 * ATT_TILE
MLA_CHUNK = 256
MLA_UNROLL = 4
DIL_BLOCKS = 8
VMEM_LIMIT = 56 * 1024 * 1024


def _cparams(*sem):
    return pltpu.CompilerParams(dimension_semantics=sem, vmem_limit_bytes=VMEM_LIMIT)


def _const_spec(shape):
    nd = len(shape)
    return pl.BlockSpec(shape, lambda *_: (0,) * nd, pipeline_mode=pl.Buffered(1))


def _nt_dot(a, b):
    return lax.dot_general(a, b, (((1,), (1,)), ((), ())), preferred_element_type=F32)


def _sigmoid(x):
    return 0.5 * jnp.tanh(0.5 * x) + 0.5


def _modulated_norm(x, g, scale, shift):
    y = x * lax.rsqrt(jnp.mean(x * x, axis=-1, keepdims=True) + EPS)
    return (y * g) * (1.0 + scale) + shift


def _mod_kernel(c_ref, w_ref, b_ref, o_ref):
    c = c_ref[...]
    c_act = c * _sigmoid(c)
    o_ref[0, 0] = jnp.sum(c_act * w_ref[0], axis=0, keepdims=True) + b_ref[0, 0]


def _modulation(c, w_ada, b_ada):
    depth = w_ada.shape[0]
    c_col = c.reshape(D_MODEL, 1)
    b4 = b_ada.reshape(depth, 3, 1, D_MODEL)
    return pl.pallas_call(
        _mod_kernel,
        grid=(depth, 3),
        in_specs=[pl.BlockSpec((D_MODEL, 1), lambda l, j: (0, 0)),
                  pl.BlockSpec((1, D_MODEL, D_MODEL), lambda l, j: (l, 0, j)),
                  pl.BlockSpec((1, 1, 1, D_MODEL), lambda l, j: (l, j, 0, 0))],
        out_specs=pl.BlockSpec((1, 1, 1, D_MODEL), lambda l, j: (l, j, 0, 0)),
        out_shape=jax.ShapeDtypeStruct((depth, 3, 1, D_MODEL), F32),
        compiler_params=_cparams("parallel", "parallel"),
        name="adaln_mod",
    )(c_col, w_ada, b4)


def _in_a_kernel(x_ref, g_ref, sc_ref, sh_ref, wa_ref, wg_ref, qkv0_ref, qkv1_ref, qkv2_ref, az_ref, mg_ref,
                 stage_sc, stage2_sc):
    h = _modulated_norm(x_ref[...], g_ref[...], sc_ref[...], sh_ref[...]).astype(BF16)
    a = jnp.dot(h, wa_ref[...], preferred_element_type=F32)
    tm = a.shape[0]
    gw = 3 * A_WIDTH
    qkv0_ref[0] = a[:, :gw].astype(BF16)
    nj = gw // LANES
    for g, out_ref in ((1, qkv1_ref), (2, qkv2_ref)):
        r = A_PATTERNS[g][1]
        for j in range(nj):
            stage_sc[j] = a[:, g * gw + j * LANES:g * gw + (j + 1) * LANES]
        if r % (PHASE_STEP * PHASE_STEP) == 0:
            r2, seg = r // PHASE_STEP, tm // PHASE_STEP
            for p1 in range(PHASE_STEP):
                for j in range(nj):
                    stage2_sc[j, p1 * seg:(p1 + 1) * seg, :] = stage_sc[j, pl.ds(p1, seg, stride=PHASE_STEP), :]
            for p1 in range(PHASE_STEP):
                for p2 in range(r2):
                    for j in range(nj):
                        out_ref[p1 + PHASE_STEP * p2, :, j * LANES:(j + 1) * LANES] = (
                            stage2_sc[j, pl.ds(p1 * seg + p2, tm // r, stride=r2), :].astype(BF16))
        else:
            for p in range(r):
                for j in range(nj):
                    out_ref[p, :, j * LANES:(j + 1) * LANES] = stage_sc[j, pl.ds(p, tm // r, stride=r), :].astype(BF16)
    az_ref[...] = a[:, A_GROUPS * gw:]
    mg_ref[...] = jnp.dot(h, wg_ref[...], preferred_element_type=F32)


def _in_proj_a(x, g, scale, shift, w_a, w_g):
    s = x.shape[0]
    tm = ROW_TILE
    gw = 3 * A_WIDTH
    row = lambda i: (i, 0)
    phase_specs = [pl.BlockSpec((r, tm // r, gw), lambda i: (0, i, 0)) for _, r in A_PATTERNS]
    phase_shapes = [jax.ShapeDtypeStruct((r, s // r, gw), BF16) for _, r in A_PATTERNS]
    return pl.pallas_call(
        _in_a_kernel,
        grid=(s // tm,),
        in_specs=[pl.BlockSpec((tm, D_MODEL), row),
                  _const_spec((1, D_MODEL)), _const_spec((1, D_MODEL)), _const_spec((1, D_MODEL)),
                  _const_spec(w_a.shape), _const_spec(w_g.shape)],
        out_specs=phase_specs + [pl.BlockSpec((tm, A_WIDTH), row),
                                 pl.BlockSpec((tm, 2 * D_MODEL), row)],
        out_shape=phase_shapes + [jax.ShapeDtypeStruct((s, A_WIDTH), F32),
                                  jax.ShapeDtypeStruct((s, 2 * D_MODEL), F32)],
        scratch_shapes=[pltpu.VMEM((gw // LANES, tm, LANES), F32)] * 2,
        compiler_params=_cparams("parallel"),
        name="in_proj_a",
    )(x, g, scale, shift, w_a, w_g)


def _in_m_kernel(x_ref, g_ref, sc_ref, sh_ref, wc_ref, wzt_ref, qg_ref, kvg_ref, wuqt_ref, wk_ref, wvt_ref,
                 cost_ref, sint_ref, ck_ref, sk_ref, qt_ref, k_ref, vt_ref, mzt_ref):
    h = _modulated_norm(x_ref[...], g_ref[...], sc_ref[...], sh_ref[...]).astype(BF16)
    mzt_ref[...] = _nt_dot(wzt_ref[...], h)
    c = jnp.dot(h, wc_ref[...], preferred_element_type=F32)
    cq = c[:, :M_Q_LORA]
    ckv = c[:, M_Q_LORA:M_Q_LORA + M_KV_LORA]
    kr_a = c[:, M_Q_LORA + M_KV_LORA:M_Q_LORA + M_KV_LORA + M_QK_PAD]
    kr_b = c[:, M_Q_LORA + M_KV_LORA + M_QK_PAD:]
    cqn = (cq * lax.rsqrt(jnp.mean(cq * cq, axis=-1, keepdims=True) + EPS) * qg_ref[...]).astype(BF16)
    ckvn = (ckv * lax.rsqrt(jnp.mean(ckv * ckv, axis=-1, keepdims=True) + EPS) * kvg_ref[...]).astype(BF16)

    tm = cq.shape[0]
    half = M_ROPE // 2
    qt = _nt_dot(wuqt_ref[...], cqn).reshape(M_HEADS, M_QK_PAD, tm)
    cos = cost_ref[...][None]
    sin = sint_ref[...][None]
    t1 = qt[:, M_NOPE:M_NOPE + half]
    t2 = qt[:, M_NOPE + half:M_QK]
    qt_ref[:, :M_NOPE] = qt[:, :M_NOPE].astype(BF16)
    qt_ref[:, M_NOPE:M_NOPE + half] = (t1 * cos - t2 * sin).astype(BF16)
    qt_ref[:, M_NOPE + half:M_QK] = (t1 * sin + t2 * cos).astype(BF16)
    qt_ref[:, M_QK:] = qt[:, M_QK:].astype(BF16)

    kr = kr_a * ck_ref[...] + kr_b * sk_ref[...]
    k_all = jnp.dot(ckvn, wk_ref[...], preferred_element_type=F32)
    for hd in range(M_HEADS):
        k_ref[hd] = (k_all[:, hd * M_QK_PAD:(hd + 1) * M_QK_PAD] + kr).astype(BF16)

    vt = _nt_dot(wvt_ref[...], ckvn)
    vt_ref[:, 0, :M_V] = vt.reshape(M_HEADS, M_V, tm).astype(BF16)
    vt_ref[:, 0, M_V:] = jnp.ones((M_HEADS, M_V_EXT - M_V, tm), BF16)


def _in_proj_m(x, g, scale, shift, w_c, w_zt, qg, kvg, w_uqt, w_k, w_vt, cos_t, sin_t, cos_k, sin_k):
    s = x.shape[0]
    tm = ATT_TILE
    row = lambda i: (i, 0)
    col = lambda i: (0, i)
    return pl.pallas_call(
        _in_m_kernel,
        grid=(s // tm,),
        in_specs=[pl.BlockSpec((tm, D_MODEL), row),
                  _const_spec((1, D_MODEL)), _const_spec((1, D_MODEL)), _const_spec((1, D_MODEL)),
                  _const_spec(w_c.shape), _const_spec(w_zt.shape),
                  _const_spec(qg.shape), _const_spec(kvg.shape),
                  _const_spec(w_uqt.shape), _const_spec(w_k.shape), _const_spec(w_vt.shape),
                  pl.BlockSpec((M_ROPE // 2, tm), col), pl.BlockSpec((M_ROPE // 2, tm), col),
                  pl.BlockSpec((tm, M_QK_PAD), row), pl.BlockSpec((tm, M_QK_PAD), row)],
        out_specs=[pl.BlockSpec((M_HEADS, M_QK_PAD, tm), lambda i: (0, 0, i)),
                   pl.BlockSpec((M_HEADS, tm, M_QK_PAD), lambda i: (0, i, 0)),
                   pl.BlockSpec((M_HEADS, 1, M_V_EXT, tm), lambda i: (0, i, 0, 0)),
                   pl.BlockSpec((M_WIDTH, tm), col)],
        out_shape=[jax.ShapeDtypeStruct((M_HEADS, M_QK_PAD, s), BF16),
                   jax.ShapeDtypeStruct((M_HEADS, s, M_QK_PAD), BF16),
                   jax.ShapeDtypeStruct((M_HEADS, s // tm, M_V_EXT, tm), BF16),
                   jax.ShapeDtypeStruct((M_WIDTH, s), F32)],
        compiler_params=_cparams("parallel"),
        name="in_proj_m",
    )(x, g, scale, shift, w_c, w_zt, qg, kvg, w_uqt, w_k, w_vt, cos_t, sin_t, cos_k, sin_k)


def _dilated_kernel(q_ref, kp_ref, kc_ref, vp_ref, vc_ref, bias0_ref, bias_ref, o_ref, lse_ref):
    pairs = A_WIDTH // LANES
    lo = lax.broadcasted_iota(jnp.int32, (1, LANES), 1) < A_HEAD_DIM
    ones = jnp.ones((2 * A_BLOCK, LANES), BF16)
    zero = jnp.zeros((), BF16)

    for b in range(DIL_BLOCKS):
        rows = slice(b * A_BLOCK, (b + 1) * A_BLOCK)
        b_ref = bias0_ref if b == 0 else bias_ref
        q_all = q_ref[0, rows] * jnp.asarray(A_HEAD_DIM ** -0.5, BF16)

        def keys(prev_ref, cur_ref, sl):
            if b == 0:
                return jnp.concatenate([prev_ref[0, :, sl], cur_ref[0, :A_BLOCK, sl]], axis=0)
            return cur_ref[0, (b - 1) * A_BLOCK:(b + 1) * A_BLOCK, sl]

        scores, v_ext = [], []
        for pr in range(pairs):
            sl = slice(pr * LANES, (pr + 1) * LANES)
            k = keys(kp_ref, kc_ref, sl)
            v_ext.append(jnp.concatenate([keys(vp_ref, vc_ref, sl), ones], axis=1))
            q = q_all[:, sl]
            for half in range(2):
                qh = jnp.where(lo if half == 0 else jnp.logical_not(lo), q, zero)
                scores.append(_nt_dot(qh, k) + b_ref[0, 2 * pr + half])

        probs, maxes = [], []
        for s in scores:
            m = jnp.max(s, axis=-1, keepdims=True)
            probs.append(jnp.exp(s - m).astype(BF16))
            maxes.append(m)

        for pr in range(pairs):
            sl = slice(pr * LANES, (pr + 1) * LANES)
            outs, lses = [], []
            for half in range(2):
                ol = jnp.dot(probs[2 * pr + half], v_ext[pr], preferred_element_type=F32)
                l = ol[:, LANES:]
                outs.append(ol[:, :LANES] / l)
                lses.append(maxes[2 * pr + half] + jnp.log(l))
            o_ref[0, rows, sl] = jnp.where(lo, outs[0], outs[1])
            lse_ref[0, rows, sl] = jnp.where(lo, lses[0], lses[1])


def _dilated_group(qkv, bias, g):
    r, l_sub, _ = qkv.shape
    rows = DIL_BLOCKS * A_BLOCK
    blk = (1, rows, A_WIDTH)
    bias_blk = (1,) + bias.shape[1:]

    def cur(t):
        return pl.BlockSpec(blk, lambda p, n: (p, n, t))

    def prev(t):
        return pl.BlockSpec((1, A_BLOCK, A_WIDTH), lambda p, n: (p, jnp.maximum(DIL_BLOCKS * n - 1, 0), t))

    out_spec = pl.BlockSpec(blk, lambda p, n: (p, n, 0))
    return pl.pallas_call(
        _dilated_kernel,
        grid=(r, l_sub // rows),
        in_specs=[cur(0), prev(1), cur(1), prev(2), cur(2),
                  pl.BlockSpec(bias_blk, lambda p, n: (jnp.where(n == 0, 1, 0), 0, 0, 0)),
                  pl.BlockSpec(bias_blk, lambda p, n: (0, 0, 0, 0))],
        out_specs=[out_spec, out_spec],
        out_shape=[jax.ShapeDtypeStruct((r, l_sub, A_WIDTH), F32)] * 2,
        compiler_params=_cparams("parallel", "parallel"),
        name=f"dilated_g{g}",
    )(qkv, qkv, qkv, qkv, qkv, bias, bias)


def _t5_bucket(dist):
    exact = REL_BUCKETS // 2
    d = jnp.maximum(dist, 1).astype(F32)
    large = exact + (jnp.log(d / exact) / math.log(REL_MAX_DIST / exact) * (REL_BUCKETS - exact)).astype(jnp.int32)
    large = jnp.minimum(large, REL_BUCKETS - 1)
    return jnp.where(dist < exact, dist, large)


def _dilated_bias(rel_bias, g, window, r):
    qi = jnp.arange(A_BLOCK)[:, None]
    ki = jnp.arange(2 * A_BLOCK)[None, :]
    j = qi + A_BLOCK - ki
    band = (j >= 0) & (j <= window // r)
    tab = rel_bias[:, g * A_HEADS:(g + 1) * A_HEADS].astype(F32)
    pick = jax.nn.one_hot(_t5_bucket(jnp.maximum(j, 0) * r), REL_BUCKETS, dtype=F32)
    b = jnp.einsum("qkb,bh->hqk", pick, tab, precision=lax.Precision.HIGHEST)
    b = jnp.where(band[None], b, NEG_INF)
    first = jnp.where(ki[None] < A_BLOCK, NEG_INF, b)
    return jnp.stack([b, first])


def _mla_kernel(nq, qt_ref, qn_ref, k_ref, vt_ref, mzt_ref, o_ref, s0_sc, s1_sc, mt0_sc, mt1_sc, m_sc, acc_sc):
    i = pl.program_id(1)
    tk, tq, cw = ATT_TILE, MLA_Q_TILE, MLA_CHUNK
    bufs = ((s0_sc, mt0_sc), (s1_sc, mt1_sc))
    m_sc[...] = jnp.full(m_sc.shape, NEG_INF, F32)
    acc_sc[...] = jnp.zeros(acc_sc.shape, F32)

    def scores(kt, diagonal=False, q_ref=qt_ref, col0=0):
        k = k_ref[0, pl.ds(pl.multiple_of(kt * tk, tk), tk), :]
        s = jnp.dot(k, q_ref[0, :, col0:], preferred_element_type=F32)
        if diagonal:
            kpos = lax.broadcasted_iota(jnp.int32, s.shape, 0)
            qpos = lax.broadcasted_iota(jnp.int32, s.shape, 1)
            s = jnp.where(kpos <= qpos, s, NEG_INF)
        return s

    def fill(slot, s, col0=0):
        s_ref, mt_ref = bufs[slot]
        s_ref[:, col0:] = s
        mt_ref[:, col0:] = jnp.max(s, axis=0, keepdims=True)

    def accumulate(kt, slot, col0=0):
        s_ref, mt_ref = bufs[slot]
        for c in range(col0 // cw, tq // cw):
            cols = slice(c * cw, (c + 1) * cw)
            m_prev = m_sc[:, cols]
            m_new = jnp.maximum(m_prev, mt_ref[:, cols])
            alpha = jnp.exp2(m_prev - m_new)
            p = jnp.exp2(s_ref[:, cols] - m_new).astype(BF16)
            acc_sc[:, cols] = alpha * acc_sc[:, cols] + jnp.dot(vt_ref[0, kt], p, preferred_element_type=F32)
            m_sc[:, cols] = m_new

    def stage(kt, slot, refill, col0=0):
        if refill == "ahead":
            s_next, fill0 = scores(kt + 2), 0
        elif refill == "diagonal":
            fill0 = slot * tk
            s_next = scores(kt + 2, True, col0=fill0)
        elif refill == "next":
            s_next, fill0 = scores(slot, q_ref=qn_ref), 0
        accumulate(kt, slot, col0)
        if refill is not None:
            fill(slot, s_next, fill0)

    def pair(pr, refill="ahead", diagonal=False):
        stage(2 * pr, 0, refill)
        stage(2 * pr + 1, 1, refill, col0=tk if diagonal else 0)

    last_refill = "next" if nq > 1 else None

    @pl.when(i == 0)
    def _():
        fill(0, scores(0, True))
        fill(1, scores(1, True, col0=tk), tk)
        pair(0, refill=last_refill, diagonal=True)

    @pl.when(i > 0)
    def _():
        full_pairs = i - 1

        def body(trip, carry):
            for u in range(MLA_UNROLL):
                pair(MLA_UNROLL * trip + u)
            return carry

        lax.fori_loop(0, full_pairs // MLA_UNROLL, body, 0)
        done = full_pairs - full_pairs % MLA_UNROLL
        width = MLA_UNROLL // 2
        while width >= 1:
            @pl.when((full_pairs % (2 * width)) >= width)
            def _(done=done, width=width):
                for u in range(width):
                    pair(done + u)

            done = done + jnp.where((full_pairs % (2 * width)) >= width, width, 0)
            width //= 2

        @pl.when(i < nq - 1)
        def _():
            pair(i - 1, refill="diagonal")
            pair(i, refill="next", diagonal=True)

        @pl.when(i == nq - 1)
        def _():
            pair(i - 1, refill="diagonal")
            pair(i, refill=None, diagonal=True)

    mz = mzt_ref[...]
    gate = mz * _sigmoid(mz)
    acc = acc_sc[...]
    o_ref[...] = (acc[:M_V] / acc[M_V:M_V + 1] * gate).astype(BF16)


def _mla_attention(qt, k, vt, mzt):
    s = k.shape[1]
    tk, tq = ATT_TILE, MLA_Q_TILE
    nq = s // tq
    return pl.pallas_call(
        functools.partial(_mla_kernel, nq),
        grid=(M_HEADS, nq),
        in_specs=[pl.BlockSpec((1, M_QK_PAD, tq), lambda h, i: (h, 0, i)),
                  pl.BlockSpec((1, M_QK_PAD, tq), lambda h, i: (h, 0, jnp.minimum(i + 1, nq - 1))),
                  pl.BlockSpec((1, s, M_QK_PAD), lambda h, i: (h, 0, 0)),
                  pl.BlockSpec((1, s // tk, M_V_EXT, tk), lambda h, i: (h, 0, 0, 0)),
                  pl.BlockSpec((M_V, tq), lambda h, i: (h, i))],
        out_specs=pl.BlockSpec((M_V, tq), lambda h, i: (h, i)),
        out_shape=jax.ShapeDtypeStruct((M_WIDTH, s), BF16),
        scratch_shapes=[pltpu.VMEM((tk, tq), F32), pltpu.VMEM((tk, tq), F32),
                        pltpu.VMEM((1, tq), F32), pltpu.VMEM((1, tq), F32),
                        pltpu.VMEM((1, tq), F32), pltpu.VMEM((M_V_EXT, tq), F32)],
        compiler_params=_cparams("arbitrary", "arbitrary"),
        name="mla_attention",
    )(qt, qt, k, vt, mzt)


def _out_kernel(final, x_ref, o0_ref, o1_ref, o2_ref, l0_ref, l1_ref, l2_ref, az_ref, ymt_ref, mg_ref, gate_ref,
                wa_ref, wb_ref, wo_ref, fg_ref, out_ref, o1_sc, o2_sc, l1_sc, l2_sc):
    tm = x_ref.shape[0]
    nj = A_WIDTH // LANES
    for src, dst in ((o1_ref, o1_sc), (o2_ref, o2_sc), (l1_ref, l1_sc), (l2_ref, l2_sc)):
        r = src.shape[0]
        for p in range(r):
            for j in range(nj):
                dst[j, pl.ds(p, tm // r, stride=r), :] = src[p, :, j * LANES:(j + 1) * LANES]

    def rows(sc):
        return jnp.concatenate([sc[j] for j in range(nj)], axis=1)

    l0, l1, l2 = l0_ref[0], rows(l1_sc), rows(l2_sc)
    mx = jnp.maximum(jnp.maximum(l0, l1), l2)
    e0, e1, e2 = jnp.exp(l0 - mx), jnp.exp(l1 - mx), jnp.exp(l2 - mx)
    mix = (o0_ref[0] * e0 + rows(o1_sc) * e1 + rows(o2_sc) * e2) / (e0 + e1 + e2)
    az = az_ref[...]
    y_a = (mix * (az * _sigmoid(az))).astype(BF16)
    t_a = jnp.dot(y_a, wa_ref[...], preferred_element_type=F32)
    t_b = lax.dot_general(ymt_ref[...], wb_ref[...], (((0,), (0,)), ((), ())), preferred_element_type=F32)
    mg = mg_ref[...]
    g_a = _sigmoid(mg[:, :D_MODEL])
    g_m = _sigmoid(mg[:, D_MODEL:])
    merged = (g_a * t_a + g_m * t_b).astype(BF16)
    y = x_ref[...] + gate_ref[...] * jnp.dot(merged, wo_ref[...], preferred_element_type=F32)
    if final:
        y = y * lax.rsqrt(jnp.mean(y * y, axis=-1, keepdims=True) + EPS) * fg_ref[...]
    out_ref[...] = y


def _out_proj(x, o_groups, lse_groups, a_z, ymt, mg, gate, w_a, w_b, w_o, final_g, final):
    s = x.shape[0]
    tm = ROW_TILE
    row = lambda i: (i, 0)
    phase_specs = [pl.BlockSpec((r, tm // r, A_WIDTH), lambda i: (0, i, 0)) for _, r in A_PATTERNS]
    return pl.pallas_call(
        functools.partial(_out_kernel, final),
        grid=(s // tm,),
        in_specs=[pl.BlockSpec((tm, D_MODEL), row)] + phase_specs * 2
                 + [pl.BlockSpec((tm, A_WIDTH), row),
                    pl.BlockSpec((M_WIDTH, tm), lambda i: (0, i)),
                    pl.BlockSpec((tm, 2 * D_MODEL), row),
                    _const_spec((1, D_MODEL)),
                    _const_spec(w_a.shape), _const_spec(w_b.shape), _const_spec(w_o.shape),
                    _const_spec((1, D_MODEL))],
        out_specs=pl.BlockSpec((tm, D_MODEL), row),
        out_shape=jax.ShapeDtypeStruct((s, D_MODEL), F32),
        scratch_shapes=[pltpu.VMEM((A_WIDTH // LANES, tm, LANES), F32)] * 4,
        compiler_params=_cparams("parallel"),
        name="out_proj",
    )(x, *o_groups, *lse_groups, a_z, ymt, mg, gate, w_a, w_b, w_o, final_g)


def _prep_layer(w_in, w_uq, w_ukv, w_out_a, w_out_b, w_o):
    o_az = A_QKV
    o_cq = o_az + A_WIDTH
    o_ckv = o_cq + M_Q_LORA
    o_kr = o_ckv + M_KV_LORA
    o_mz = o_kr + M_ROPE
    o_mg = o_mz + M_WIDTH
    half = M_ROPE // 2
    w_qkv = w_in[:, :A_QKV].reshape(D_MODEL, 3, A_GROUPS, A_WIDTH).transpose(0, 2, 1, 3).reshape(D_MODEL, A_QKV)
    w_a = jnp.concatenate([w_qkv, w_in[:, o_az:o_cq]], axis=1).astype(BF16)
    w_g = w_in[:, o_mg:].astype(BF16)
    w_kr = w_in[:, o_kr:o_mz]
    z_lo = jnp.zeros((D_MODEL, M_NOPE), F32)
    z_hi = jnp.zeros((D_MODEL, M_QK_PAD - M_QK), F32)
    kr_a = jnp.concatenate([z_lo, w_kr, z_hi], axis=1)
    kr_b = jnp.concatenate([z_lo, w_kr[:, half:], w_kr[:, :half], z_hi], axis=1)
    w_c = jnp.concatenate([w_in[:, o_cq:o_kr], kr_a, kr_b], axis=1).astype(BF16)
    w_zt = w_in[:, o_mz:o_mg].T.astype(BF16)
    scale = M_QK ** -0.5 * math.log2(math.e)
    uq = (w_uq * scale).reshape(M_Q_LORA, M_HEADS, M_QK)
    uq = jnp.pad(uq, ((0, 0), (0, 0), (0, M_QK_PAD - M_QK)))
    w_uqt = uq.reshape(M_Q_LORA, M_HEADS * M_QK_PAD).T.astype(BF16)
    ukv = w_ukv.reshape(M_KV_LORA, M_HEADS, M_NOPE + M_V)
    w_k = jnp.pad(ukv[:, :, :M_NOPE], ((0, 0), (0, 0), (0, M_QK_PAD - M_NOPE)))
    w_k = w_k.reshape(M_KV_LORA, M_HEADS * M_QK_PAD).astype(BF16)
    w_vt = ukv[:, :, M_NOPE:].reshape(M_KV_LORA, M_WIDTH).T.astype(BF16)
    return (w_a, w_g, w_c, w_zt, w_uqt, w_k, w_vt,
            w_out_a.astype(BF16), w_out_b.astype(BF16), w_o.astype(BF16))


def _rope_tables(positions):
    half = M_ROPE // 2
    inv_freq = 1.0 / (ROPE_THETA ** (jnp.arange(0, M_ROPE, 2, dtype=F32) / M_ROPE))
    ang = positions.astype(F32)[:, None] * inv_freq
    cos, sin = jnp.cos(ang), jnp.sin(ang)
    s = positions.shape[0]
    z_lo = jnp.zeros((s, M_NOPE), F32)
    z_hi = jnp.zeros((s, M_QK_PAD - M_QK), F32)
    cos_k = jnp.concatenate([z_lo, cos, cos, z_hi], axis=1)
    sin_k = jnp.concatenate([z_lo, -sin, sin, z_hi], axis=1)
    return cos.T, sin.T, cos_k, sin_k


def kernel(x, c, positions, w_ada, b_ada, norm_g, w_in, q_norm_g, w_uq, kv_norm_g, w_ukv, w_out_a, w_out_b, w_o,
           rel_bias, final_norm_g):
    batch, s, _ = x.shape
    assert batch == 1 and s % (DIL_BLOCKS * A_BLOCK * A_PATTERNS[-1][1]) == 0
    assert s % MLA_Q_TILE == 0 and s % ROW_TILE == 0
    depth = w_ada.shape[0]
    xs = x[0]
    mod = _modulation(c, w_ada, b_ada)
    cos_t, sin_t, cos_k, sin_k = _rope_tables(positions[0])
    biases = [_dilated_bias(rel_bias, g, w, r) for g, (w, r) in enumerate(A_PATTERNS)]
    final_g = final_norm_g.reshape(1, D_MODEL)
    for l in range(depth):
        (w_a, w_g, w_c, w_zt, w_uqt, w_k, w_vt, wo_a, wo_b, wo) = _prep_layer(
            w_in[l], w_uq[l], w_ukv[l], w_out_a[l], w_out_b[l], w_o[l])
        shift, scale, gate = mod[l, 0], mod[l, 1], mod[l, 2]
        g = norm_g[l].reshape(1, D_MODEL)
        qkv0, qkv1, qkv2, a_z, mg = _in_proj_a(xs, g, scale, shift, w_a, w_g)
        qt, k, vt, mzt = _in_proj_m(xs, g, scale, shift, w_c, w_zt,
                                    q_norm_g[l].reshape(1, M_Q_LORA), kv_norm_g[l].reshape(1, M_KV_LORA),
                                    w_uqt, w_k, w_vt, cos_t, sin_t, cos_k, sin_k)
        outs = [_dilated_group(qkv, biases[gi], gi) for gi, qkv in enumerate((qkv0, qkv1, qkv2))]
        ymt = _mla_attention(qt, k, vt, mzt)
        xs = _out_proj(xs, [o for o, _ in outs], [ls for _, ls in outs], a_z, ymt, mg, gate,
                       wo_a, wo_b, wo, final_g, l == depth - 1)
    return xs[None]
```

```python
import functools
import math

import jax
import jax.numpy as jnp
from jax import lax
from jax.experimental import pallas as pl
from jax.experimental.pallas import tpu as pltpu

F32 = jnp.float32
BF16 = jnp.bfloat16

D_MODEL = 1024
A_PATTERNS = ((128, 1), (512, 4), (2048, 16))
A_GROUPS = 3
A_HEADS = 8
A_HEAD_DIM = 64
A_WIDTH = A_HEADS * A_HEAD_DIM
A_BLOCK = 128
A_QKV = 3 * A_GROUPS * A_WIDTH
M_HEADS = 16
M_Q_LORA = 256
M_KV_LORA = 128
M_NOPE = 64
M_ROPE = 32
M_V = 64
M_QK = M_NOPE + M_ROPE
M_QK_PAD = 128
M_WIDTH = M_HEADS * M_V
M_V_EXT = M_V + 16
ROPE_THETA = 10000.0
REL_BUCKETS = 32
REL_MAX_DIST = 2048
EPS = 1e-6
NEG_INF = -1e30
LANES = 128
PHASE_STEP = 4

ROW_TILE = 512
ATT_TILE = 512
MLA_Q_TILE = 2 * ATT_TILE
MLA_CHUNK = 256
MLA_UNROLL = 4
DIL_BLOCKS = 8
VMEM_LIMIT = 56 * 1024 * 1024


def _cparams(*sem, **extra):
    return pltpu.CompilerParams(dimension_semantics=sem, vmem_limit_bytes=VMEM_LIMIT, **extra)


def _const_spec(shape):
    nd = len(shape)
    return pl.BlockSpec(shape, lambda *_: (0,) * nd, pipeline_mode=pl.Buffered(1))


def _nt_dot(a, b):
    return lax.dot_general(a, b, (((1,), (1,)), ((), ())), preferred_element_type=F32)


def _sigmoid(x):
    return 0.5 * jnp.tanh(0.5 * x) + 0.5


def _modulated_norm(x, g, scale, shift):
    y = x * lax.rsqrt(jnp.mean(x * x, axis=-1, keepdims=True) + EPS)
    return (y * g) * (1.0 + scale) + shift


def _mod_kernel(c_ref, w_ref, b_ref, o_ref):
    c = c_ref[...]
    c_act = c * _sigmoid(c)
    o_ref[0, 0] = jnp.sum(c_act * w_ref[0], axis=0, keepdims=True) + b_ref[0, 0]


def _modulation(c, w_ada, b_ada):
    depth = w_ada.shape[0]
    c_col = c.reshape(D_MODEL, 1)
    b4 = b_ada.reshape(depth, 3, 1, D_MODEL)
    return pl.pallas_call(
        _mod_kernel,
        grid=(depth, 3),
        in_specs=[pl.BlockSpec((D_MODEL, 1), lambda l, j: (0, 0)),
                  pl.BlockSpec((1, D_MODEL, D_MODEL), lambda l, j: (l, 0, j)),
                  pl.BlockSpec((1, 1, 1, D_MODEL), lambda l, j: (l, j, 0, 0))],
        out_specs=pl.BlockSpec((1, 1, 1, D_MODEL), lambda l, j: (l, j, 0, 0)),
        out_shape=jax.ShapeDtypeStruct((depth, 3, 1, D_MODEL), F32),
        compiler_params=_cparams("parallel", "parallel"),
        name="adaln_mod",
    )(c_col, w_ada, b4)


def _in_a_kernel(x_ref, g_ref, sc_ref, sh_ref, wa_ref, wg_ref, qkv0_ref, qkv1_ref, qkv2_ref, az_ref, mg_ref,
                 stage_sc, stage2_sc):
    h = _modulated_norm(x_ref[...], g_ref[...], sc_ref[...], sh_ref[...]).astype(BF16)
    a = jnp.dot(h, wa_ref[...], preferred_element_type=F32)
    tm = a.shape[0]
    gw = 3 * A_WIDTH
    qkv0_ref[0] = a[:, :gw].astype(BF16)
    nj = gw // LANES
    for g, out_ref in ((1, qkv1_ref), (2, qkv2_ref)):
        r = A_PATTERNS[g][1]
        for j in range(nj):
            stage_sc[j] = a[:, g * gw + j * LANES:g * gw + (j + 1) * LANES]
        if r % (PHASE_STEP * PHASE_STEP) == 0:
            r2, seg = r // PHASE_STEP, tm // PHASE_STEP
            for p1 in range(PHASE_STEP):
                for j in range(nj):
                    stage2_sc[j, p1 * seg:(p1 + 1) * seg, :] = stage_sc[j, pl.ds(p1, seg, stride=PHASE_STEP), :]
            for p1 in range(PHASE_STEP):
                for p2 in range(r2):
                    for j in range(nj):
                        out_ref[p1 + PHASE_STEP * p2, :, j * LANES:(j + 1) * LANES] = (
                            stage2_sc[j, pl.ds(p1 * seg + p2, tm // r, stride=r2), :].astype(BF16))
        else:
            for p in range(r):
                for j in range(nj):
                    out_ref[p, :, j * LANES:(j + 1) * LANES] = stage_sc[j, pl.ds(p, tm // r, stride=r), :].astype(BF16)
    az_ref[...] = a[:, A_GROUPS * gw:]
    mg_ref[...] = jnp.dot(h, wg_ref[...], preferred_element_type=F32)


def _in_proj_a(x, g, scale, shift, w_a, w_g):
    s = x.shape[0]
    tm = ROW_TILE
    gw = 3 * A_WIDTH
    row = lambda i: (i, 0)
    phase_specs = [pl.BlockSpec((r, tm // r, gw), lambda i: (0, i, 0)) for _, r in A_PATTERNS]
    phase_shapes = [jax.ShapeDtypeStruct((r, s // r, gw), BF16) for _, r in A_PATTERNS]
    return pl.pallas_call(
        _in_a_kernel,
        grid=(s // tm,),
        in_specs=[pl.BlockSpec((tm, D_MODEL), row),
                  _const_spec((1, D_MODEL)), _const_spec((1, D_MODEL)), _const_spec((1, D_MODEL)),
                  _const_spec(w_a.shape), _const_spec(w_g.shape)],
        out_specs=phase_specs + [pl.BlockSpec((tm, A_WIDTH), row),
                                 pl.BlockSpec((tm, 2 * D_MODEL), row)],
        out_shape=phase_shapes + [jax.ShapeDtypeStruct((s, A_WIDTH), F32),
                                  jax.ShapeDtypeStruct((s, 2 * D_MODEL), F32)],
        scratch_shapes=[pltpu.VMEM((gw // LANES, tm, LANES), F32)] * 2,
        compiler_params=_cparams("parallel"),
        name="in_proj_a",
    )(x, g, scale, shift, w_a, w_g)


def _in_m_kernel(x_ref, g_ref, sc_ref, sh_ref, wc_ref, wzt_ref, qg_ref, kvg_ref, wuqt_ref, wk_ref, wvt_ref,
                 cost_ref, sint_ref, ck_ref, sk_ref, qt_ref, k_ref, vt_ref, mzt_ref):
    h = _modulated_norm(x_ref[...], g_ref[...], sc_ref[...], sh_ref[...]).astype(BF16)
    mzt_ref[...] = _nt_dot(wzt_ref[...], h)
    c = jnp.dot(h, wc_ref[...], preferred_element_type=F32)
    cq = c[:, :M_Q_LORA]
    ckv = c[:, M_Q_LORA:M_Q_LORA + M_KV_LORA]
    kr_a = c[:, M_Q_LORA + M_KV_LORA:M_Q_LORA + M_KV_LORA + M_QK_PAD]
    kr_b = c[:, M_Q_LORA + M_KV_LORA + M_QK_PAD:]
    cqn = (cq * lax.rsqrt(jnp.mean(cq * cq, axis=-1, keepdims=True) + EPS) * qg_ref[...]).astype(BF16)
    ckvn = (ckv * lax.rsqrt(jnp.mean(ckv * ckv, axis=-1, keepdims=True) + EPS) * kvg_ref[...]).astype(BF16)

    tm = cq.shape[0]
    half = M_ROPE // 2
    qt = _nt_dot(wuqt_ref[...], cqn).reshape(M_HEADS, M_QK_PAD, tm)
    cos = cost_ref[...][None]
    sin = sint_ref[...][None]
    t1 = qt[:, M_NOPE:M_NOPE + half]
    t2 = qt[:, M_NOPE + half:M_QK]
    qt_ref[:, :M_NOPE] = qt[:, :M_NOPE].astype(BF16)
    qt_ref[:, M_NOPE:M_NOPE + half] = (t1 * cos - t2 * sin).astype(BF16)
    qt_ref[:, M_NOPE + half:M_QK] = (t1 * sin + t2 * cos).astype(BF16)
    qt_ref[:, M_QK:] = qt[:, M_QK:].astype(BF16)

    kr = kr_a * ck_ref[...] + kr_b * sk_ref[...]
    k_all = jnp.dot(ckvn, wk_ref[...], preferred_element_type=F32)
    for hd in range(M_HEADS):
        k_ref[hd] = (k_all[:, hd * M_QK_PAD:(hd + 1) * M_QK_PAD] + kr).astype(BF16)

    vt = _nt_dot(wvt_ref[...], ckvn)
    vt_ref[:, 0, :M_V] = vt.reshape(M_HEADS, M_V, tm).astype(BF16)
    vt_ref[:, 0, M_V:] = jnp.ones((M_HEADS, M_V_EXT - M_V, tm), BF16)


def _in_proj_m(x, g, scale, shift, w_c, w_zt, qg, kvg, w_uqt, w_k, w_vt, cos_t, sin_t, cos_k, sin_k):
    s = x.shape[0]
    tm = ATT_TILE
    row = lambda i: (i, 0)
    col = lambda i: (0, i)
    return pl.pallas_call(
        _in_m_kernel,
        grid=(s // tm,),
        in_specs=[pl.BlockSpec((tm, D_MODEL), row),
                  _const_spec((1, D_MODEL)), _const_spec((1, D_MODEL)), _const_spec((1, D_MODEL)),
                  _const_spec(w_c.shape), _const_spec(w_zt.shape),
                  _const_spec(qg.shape), _const_spec(kvg.shape),
                  _const_spec(w_uqt.shape), _const_spec(w_k.shape), _const_spec(w_vt.shape),
                  pl.BlockSpec((M_ROPE // 2, tm), col), pl.BlockSpec((M_ROPE // 2, tm), col),
                  pl.BlockSpec((tm, M_QK_PAD), row), pl.BlockSpec((tm, M_QK_PAD), row)],
        out_specs=[pl.BlockSpec((M_HEADS, M_QK_PAD, tm), lambda i: (0, 0, i)),
                   pl.BlockSpec((M_HEADS, tm, M_QK_PAD), lambda i: (0, i, 0)),
                   pl.BlockSpec((M_HEADS, 1, M_V_EXT, tm), lambda i: (0, i, 0, 0)),
                   pl.BlockSpec((M_WIDTH, tm), col)],
        out_shape=[jax.ShapeDtypeStruct((M_HEADS, M_QK_PAD, s), BF16),
                   jax.ShapeDtypeStruct((M_HEADS, s, M_QK_PAD), BF16),
                   jax.ShapeDtypeStruct((M_HEADS, s // tm, M_V_EXT, tm), BF16),
                   jax.ShapeDtypeStruct((M_WIDTH, s), F32)],
        compiler_params=_cparams("parallel"),
        name="in_proj_m",
    )(x, g, scale, shift, w_c, w_zt, qg, kvg, w_uqt, w_k, w_vt, cos_t, sin_t, cos_k, sin_k)


def _dilated_kernel(q_ref, kp_ref, kc_ref, vp_ref, vc_ref, bias0_ref, bias_ref, o_ref, lse_ref):
    pairs = A_WIDTH // LANES
    lo = lax.broadcasted_iota(jnp.int32, (1, LANES), 1) < A_HEAD_DIM
    ones = jnp.ones((2 * A_BLOCK, LANES), BF16)
    zero = jnp.zeros((), BF16)

    for b in range(DIL_BLOCKS):
        rows = slice(b * A_BLOCK, (b + 1) * A_BLOCK)
        b_ref = bias0_ref if b == 0 else bias_ref
        q_all = q_ref[0, rows] * jnp.asarray(A_HEAD_DIM ** -0.5, BF16)

        def keys(prev_ref, cur_ref, sl):
            if b == 0:
                return jnp.concatenate([prev_ref[0, :, sl], cur_ref[0, :A_BLOCK, sl]], axis=0)
            return cur_ref[0, (b - 1) * A_BLOCK:(b + 1) * A_BLOCK, sl]

        scores, v_ext = [], []
        for pr in range(pairs):
            sl = slice(pr * LANES, (pr + 1) * LANES)
            k = keys(kp_ref, kc_ref, sl)
            v_ext.append(jnp.concatenate([keys(vp_ref, vc_ref, sl), ones], axis=1))
            q = q_all[:, sl]
            for half in range(2):
                qh = jnp.where(lo if half == 0 else jnp.logical_not(lo), q, zero)
                scores.append(_nt_dot(qh, k) + b_ref[0, 2 * pr + half])

        probs, maxes = [], []
        for s in scores:
            m = jnp.max(s, axis=-1, keepdims=True)
            probs.append(jnp.exp(s - m).astype(BF16))
            maxes.append(m)

        for pr in range(pairs):
            sl = slice(pr * LANES, (pr + 1) * LANES)
            outs, lses = [], []
            for half in range(2):
                ol = jnp.dot(probs[2 * pr + half], v_ext[pr], preferred_element_type=F32)
                l = ol[:, LANES:]
                outs.append(ol[:, :LANES] / l)
                lses.append(maxes[2 * pr + half] + jnp.log(l))
            o_ref[0, rows, sl] = jnp.where(lo, outs[0], outs[1])
            lse_ref[0, rows, sl] = jnp.where(lo, lses[0], lses[1])


def _dilated_group(qkv, bias, g):
    r, l_sub, _ = qkv.shape
    rows = DIL_BLOCKS * A_BLOCK
    blk = (1, rows, A_WIDTH)
    bias_blk = (1,) + bias.shape[1:]

    def cur(t):
        return pl.BlockSpec(blk, lambda p, n: (p, n, t))

    def prev(t):
        return pl.BlockSpec((1, A_BLOCK, A_WIDTH), lambda p, n: (p, jnp.maximum(DIL_BLOCKS * n - 1, 0), t))

    out_spec = pl.BlockSpec(blk, lambda p, n: (p, n, 0))
    return pl.pallas_call(
        _dilated_kernel,
        grid=(r, l_sub // rows),
        in_specs=[cur(0), prev(1), cur(1), prev(2), cur(2),
                  pl.BlockSpec(bias_blk, lambda p, n: (jnp.where(n == 0, 1, 0), 0, 0, 0)),
                  pl.BlockSpec(bias_blk, lambda p, n: (0, 0, 0, 0))],
        out_specs=[out_spec, out_spec],
        out_shape=[jax.ShapeDtypeStruct((r, l_sub, A_WIDTH), F32)] * 2,
        compiler_params=_cparams("parallel", "parallel"),
        name=f"dilated_g{g}",
    )(qkv, qkv, qkv, qkv, qkv, bias, bias)


def _t5_bucket(dist):
    exact = REL_BUCKETS // 2
    d = jnp.maximum(dist, 1).astype(F32)
    large = exact + (jnp.log(d / exact) / math.log(REL_MAX_DIST / exact) * (REL_BUCKETS - exact)).astype(jnp.int32)
    large = jnp.minimum(large, REL_BUCKETS - 1)
    return jnp.where(dist < exact, dist, large)


def _dilated_bias(rel_bias, g, window, r):
    qi = jnp.arange(A_BLOCK)[:, None]
    ki = jnp.arange(2 * A_BLOCK)[None, :]
    j = qi + A_BLOCK - ki
    band = (j >= 0) & (j <= window // r)
    tab = rel_bias[:, g * A_HEADS:(g + 1) * A_HEADS].astype(F32)
    pick = jax.nn.one_hot(_t5_bucket(jnp.maximum(j, 0) * r), REL_BUCKETS, dtype=F32)
    b = jnp.einsum("qkb,bh->hqk", pick, tab, precision=lax.Precision.HIGHEST)
    b = jnp.where(band[None], b, NEG_INF)
    first = jnp.where(ki[None] < A_BLOCK, NEG_INF, b)
    return jnp.stack([b, first])


def _mla_kernel(nq, qt_ref, qn_ref, k_ref, vt_ref, mzt_ref, o_ref, s0_sc, s1_sc, mt0_sc, mt1_sc, m_sc, acc_sc):
    i = pl.program_id(1)
    tk, tq, cw = ATT_TILE, MLA_Q_TILE, MLA_CHUNK
    bufs = ((s0_sc, mt0_sc), (s1_sc, mt1_sc))
    m_sc[...] = jnp.full(m_sc.shape, NEG_INF, F32)
    acc_sc[...] = jnp.zeros(acc_sc.shape, F32)

    def scores(kt, diagonal=False, q_ref=qt_ref, col0=0):
        k = k_ref[0, pl.ds(pl.multiple_of(kt * tk, tk), tk), :]
        s = jnp.dot(k, q_ref[0, :, col0:], preferred_element_type=F32)
        if diagonal:
            kpos = lax.broadcasted_iota(jnp.int32, s.shape, 0)
            qpos = lax.broadcasted_iota(jnp.int32, s.shape, 1)
            s = jnp.where(kpos <= qpos, s, NEG_INF)
        return s

    def fill(slot, s, col0=0):
        s_ref, mt_ref = bufs[slot]
        s_ref[:, col0:] = s
        mt_ref[:, col0:] = jnp.max(s, axis=0, keepdims=True)

    def accumulate_chunk(kt, slot, c):
        s_ref, mt_ref = bufs[slot]
        cols = slice(c * cw, (c + 1) * cw)
        m_prev = m_sc[:, cols]
        m_new = jnp.maximum(m_prev, mt_ref[:, cols])
        alpha = jnp.exp2(m_prev - m_new)
        p = jnp.exp2(s_ref[:, cols] - m_new).astype(BF16)
        acc_sc[:, cols] = alpha * acc_sc[:, cols] + jnp.dot(vt_ref[0, kt], p, preferred_element_type=F32)
        m_sc[:, cols] = m_new

    def stage(kt, slot, refill, col0=0):
        s_ref, mt_ref = bufs[slot]
        q_ref, fill0, k_tile = qt_ref, 0, kt + 2
        if refill == "diagonal":
            fill0 = slot * tk
        elif refill == "next":
            q_ref, k_tile = qn_ref, slot
        if refill is not None:
            k = k_ref[0, pl.ds(pl.multiple_of(k_tile * tk, tk), tk), :]
        parts = {}
        for c in range(tq // cw):
            if refill is not None and c * cw >= fill0:
                s = jnp.dot(k, q_ref[0, :, c * cw:(c + 1) * cw], preferred_element_type=F32)
                first_q = c * cw - fill0
                if refill == "diagonal" and first_q < tk - 1:
                    kpos = lax.broadcasted_iota(jnp.int32, s.shape, 0)
                    qpos = lax.broadcasted_iota(jnp.int32, s.shape, 1) + first_q
                    s = jnp.where(kpos <= qpos, s, NEG_INF)
                parts[c] = s
            if c * cw >= col0:
                accumulate_chunk(kt, slot, c)
        for c, s in parts.items():
            cols = slice(c * cw, (c + 1) * cw)
            s_ref[:, cols] = s
            mt_ref[:, cols] = jnp.max(s, axis=0, keepdims=True)

    def pair(pr, refill="ahead", diagonal=False):
        stage(2 * pr, 0, refill)
        stage(2 * pr + 1, 1, refill, col0=tk if diagonal else 0)

    last_refill = "next" if nq > 1 else None

    @pl.when(i == 0)
    def _():
        fill(0, scores(0, True))
        fill(1, scores(1, True, col0=tk), tk)
        pair(0, refill=last_refill, diagonal=True)

    @pl.when(i > 0)
    def _():
        full_pairs = i - 1

        def body(trip, carry):
            for u in range(MLA_UNROLL):
                pair(MLA_UNROLL * trip + u)
            return carry

        lax.fori_loop(0, full_pairs // MLA_UNROLL, body, 0)
        done = full_pairs - full_pairs % MLA_UNROLL
        width = MLA_UNROLL // 2
        while width >= 1:
            @pl.when((full_pairs % (2 * width)) >= width)
            def _(done=done, width=width):
                for u in range(width):
                    pair(done + u)

            done = done + jnp.where((full_pairs % (2 * width)) >= width, width, 0)
            width //= 2

        @pl.when(i < nq - 1)
        def _():
            pair(i - 1, refill="diagonal")
            pair(i, refill="next", diagonal=True)

        @pl.when(i == nq - 1)
        def _():
            pair(i - 1, refill="diagonal")
            pair(i, refill=None, diagonal=True)

    mz = mzt_ref[...]
    gate = mz * _sigmoid(mz)
    acc = acc_sc[...]
    o_ref[...] = (acc[:M_V] / acc[M_V:M_V + 1] * gate).astype(BF16)


def _mla_attention(qt, k, vt, mzt):
    s = k.shape[1]
    tk, tq = ATT_TILE, MLA_Q_TILE
    nq = s // tq
    return pl.pallas_call(
        functools.partial(_mla_kernel, nq),
        grid=(M_HEADS, nq),
        in_specs=[pl.BlockSpec((1, M_QK_PAD, tq), lambda h, i: (h, 0, i)),
                  pl.BlockSpec((1, M_QK_PAD, tq), lambda h, i: (h, 0, jnp.minimum(i + 1, nq - 1))),
                  pl.BlockSpec((1, s, M_QK_PAD), lambda h, i: (h, 0, 0)),
                  pl.BlockSpec((1, s // tk, M_V_EXT, tk), lambda h, i: (h, 0, 0, 0)),
                  pl.BlockSpec((M_V, tq), lambda h, i: (h, i))],
        out_specs=pl.BlockSpec((M_V, tq), lambda h, i: (h, i)),
        out_shape=jax.ShapeDtypeStruct((M_WIDTH, s), BF16),
        scratch_shapes=[pltpu.VMEM((tk, tq), F32), pltpu.VMEM((tk, tq), F32),
                        pltpu.VMEM((1, tq), F32), pltpu.VMEM((1, tq), F32),
                        pltpu.VMEM((1, tq), F32), pltpu.VMEM((M_V_EXT, tq), F32)],
        compiler_params=_cparams("arbitrary", "arbitrary"),
        name="mla_attention",
    )(qt, qt, k, vt, mzt)


def _out_kernel(final, x_ref, o0_ref, o1_ref, o2_ref, l0_ref, l1_ref, l2_ref, az_ref, ymt_ref, mg_ref, gate_ref,
                wa_ref, wb_ref, wo_ref, fg_ref, out_ref, o1_sc, o2_sc, l1_sc, l2_sc):
    tm = x_ref.shape[0]
    nj = A_WIDTH // LANES
    for src, dst in ((o1_ref, o1_sc), (o2_ref, o2_sc), (l1_ref, l1_sc), (l2_ref, l2_sc)):
        r = src.shape[0]
        for p in range(r):
            for j in range(nj):
                dst[j, pl.ds(p, tm // r, stride=r), :] = src[p, :, j * LANES:(j + 1) * LANES]

    def rows(sc):
        return jnp.concatenate([sc[j] for j in range(nj)], axis=1)

    l0, l1, l2 = l0_ref[0], rows(l1_sc), rows(l2_sc)
    mx = jnp.maximum(jnp.maximum(l0, l1), l2)
    e0, e1, e2 = jnp.exp(l0 - mx), jnp.exp(l1 - mx), jnp.exp(l2 - mx)
    mix = (o0_ref[0] * e0 + rows(o1_sc) * e1 + rows(o2_sc) * e2) / (e0 + e1 + e2)
    az = az_ref[...]
    y_a = (mix * (az * _sigmoid(az))).astype(BF16)
    t_a = jnp.dot(y_a, wa_ref[...], preferred_element_type=F32)
    t_b = lax.dot_general(ymt_ref[...], wb_ref[...], (((0,), (0,)), ((), ())), preferred_element_type=F32)
    mg = mg_ref[...]
    g_a = _sigmoid(mg[:, :D_MODEL])
    g_m = _sigmoid(mg[:, D_MODEL:])
    merged = (g_a * t_a + g_m * t_b).astype(BF16)
    y = x_ref[...] + gate_ref[...] * jnp.dot(merged, wo_ref[...], preferred_element_type=F32)
    if final:
        y = y * lax.rsqrt(jnp.mean(y * y, axis=-1, keepdims=True) + EPS) * fg_ref[...]
    out_ref[...] = y


def _out_proj(x, o_groups, lse_groups, a_z, ymt, mg, gate, w_a, w_b, w_o, final_g, final):
    s = x.shape[0]
    tm = ROW_TILE
    row = lambda i: (i, 0)
    phase_specs = [pl.BlockSpec((r, tm // r, A_WIDTH), lambda i: (0, i, 0)) for _, r in A_PATTERNS]
    return pl.pallas_call(
        functools.partial(_out_kernel, final),
        grid=(s // tm,),
        in_specs=[pl.BlockSpec((tm, D_MODEL), row)] + phase_specs * 2
                 + [pl.BlockSpec((tm, A_WIDTH), row),
                    pl.BlockSpec((M_WIDTH, tm), lambda i: (0, i)),
                    pl.BlockSpec((tm, 2 * D_MODEL), row),
                    _const_spec((1, D_MODEL)),
                    _const_spec(w_a.shape), _const_spec(w_b.shape), _const_spec(w_o.shape),
                    _const_spec((1, D_MODEL))],
        out_specs=pl.BlockSpec((tm, D_MODEL), row),
        out_shape=jax.ShapeDtypeStruct((s, D_MODEL), F32),
        scratch_shapes=[pltpu.VMEM((A_WIDTH // LANES, tm, LANES), F32)] * 4,
        compiler_params=_cparams("parallel"),
        name="out_proj",
    )(x, *o_groups, *lse_groups, a_z, ymt, mg, gate, w_a, w_b, w_o, final_g)


def _prep_layer(w_in, w_uq, w_ukv, w_out_a, w_out_b, w_o):
    o_az = A_QKV
    o_cq = o_az + A_WIDTH
    o_ckv = o_cq + M_Q_LORA
    o_kr = o_ckv + M_KV_LORA
    o_mz = o_kr + M_ROPE
    o_mg = o_mz + M_WIDTH
    half = M_ROPE // 2
    w_qkv = w_in[:, :A_QKV].reshape(D_MODEL, 3, A_GROUPS, A_WIDTH).transpose(0, 2, 1, 3).reshape(D_MODEL, A_QKV)
    w_a = jnp.concatenate([w_qkv, w_in[:, o_az:o_cq]], axis=1).astype(BF16)
    w_g = w_in[:, o_mg:].astype(BF16)
    w_kr = w_in[:, o_kr:o_mz]
    z_lo = jnp.zeros((D_MODEL, M_NOPE), F32)
    z_hi = jnp.zeros((D_MODEL, M_QK_PAD - M_QK), F32)
    kr_a = jnp.concatenate([z_lo, w_kr, z_hi], axis=1)
    kr_b = jnp.concatenate([z_lo, w_kr[:, half:], w_kr[:, :half], z_hi], axis=1)
    w_c = jnp.concatenate([w_in[:, o_cq:o_kr], kr_a, kr_b], axis=1).astype(BF16)
    w_zt = w_in[:, o_mz:o_mg].T.astype(BF16)
    scale = M_QK ** -0.5 * math.log2(math.e)
    uq = (w_uq * scale).reshape(M_Q_LORA, M_HEADS, M_QK)
    uq = jnp.pad(uq, ((0, 0), (0, 0), (0, M_QK_PAD - M_QK)))
    w_uqt = uq.reshape(M_Q_LORA, M_HEADS * M_QK_PAD).T.astype(BF16)
    ukv = w_ukv.reshape(M_KV_LORA, M_HEADS, M_NOPE + M_V)
    w_k = jnp.pad(ukv[:, :, :M_NOPE], ((0, 0), (0, 0), (0, M_QK_PAD - M_NOPE)))
    w_k = w_k.reshape(M_KV_LORA, M_HEADS * M_QK_PAD).astype(BF16)
    w_vt = ukv[:, :, M_NOPE:].reshape(M_KV_LORA, M_WIDTH).T.astype(BF16)
    return (w_a, w_g, w_c, w_zt, w_uqt, w_k, w_vt,
            w_out_a.astype(BF16), w_out_b.astype(BF16), w_o.astype(BF16))


def _rope_tables(positions):
    half = M_ROPE // 2
    inv_freq = 1.0 / (ROPE_THETA ** (jnp.arange(0, M_ROPE, 2, dtype=F32) / M_ROPE))
    ang = positions.astype(F32)[:, None] * inv_freq
    cos, sin = jnp.cos(ang), jnp.sin(ang)
    s = positions.shape[0]
    z_lo = jnp.zeros((s, M_NOPE), F32)
    z_hi = jnp.zeros((s, M_QK_PAD - M_QK), F32)
    cos_k = jnp.concatenate([z_lo, cos, cos, z_hi], axis=1)
    sin_k = jnp.concatenate([z_lo, -sin, sin, z_hi], axis=1)
    return cos.T, sin.T, cos_k, sin_k


def kernel(x, c, positions, w_ada, b_ada, norm_g, w_in, q_norm_g, w_uq, kv_norm_g, w_ukv, w_out_a, w_out_b, w_o,
           rel_bias, final_norm_g):
    batch, s, _ = x.shape
    assert batch == 1 and s % (DIL_BLOCKS * A_BLOCK * A_PATTERNS[-1][1]) == 0
    assert s % MLA_Q_TILE == 0 and s % ROW_TILE == 0
    depth = w_ada.shape[0]
    xs = x[0]
    mod = _modulation(c, w_ada, b_ada)
    cos_t, sin_t, cos_k, sin_k = _rope_tables(positions[0])
    biases = [_dilated_bias(rel_bias, g, w, r) for g, (w, r) in enumerate(A_PATTERNS)]
    final_g = final_norm_g.reshape(1, D_MODEL)
    for l in range(depth):
        (w_a, w_g, w_c, w_zt, w_uqt, w_k, w_vt, wo_a, wo_b, wo) = _prep_layer(
            w_in[l], w_uq[l], w_ukv[l], w_out_a[l], w_out_b[l], w_o[l])
        shift, scale, gate = mod[l, 0], mod[l, 1], mod[l, 2]
        g = norm_g[l].reshape(1, D_MODEL)
        qkv0, qkv1, qkv2, a_z, mg = _in_proj_a(xs, g, scale, shift, w_a, w_g)
        qt, k, vt, mzt = _in_proj_m(xs, g, scale, shift, w_c, w_zt,
                                    q_norm_g[l].reshape(1, M_Q_LORA), kv_norm_g[l].reshape(1, M_KV_LORA),
                                    w_uqt, w_k, w_vt, cos_t, sin_t, cos_k, sin_k)
        outs = [_dilated_group(qkv, biases[gi], gi) for gi, qkv in enumerate((qkv0, qkv1, qkv2))]
        ymt = _mla_attention(qt, k, vt, mzt)
        xs = _out_proj(xs, [o for o, _ in outs], [ls for _, ls in outs], a_z, ymt, mg, gate,
                       wo_a, wo_b, wo, final_g, l == depth - 1)
    return xs[None]
```

```python
import functools
import math

import jax
import jax.numpy as jnp
from jax import lax
from jax.experimental import pallas as pl
from jax.experimental.pallas import tpu as pltpu

F32 = jnp.float32
BF16 = jnp.bfloat16

D_MODEL = 1024
A_PATTERNS = ((128, 1), (512, 4), (2048, 16))
A_GROUPS = 3
A_HEADS = 8
A_HEAD_DIM = 64
A_WIDTH = A_HEADS * A_HEAD_DIM
A_BLOCK = 128
A_QKV = 3 * A_GROUPS * A_WIDTH
M_HEADS = 16
M_Q_LORA = 256
M_KV_LORA = 128
M_NOPE = 64
M_ROPE = 32
M_V = 64
M_QK = M_NOPE + M_ROPE
M_QK_PAD = 128
M_WIDTH = M_HEADS * M_V
M_V_EXT = M_V + 16
ROPE_THETA = 10000.0
REL_BUCKETS = 32
REL_MAX_DIST = 2048
EPS = 1e-6
NEG_INF = -1e30
LANES = 128
PHASE_STEP = 4

ROW_TILE = 512
ATT_TILE = 512
MLA_Q_TILE = 2 * ATT_TILE
MLA_CHUNK = 256
MLA_UNROLL = 4
DIL_BLOCKS = 8
VMEM_LIMIT = 56 * 1024 * 1024


def _cparams(*sem, **extra):
    return pltpu.CompilerParams(dimension_semantics=sem, vmem_limit_bytes=VMEM_LIMIT, **extra)


def _const_spec(shape):
    nd = len(shape)
    return pl.BlockSpec(shape, lambda *_: (0,) * nd, pipeline_mode=pl.Buffered(1))


def _nt_dot(a, b):
    return lax.dot_general(a, b, (((1,), (1,)), ((), ())), preferred_element_type=F32)


def _sigmoid(x):
    return 0.5 * jnp.tanh(0.5 * x) + 0.5


def _modulated_norm(x, g, scale, shift):
    y = x * lax.rsqrt(jnp.mean(x * x, axis=-1, keepdims=True) + EPS)
    return (y * g) * (1.0 + scale) + shift


def _mod_kernel(c_ref, w_ref, b_ref, o_ref):
    c = c_ref[...]
    c_act = c * _sigmoid(c)
    o_ref[0, 0] = jnp.sum(c_act * w_ref[0], axis=0, keepdims=True) + b_ref[0, 0]


def _modulation(c, w_ada, b_ada):
    depth = w_ada.shape[0]
    c_col = c.reshape(D_MODEL, 1)
    b4 = b_ada.reshape(depth, 3, 1, D_MODEL)
    return pl.pallas_call(
        _mod_kernel,
        grid=(depth, 3),
        in_specs=[pl.BlockSpec((D_MODEL, 1), lambda l, j: (0, 0)),
                  pl.BlockSpec((1, D_MODEL, D_MODEL), lambda l, j: (l, 0, j)),
                  pl.BlockSpec((1, 1, 1, D_MODEL), lambda l, j: (l, j, 0, 0))],
        out_specs=pl.BlockSpec((1, 1, 1, D_MODEL), lambda l, j: (l, j, 0, 0)),
        out_shape=jax.ShapeDtypeStruct((depth, 3, 1, D_MODEL), F32),
        compiler_params=_cparams("parallel", "parallel"),
        name="adaln_mod",
    )(c_col, w_ada, b4)


def _in_a_kernel(x_ref, g_ref, sc_ref, sh_ref, wa_ref, wg_ref, qkv0_ref, qkv1_ref, qkv2_ref, az_ref, mg_ref,
                 stage1_sc, stage_sc, stage2_sc):
    h = _modulated_norm(x_ref[...], g_ref[...], sc_ref[...], sh_ref[...]).astype(BF16)
    tm = h.shape[0]
    gw = 3 * A_WIDTH
    nj = gw // LANES

    def project(c0, c1):
        return jnp.dot(h, wa_ref[:, c0:c1], preferred_element_type=F32)

    qkv0_ref[0] = project(0, gw).astype(BF16)
    for g, out_ref, stage_sc in ((1, qkv1_ref, stage1_sc), (2, qkv2_ref, stage_sc)):
        r = A_PATTERNS[g][1]
        a = project(g * gw, (g + 1) * gw)
        for j in range(nj):
            stage_sc[j] = a[:, j * LANES:(j + 1) * LANES]
        if r % (PHASE_STEP * PHASE_STEP) == 0:
            r2, seg = r // PHASE_STEP, tm // PHASE_STEP
            for p1 in range(PHASE_STEP):
                for j in range(nj):
                    stage2_sc[j, p1 * seg:(p1 + 1) * seg, :] = stage_sc[j, pl.ds(p1, seg, stride=PHASE_STEP), :]
            for p1 in range(PHASE_STEP):
                for p2 in range(r2):
                    for j in range(nj):
                        out_ref[p1 + PHASE_STEP * p2, :, j * LANES:(j + 1) * LANES] = (
                            stage2_sc[j, pl.ds(p1 * seg + p2, tm // r, stride=r2), :].astype(BF16))
        else:
            for p in range(r):
                for j in range(nj):
                    out_ref[p, :, j * LANES:(j + 1) * LANES] = stage_sc[j, pl.ds(p, tm // r, stride=r), :].astype(BF16)
    az_ref[...] = project(A_GROUPS * gw, A_GROUPS * gw + A_WIDTH)
    mg_ref[...] = jnp.dot(h, wg_ref[...], preferred_element_type=F32)


def _in_proj_a(x, g, scale, shift, w_a, w_g):
    s = x.shape[0]
    tm = ROW_TILE
    gw = 3 * A_WIDTH
    row = lambda i: (i, 0)
    phase_specs = [pl.BlockSpec((r, tm // r, gw), lambda i: (0, i, 0)) for _, r in A_PATTERNS]
    phase_shapes = [jax.ShapeDtypeStruct((r, s // r, gw), BF16) for _, r in A_PATTERNS]
    return pl.pallas_call(
        _in_a_kernel,
        grid=(s // tm,),
        in_specs=[pl.BlockSpec((tm, D_MODEL), row),
                  _const_spec((1, D_MODEL)), _const_spec((1, D_MODEL)), _const_spec((1, D_MODEL)),
                  _const_spec(w_a.shape), _const_spec(w_g.shape)],
        out_specs=phase_specs + [pl.BlockSpec((tm, A_WIDTH), row),
                                 pl.BlockSpec((tm, 2 * D_MODEL), row)],
        out_shape=phase_shapes + [jax.ShapeDtypeStruct((s, A_WIDTH), F32),
                                  jax.ShapeDtypeStruct((s, 2 * D_MODEL), F32)],
        scratch_shapes=[pltpu.VMEM((gw // LANES, tm, LANES), F32)] * 3,
        compiler_params=_cparams("parallel"),
        name="in_proj_a",
    )(x, g, scale, shift, w_a, w_g)


def _in_m_kernel(x_ref, g_ref, sc_ref, sh_ref, wc_ref, wzt_ref, qg_ref, kvg_ref, wuqt_ref, wk_ref, wvt_ref,
                 cost_ref, sint_ref, ck_ref, sk_ref, qt_ref, k_ref, vt_ref, mzt_ref):
    h = _modulated_norm(x_ref[...], g_ref[...], sc_ref[...], sh_ref[...]).astype(BF16)
    mzt_ref[...] = _nt_dot(wzt_ref[...], h)
    c = jnp.dot(h, wc_ref[...], preferred_element_type=F32)
    cq = c[:, :M_Q_LORA]
    ckv = c[:, M_Q_LORA:M_Q_LORA + M_KV_LORA]
    kr_a = c[:, M_Q_LORA + M_KV_LORA:M_Q_LORA + M_KV_LORA + M_QK_PAD]
    kr_b = c[:, M_Q_LORA + M_KV_LORA + M_QK_PAD:]
    cqn = (cq * lax.rsqrt(jnp.mean(cq * cq, axis=-1, keepdims=True) + EPS) * qg_ref[...]).astype(BF16)
    ckvn = (ckv * lax.rsqrt(jnp.mean(ckv * ckv, axis=-1, keepdims=True) + EPS) * kvg_ref[...]).astype(BF16)

    tm = cq.shape[0]
    half = M_ROPE // 2
    qt = _nt_dot(wuqt_ref[...], cqn).reshape(M_HEADS, M_QK_PAD, tm)
    cos = cost_ref[...][None]
    sin = sint_ref[...][None]
    t1 = qt[:, M_NOPE:M_NOPE + half]
    t2 = qt[:, M_NOPE + half:M_QK]
    qt_ref[:, :M_NOPE] = qt[:, :M_NOPE].astype(BF16)
    qt_ref[:, M_NOPE:M_NOPE + half] = (t1 * cos - t2 * sin).astype(BF16)
    qt_ref[:, M_NOPE + half:M_QK] = (t1 * sin + t2 * cos).astype(BF16)
    qt_ref[:, M_QK:] = qt[:, M_QK:].astype(BF16)

    kr = kr_a * ck_ref[...] + kr_b * sk_ref[...]
    k_all = jnp.dot(ckvn, wk_ref[...], preferred_element_type=F32)
    for hd in range(M_HEADS):
        k_ref[hd] = (k_all[:, hd * M_QK_PAD:(hd + 1) * M_QK_PAD] + kr).astype(BF16)

    vt = _nt_dot(wvt_ref[...], ckvn)
    vt_ref[:, 0, :M_V] = vt.reshape(M_HEADS, M_V, tm).astype(BF16)
    vt_ref[:, 0, M_V:] = jnp.ones((M_HEADS, M_V_EXT - M_V, tm), BF16)


def _in_proj_m(x, g, scale, shift, w_c, w_zt, qg, kvg, w_uqt, w_k, w_vt, cos_t, sin_t, cos_k, sin_k):
    s = x.shape[0]
    tm = ATT_TILE
    row = lambda i: (i, 0)
    col = lambda i: (0, i)
    return pl.pallas_call(
        _in_m_kernel,
        grid=(s // tm,),
        in_specs=[pl.BlockSpec((tm, D_MODEL), row),
                  _const_spec((1, D_MODEL)), _const_spec((1, D_MODEL)), _const_spec((1, D_MODEL)),
                  _const_spec(w_c.shape), _const_spec(w_zt.shape),
                  _const_spec(qg.shape), _const_spec(kvg.shape),
                  _const_spec(w_uqt.shape), _const_spec(w_k.shape), _const_spec(w_vt.shape),
                  pl.BlockSpec((M_ROPE // 2, tm), col), pl.BlockSpec((M_ROPE // 2, tm), col),
                  pl.BlockSpec((tm, M_QK_PAD), row), pl.BlockSpec((tm, M_QK_PAD), row)],
        out_specs=[pl.BlockSpec((M_HEADS, M_QK_PAD, tm), lambda i: (0, 0, i)),
                   pl.BlockSpec((M_HEADS, tm, M_QK_PAD), lambda i: (0, i, 0)),
                   pl.BlockSpec((M_HEADS, 1, M_V_EXT, tm), lambda i: (0, i, 0, 0)),
                   pl.BlockSpec((M_WIDTH, tm), col)],
        out_shape=[jax.ShapeDtypeStruct((M_HEADS, M_QK_PAD, s), BF16),
                   jax.ShapeDtypeStruct((M_HEADS, s, M_QK_PAD), BF16),
                   jax.ShapeDtypeStruct((M_HEADS, s // tm, M_V_EXT, tm), BF16),
                   jax.ShapeDtypeStruct((M_WIDTH, s), F32)],
        compiler_params=_cparams("parallel"),
        name="in_proj_m",
    )(x, g, scale, shift, w_c, w_zt, qg, kvg, w_uqt, w_k, w_vt, cos_t, sin_t, cos_k, sin_k)


def _dilated_kernel(q_ref, kp_ref, kc_ref, vp_ref, vc_ref, bias0_ref, bias_ref, o_ref, lse_ref):
    pairs = A_WIDTH // LANES
    lo = lax.broadcasted_iota(jnp.int32, (1, LANES), 1) < A_HEAD_DIM
    ones = jnp.ones((2 * A_BLOCK, LANES), BF16)
    zero = jnp.zeros((), BF16)

    for b in range(DIL_BLOCKS):
        rows = slice(b * A_BLOCK, (b + 1) * A_BLOCK)
        b_ref = bias0_ref if b == 0 else bias_ref
        q_all = q_ref[0, rows] * jnp.asarray(A_HEAD_DIM ** -0.5, BF16)

        def keys(prev_ref, cur_ref, sl):
            if b == 0:
                return jnp.concatenate([prev_ref[0, :, sl], cur_ref[0, :A_BLOCK, sl]], axis=0)
            return cur_ref[0, (b - 1) * A_BLOCK:(b + 1) * A_BLOCK, sl]

        scores, v_ext = [], []
        for pr in range(pairs):
            sl = slice(pr * LANES, (pr + 1) * LANES)
            k = keys(kp_ref, kc_ref, sl)
            v_ext.append(jnp.concatenate([keys(vp_ref, vc_ref, sl), ones], axis=1))
            q = q_all[:, sl]
            for half in range(2):
                qh = jnp.where(lo if half == 0 else jnp.logical_not(lo), q, zero)
                scores.append(_nt_dot(qh, k) + b_ref[0, 2 * pr + half])

        probs, maxes = [], []
        for s in scores:
            m = jnp.max(s, axis=-1, keepdims=True)
            probs.append(jnp.exp(s - m).astype(BF16))
            maxes.append(m)

        for pr in range(pairs):
            sl = slice(pr * LANES, (pr + 1) * LANES)
            outs, lses = [], []
            for half in range(2):
                ol = jnp.dot(probs[2 * pr + half], v_ext[pr], preferred_element_type=F32)
                l = ol[:, LANES:]
                outs.append(ol[:, :LANES] / l)
                lses.append(maxes[2 * pr + half] + jnp.log(l))
            o_ref[0, rows, sl] = jnp.where(lo, outs[0], outs[1])
            lse_ref[0, rows, sl] = jnp.where(lo, lses[0], lses[1])


def _dilated_group(qkv, bias, g):
    r, l_sub, _ = qkv.shape
    rows = DIL_BLOCKS * A_BLOCK
    blk = (1, rows, A_WIDTH)
    bias_blk = (1,) + bias.shape[1:]

    def cur(t):
        return pl.BlockSpec(blk, lambda p, n: (p, n, t))

    def prev(t):
        return pl.BlockSpec((1, A_BLOCK, A_WIDTH), lambda p, n: (p, jnp.maximum(DIL_BLOCKS * n - 1, 0), t))

    out_spec = pl.BlockSpec(blk, lambda p, n: (p, n, 0))
    return pl.pallas_call(
        _dilated_kernel,
        grid=(r, l_sub // rows),
        in_specs=[cur(0), prev(1), cur(1), prev(2), cur(2),
                  pl.BlockSpec(bias_blk, lambda p, n: (jnp.where(n == 0, 1, 0), 0, 0, 0)),
                  pl.BlockSpec(bias_blk, lambda p, n: (0, 0, 0, 0))],
        out_specs=[out_spec, out_spec],
        out_shape=[jax.ShapeDtypeStruct((r, l_sub, A_WIDTH), F32)] * 2,
        compiler_params=_cparams("parallel", "parallel"),
        name=f"dilated_g{g}",
    )(qkv, qkv, qkv, qkv, qkv, bias, bias)


def _t5_bucket(dist):
    exact = REL_BUCKETS // 2
    d = jnp.maximum(dist, 1).astype(F32)
    large = exact + (jnp.log(d / exact) / math.log(REL_MAX_DIST / exact) * (REL_BUCKETS - exact)).astype(jnp.int32)
    large = jnp.minimum(large, REL_BUCKETS - 1)
    return jnp.where(dist < exact, dist, large)


def _dilated_bias(rel_bias, g, window, r):
    qi = jnp.arange(A_BLOCK)[:, None]
    ki = jnp.arange(2 * A_BLOCK)[None, :]
    j = qi + A_BLOCK - ki
    band = (j >= 0) & (j <= window // r)
    tab = rel_bias[:, g * A_HEADS:(g + 1) * A_HEADS].astype(F32)
    pick = jax.nn.one_hot(_t5_bucket(jnp.maximum(j, 0) * r), REL_BUCKETS, dtype=F32)
    b = jnp.einsum("qkb,bh->hqk", pick, tab, precision=lax.Precision.HIGHEST)
    b = jnp.where(band[None], b, NEG_INF)
    first = jnp.where(ki[None] < A_BLOCK, NEG_INF, b)
    return jnp.stack([b, first])


def _mla_kernel(nq, qt_ref, qn_ref, k_ref, vt_ref, mzt_ref, o_ref, s0_sc, s1_sc, mt0_sc, mt1_sc, m_sc, acc_sc):
    i = pl.program_id(1)
    tk, tq, cw = ATT_TILE, MLA_Q_TILE, MLA_CHUNK
    bufs = ((s0_sc, mt0_sc), (s1_sc, mt1_sc))
    m_sc[...] = jnp.full(m_sc.shape, NEG_INF, F32)
    acc_sc[...] = jnp.zeros(acc_sc.shape, F32)

    def scores(kt, diagonal=False, q_ref=qt_ref, col0=0):
        k = k_ref[0, pl.ds(pl.multiple_of(kt * tk, tk), tk), :]
        s = jnp.dot(k, q_ref[0, :, col0:], preferred_element_type=F32)
        if diagonal:
            kpos = lax.broadcasted_iota(jnp.int32, s.shape, 0)
            qpos = lax.broadcasted_iota(jnp.int32, s.shape, 1)
            s = jnp.where(kpos <= qpos, s, NEG_INF)
        return s

    def fill(slot, s, col0=0):
        s_ref, mt_ref = bufs[slot]
        s_ref[:, col0:] = s
        mt_ref[:, col0:] = jnp.max(s, axis=0, keepdims=True)

    def accumulate_chunk(kt, slot, c):
        s_ref, mt_ref = bufs[slot]
        cols = slice(c * cw, (c + 1) * cw)
        m_prev = m_sc[:, cols]
        m_new = jnp.maximum(m_prev, mt_ref[:, cols])
        alpha = jnp.exp2(m_prev - m_new)
        p = jnp.exp2(s_ref[:, cols] - m_new).astype(BF16)
        acc_sc[:, cols] = alpha * acc_sc[:, cols] + jnp.dot(vt_ref[0, kt], p, preferred_element_type=F32)
        m_sc[:, cols] = m_new

    def stage(kt, slot, refill, col0=0):
        s_ref, mt_ref = bufs[slot]
        q_ref, fill0, k_tile = qt_ref, 0, kt + 2
        if refill == "diagonal":
            fill0 = slot * tk
        elif refill == "next":
            q_ref, k_tile = qn_ref, slot
        if refill is not None:
            k = k_ref[0, pl.ds(pl.multiple_of(k_tile * tk, tk), tk), :]
        parts = {}
        for c in range(tq // cw):
            if refill is not None and c * cw >= fill0:
                s = jnp.dot(k, q_ref[0, :, c * cw:(c + 1) * cw], preferred_element_type=F32)
                first_q = c * cw - fill0
                if refill == "diagonal" and first_q < tk - 1:
                    kpos = lax.broadcasted_iota(jnp.int32, s.shape, 0)
                    qpos = lax.broadcasted_iota(jnp.int32, s.shape, 1) + first_q
                    s = jnp.where(kpos <= qpos, s, NEG_INF)
                parts[c] = s
            if c * cw >= col0:
                accumulate_chunk(kt, slot, c)
        for c, s in parts.items():
            cols = slice(c * cw, (c + 1) * cw)
            s_ref[:, cols] = s
            mt_ref[:, cols] = jnp.max(s, axis=0, keepdims=True)

    def pair(pr, refill="ahead", diagonal=False):
        stage(2 * pr, 0, refill)
        stage(2 * pr + 1, 1, refill, col0=tk if diagonal else 0)

    last_refill = "next" if nq > 1 else None

    @pl.when(i == 0)
    def _():
        fill(0, scores(0, True))
        fill(1, scores(1, True, col0=tk), tk)
        pair(0, refill=last_refill, diagonal=True)

    @pl.when(i > 0)
    def _():
        full_pairs = i - 1

        def body(trip, carry):
            for u in range(MLA_UNROLL):
                pair(MLA_UNROLL * trip + u)
            return carry

        lax.fori_loop(0, full_pairs // MLA_UNROLL, body, 0)
        done = full_pairs - full_pairs % MLA_UNROLL
        width = MLA_UNROLL // 2
        while width >= 1:
            @pl.when((full_pairs % (2 * width)) >= width)
            def _(done=done, width=width):
                for u in range(width):
                    pair(done + u)

            done = done + jnp.where((full_pairs % (2 * width)) >= width, width, 0)
            width //= 2

        @pl.when(i < nq - 1)
        def _():
            pair(i - 1, refill="diagonal")
            pair(i, refill="next", diagonal=True)

        @pl.when(i == nq - 1)
        def _():
            pair(i - 1, refill="diagonal")
            pair(i, refill=None, diagonal=True)

    mz = mzt_ref[...]
    gate = mz * _sigmoid(mz)
    acc = acc_sc[...]
    o_ref[...] = (acc[:M_V] / acc[M_V:M_V + 1] * gate).astype(BF16)


def _mla_attention(qt, k, vt, mzt):
    s = k.shape[1]
    tk, tq = ATT_TILE, MLA_Q_TILE
    nq = s // tq
    return pl.pallas_call(
        functools.partial(_mla_kernel, nq),
        grid=(M_HEADS, nq),
        in_specs=[pl.BlockSpec((1, M_QK_PAD, tq), lambda h, i: (h, 0, i)),
                  pl.BlockSpec((1, M_QK_PAD, tq), lambda h, i: (h, 0, jnp.minimum(i + 1, nq - 1))),
                  pl.BlockSpec((1, s, M_QK_PAD), lambda h, i: (h, 0, 0)),
                  pl.BlockSpec((1, s // tk, M_V_EXT, tk), lambda h, i: (h, 0, 0, 0)),
                  pl.BlockSpec((M_V, tq), lambda h, i: (h, i))],
        out_specs=pl.BlockSpec((M_V, tq), lambda h, i: (h, i)),
        out_shape=jax.ShapeDtypeStruct((M_WIDTH, s), BF16),
        scratch_shapes=[pltpu.VMEM((tk, tq), F32), pltpu.VMEM((tk, tq), F32),
                        pltpu.VMEM((1, tq), F32), pltpu.VMEM((1, tq), F32),
                        pltpu.VMEM((1, tq), F32), pltpu.VMEM((M_V_EXT, tq), F32)],
        compiler_params=_cparams("arbitrary", "arbitrary"),
        name="mla_attention",
    )(qt, qt, k, vt, mzt)


def _out_kernel(final, x_ref, o0_ref, o1_ref, o2_ref, l0_ref, l1_ref, l2_ref, az_ref, ymt_ref, mg_ref, gate_ref,
                wa_ref, wb_ref, wo_ref, fg_ref, out_ref, o1_sc, o2_sc, l1_sc, l2_sc):
    tm = x_ref.shape[0]
    nj = A_WIDTH // LANES
    for src, dst in ((o1_ref, o1_sc), (o2_ref, o2_sc), (l1_ref, l1_sc), (l2_ref, l2_sc)):
        r = src.shape[0]
        for p in range(r):
            for j in range(nj):
                dst[j, pl.ds(p, tm // r, stride=r), :] = src[p, :, j * LANES:(j + 1) * LANES]

    def rows(sc):
        return jnp.concatenate([sc[j] for j in range(nj)], axis=1)

    l0, l1, l2 = l0_ref[0], rows(l1_sc), rows(l2_sc)
    mx = jnp.maximum(jnp.maximum(l0, l1), l2)
    e0, e1, e2 = jnp.exp(l0 - mx), jnp.exp(l1 - mx), jnp.exp(l2 - mx)
    mix = (o0_ref[0] * e0 + rows(o1_sc) * e1 + rows(o2_sc) * e2) / (e0 + e1 + e2)
    az = az_ref[...]
    y_a = (mix * (az * _sigmoid(az))).astype(BF16)
    t_a = jnp.dot(y_a, wa_ref[...], preferred_element_type=F32)
    t_b = lax.dot_general(ymt_ref[...], wb_ref[...], (((0,), (0,)), ((), ())), preferred_element_type=F32)
    mg = mg_ref[...]
    g_a = _sigmoid(mg[:, :D_MODEL])
    g_m = _sigmoid(mg[:, D_MODEL:])
    merged = (g_a * t_a + g_m * t_b).astype(BF16)
    y = x_ref[...] + gate_ref[...] * jnp.dot(merged, wo_ref[...], preferred_element_type=F32)
    if final:
        y = y * lax.rsqrt(jnp.mean(y * y, axis=-1, keepdims=True) + EPS) * fg_ref[...]
    out_ref[...] = y


def _out_proj(x, o_groups, lse_groups, a_z, ymt, mg, gate, w_a, w_b, w_o, final_g, final):
    s = x.shape[0]
    tm = ROW_TILE
    row = lambda i: (i, 0)
    phase_specs = [pl.BlockSpec((r, tm // r, A_WIDTH), lambda i: (0, i, 0)) for _, r in A_PATTERNS]
    return pl.pallas_call(
        functools.partial(_out_kernel, final),
        grid=(s // tm,),
        in_specs=[pl.BlockSpec((tm, D_MODEL), row)] + phase_specs * 2
                 + [pl.BlockSpec((tm, A_WIDTH), row),
                    pl.BlockSpec((M_WIDTH, tm), lambda i: (0, i)),
                    pl.BlockSpec((tm, 2 * D_MODEL), row),
                    _const_spec((1, D_MODEL)),
                    _const_spec(w_a.shape), _const_spec(w_b.shape), _const_spec(w_o.shape),
                    _const_spec((1, D_MODEL))],
        out_specs=pl.BlockSpec((tm, D_MODEL), row),
        out_shape=jax.ShapeDtypeStruct((s, D_MODEL), F32),
        scratch_shapes=[pltpu.VMEM((A_WIDTH // LANES, tm, LANES), F32)] * 4,
        compiler_params=_cparams("parallel"),
        name="out_proj",
    )(x, *o_groups, *lse_groups, a_z, ymt, mg, gate, w_a, w_b, w_o, final_g)


def _prep_layer(w_in, w_uq, w_ukv, w_out_a, w_out_b, w_o):
    o_az = A_QKV
    o_cq = o_az + A_WIDTH
    o_ckv = o_cq + M_Q_LORA
    o_kr = o_ckv + M_KV_LORA
    o_mz = o_kr + M_ROPE
    o_mg = o_mz + M_WIDTH
    half = M_ROPE // 2
    w_qkv = w_in[:, :A_QKV].reshape(D_MODEL, 3, A_GROUPS, A_WIDTH).transpose(0, 2, 1, 3).reshape(D_MODEL, A_QKV)
    w_a = jnp.concatenate([w_qkv, w_in[:, o_az:o_cq]], axis=1).astype(BF16)
    w_g = w_in[:, o_mg:].astype(BF16)
    w_kr = w_in[:, o_kr:o_mz]
    z_lo = jnp.zeros((D_MODEL, M_NOPE), F32)
    z_hi = jnp.zeros((D_MODEL, M_QK_PAD - M_QK), F32)
    kr_a = jnp.concatenate([z_lo, w_kr, z_hi], axis=1)
    kr_b = jnp.concatenate([z_lo, w_kr[:, half:], w_kr[:, :half], z_hi], axis=1)
    w_c = jnp.concatenate([w_in[:, o_cq:o_kr], kr_a, kr_b], axis=1).astype(BF16)
    w_zt = w_in[:, o_mz:o_mg].T.astype(BF16)
    scale = M_QK ** -0.5 * math.log2(math.e)
    uq = (w_uq * scale).reshape(M_Q_LORA, M_HEADS, M_QK)
    uq = jnp.pad(uq, ((0, 0), (0, 0), (0, M_QK_PAD - M_QK)))
    w_uqt = uq.reshape(M_Q_LORA, M_HEADS * M_QK_PAD).T.astype(BF16)
    ukv = w_ukv.reshape(M_KV_LORA, M_HEADS, M_NOPE + M_V)
    w_k = jnp.pad(ukv[:, :, :M_NOPE], ((0, 0), (0, 0), (0, M_QK_PAD - M_NOPE)))
    w_k = w_k.reshape(M_KV_LORA, M_HEADS * M_QK_PAD).astype(BF16)
    w_vt = ukv[:, :, M_NOPE:].reshape(M_KV_LORA, M_WIDTH).T.astype(BF16)
    return (w_a, w_g, w_c, w_zt, w_uqt, w_k, w_vt,
            w_out_a.astype(BF16), w_out_b.astype(BF16), w_o.astype(BF16))


def _rope_tables(positions):
    half = M_ROPE // 2
    inv_freq = 1.0 / (ROPE_THETA ** (jnp.arange(0, M_ROPE, 2, dtype=F32) / M_ROPE))
    ang = positions.astype(F32)[:, None] * inv_freq
    cos, sin = jnp.cos(ang), jnp.sin(ang)
    s = positions.shape[0]
    z_lo = jnp.zeros((s, M_NOPE), F32)
    z_hi = jnp.zeros((s, M_QK_PAD - M_QK), F32)
    cos_k = jnp.concatenate([z_lo, cos, cos, z_hi], axis=1)
    sin_k = jnp.concatenate([z_lo, -sin, sin, z_hi], axis=1)
    return cos.T, sin.T, cos_k, sin_k


def kernel(x, c, positions, w_ada, b_ada, norm_g, w_in, q_norm_g, w_uq, kv_norm_g, w_ukv, w_out_a, w_out_b, w_o,
           rel_bias, final_norm_g):
    batch, s, _ = x.shape
    assert batch == 1 and s % (DIL_BLOCKS * A_BLOCK * A_PATTERNS[-1][1]) == 0
    assert s % MLA_Q_TILE == 0 and s % ROW_TILE == 0
    depth = w_ada.shape[0]
    xs = x[0]
    mod = _modulation(c, w_ada, b_ada)
    cos_t, sin_t, cos_k, sin_k = _rope_tables(positions[0])
    biases = [_dilated_bias(rel_bias, g, w, r) for g, (w, r) in enumerate(A_PATTERNS)]
    final_g = final_norm_g.reshape(1, D_MODEL)
    for l in range(depth):
        (w_a, w_g, w_c, w_zt, w_uqt, w_k, w_vt, wo_a, wo_b, wo) = _prep_layer(
            w_in[l], w_uq[l], w_ukv[l], w_out_a[l], w_out_b[l], w_o[l])
        shift, scale, gate = mod[l, 0], mod[l, 1], mod[l, 2]
        g = norm_g[l].reshape(1, D_MODEL)
        qkv0, qkv1, qkv2, a_z, mg = _in_proj_a(xs, g, scale, shift, w_a, w_g)
        qt, k, vt, mzt = _in_proj_m(xs, g, scale, shift, w_c, w_zt,
                                    q_norm_g[l].reshape(1, M_Q_LORA), kv_norm_g[l].reshape(1, M_KV_LORA),
                                    w_uqt, w_k, w_vt, cos_t, sin_t, cos_k, sin_k)
        outs = [_dilated_group(qkv, biases[gi], gi) for gi, qkv in enumerate((qkv0, qkv1, qkv2))]
        ymt = _mla_attention(qt, k, vt, mzt)
        xs = _out_proj(xs, [o for o, _ in outs], [ls for _, ls in outs], a_z, ymt, mg, gate,
                       wo_a, wo_b, wo, final_g, l == depth - 1)
    return xs[None]
```

```python
import functools
import math

import jax
import jax.numpy as jnp
from jax import lax
from jax.experimental import pallas as pl
from jax.experimental.pallas import tpu as pltpu

F32 = jnp.float32
BF16 = jnp.bfloat16

D_MODEL = 1024
A_PATTERNS = ((128, 1), (512, 4), (2048, 16))
A_GROUPS = 3
A_HEADS = 8
A_HEAD_DIM = 64
A_WIDTH = A_HEADS * A_HEAD_DIM
A_BLOCK = 128
A_QKV = 3 * A_GROUPS * A_WIDTH
M_HEADS = 16
M_Q_LORA = 256
M_KV_LORA = 128
M_NOPE = 64
M_ROPE = 32
M_V = 64
M_QK = M_NOPE + M_ROPE
M_QK_PAD = 128
M_WIDTH = M_HEADS * M_V
M_V_EXT = M_V + 16
ROPE_THETA = 10000.0
REL_BUCKETS = 32
REL_MAX_DIST = 2048
EPS = 1e-6
NEG_INF = -1e30
LANES = 128
PHASE_STEP = 4

ROW_TILE = 512
ATT_TILE = 512
MLA_Q_TILE = 2 * ATT_TILE
MLA_CHUNK = 256
MLA_UNROLL = 8
DIL_BLOCKS = 8
VMEM_LIMIT = 56 * 1024 * 1024


def _cparams(*sem, **extra):
    return pltpu.CompilerParams(dimension_semantics=sem, vmem_limit_bytes=VMEM_LIMIT, **extra)


def _const_spec(shape):
    nd = len(shape)
    return pl.BlockSpec(shape, lambda *_: (0,) * nd, pipeline_mode=pl.Buffered(1))


def _nt_dot(a, b):
    return lax.dot_general(a, b, (((1,), (1,)), ((), ())), preferred_element_type=F32)


def _sigmoid(x):
    return 0.5 * jnp.tanh(0.5 * x) + 0.5


def _modulated_norm(x, g, scale, shift):
    y = x * lax.rsqrt(jnp.mean(x * x, axis=-1, keepdims=True) + EPS)
    return (y * g) * (1.0 + scale) + shift


def _mod_kernel(c_ref, w_ref, b_ref, o_ref):
    c = c_ref[...]
    c_act = c * _sigmoid(c)
    o_ref[0, 0] = jnp.sum(c_act * w_ref[0], axis=0, keepdims=True) + b_ref[0, 0]


def _modulation(c, w_ada, b_ada):
    depth = w_ada.shape[0]
    c_col = c.reshape(D_MODEL, 1)
    b4 = b_ada.reshape(depth, 3, 1, D_MODEL)
    return pl.pallas_call(
        _mod_kernel,
        grid=(depth, 3),
        in_specs=[pl.BlockSpec((D_MODEL, 1), lambda l, j: (0, 0)),
                  pl.BlockSpec((1, D_MODEL, D_MODEL), lambda l, j: (l, 0, j)),
                  pl.BlockSpec((1, 1, 1, D_MODEL), lambda l, j: (l, j, 0, 0))],
        out_specs=pl.BlockSpec((1, 1, 1, D_MODEL), lambda l, j: (l, j, 0, 0)),
        out_shape=jax.ShapeDtypeStruct((depth, 3, 1, D_MODEL), F32),
        compiler_params=_cparams("parallel", "parallel"),
        name="adaln_mod",
    )(c_col, w_ada, b4)


def _in_a_kernel(x_ref, g_ref, sc_ref, sh_ref, wa_ref, wg_ref, qkv0_ref, qkv1_ref, qkv2_ref, az_ref, mg_ref,
                 stage1_sc, stage_sc, stage2_sc):
    h = _modulated_norm(x_ref[...], g_ref[...], sc_ref[...], sh_ref[...]).astype(BF16)
    tm = h.shape[0]
    gw = 3 * A_WIDTH
    nj = gw // LANES

    def project(c0, c1):
        return jnp.dot(h, wa_ref[:, c0:c1], preferred_element_type=F32)

    qkv0_ref[0] = project(0, gw).astype(BF16)
    for g, out_ref, stage_sc in ((1, qkv1_ref, stage1_sc), (2, qkv2_ref, stage_sc)):
        r = A_PATTERNS[g][1]
        a = project(g * gw, (g + 1) * gw)
        for j in range(nj):
            stage_sc[j] = a[:, j * LANES:(j + 1) * LANES]
        if r % (PHASE_STEP * PHASE_STEP) == 0:
            r2, seg = r // PHASE_STEP, tm // PHASE_STEP
            for p1 in range(PHASE_STEP):
                for j in range(nj):
                    stage2_sc[j, p1 * seg:(p1 + 1) * seg, :] = stage_sc[j, pl.ds(p1, seg, stride=PHASE_STEP), :]
            for p1 in range(PHASE_STEP):
                for p2 in range(r2):
                    for j in range(nj):
                        out_ref[p1 + PHASE_STEP * p2, :, j * LANES:(j + 1) * LANES] = (
                            stage2_sc[j, pl.ds(p1 * seg + p2, tm // r, stride=r2), :].astype(BF16))
        else:
            for p in range(r):
                for j in range(nj):
                    out_ref[p, :, j * LANES:(j + 1) * LANES] = stage_sc[j, pl.ds(p, tm // r, stride=r), :].astype(BF16)
    az_ref[...] = project(A_GROUPS * gw, A_GROUPS * gw + A_WIDTH)
    mg_ref[...] = jnp.dot(h, wg_ref[...], preferred_element_type=F32)


def _in_proj_a(x, g, scale, shift, w_a, w_g):
    s = x.shape[0]
    tm = ROW_TILE
    gw = 3 * A_WIDTH
    row = lambda i: (i, 0)
    phase_specs = [pl.BlockSpec((r, tm // r, gw), lambda i: (0, i, 0)) for _, r in A_PATTERNS]
    phase_shapes = [jax.ShapeDtypeStruct((r, s // r, gw), BF16) for _, r in A_PATTERNS]
    return pl.pallas_call(
        _in_a_kernel,
        grid=(s // tm,),
        in_specs=[pl.BlockSpec((tm, D_MODEL), row),
                  _const_spec((1, D_MODEL)), _const_spec((1, D_MODEL)), _const_spec((1, D_MODEL)),
                  _const_spec(w_a.shape), _const_spec(w_g.shape)],
        out_specs=phase_specs + [pl.BlockSpec((tm, A_WIDTH), row),
                                 pl.BlockSpec((tm, 2 * D_MODEL), row)],
        out_shape=phase_shapes + [jax.ShapeDtypeStruct((s, A_WIDTH), F32),
                                  jax.ShapeDtypeStruct((s, 2 * D_MODEL), F32)],
        scratch_shapes=[pltpu.VMEM((gw // LANES, tm, LANES), F32)] * 3,
        compiler_params=_cparams("parallel"),
        name="in_proj_a",
    )(x, g, scale, shift, w_a, w_g)


def _in_m_kernel(x_ref, g_ref, sc_ref, sh_ref, wc_ref, wzt_ref, qg_ref, kvg_ref, wuqt_ref, wk_ref, wvt_ref,
                 cost_ref, sint_ref, ck_ref, sk_ref, qt_ref, k_ref, vt_ref, mzt_ref):
    h = _modulated_norm(x_ref[...], g_ref[...], sc_ref[...], sh_ref[...]).astype(BF16)
    mzt_ref[...] = _nt_dot(wzt_ref[...], h)
    c = jnp.dot(h, wc_ref[...], preferred_element_type=F32)
    cq = c[:, :M_Q_LORA]
    ckv = c[:, M_Q_LORA:M_Q_LORA + M_KV_LORA]
    kr_a = c[:, M_Q_LORA + M_KV_LORA:M_Q_LORA + M_KV_LORA + M_QK_PAD]
    kr_b = c[:, M_Q_LORA + M_KV_LORA + M_QK_PAD:]
    cqn = (cq * lax.rsqrt(jnp.mean(cq * cq, axis=-1, keepdims=True) + EPS) * qg_ref[...]).astype(BF16)
    ckvn = (ckv * lax.rsqrt(jnp.mean(ckv * ckv, axis=-1, keepdims=True) + EPS) * kvg_ref[...]).astype(BF16)

    tm = cq.shape[0]
    half = M_ROPE // 2
    qt = _nt_dot(wuqt_ref[...], cqn).reshape(M_HEADS, M_QK_PAD, tm)
    cos = cost_ref[...][None]
    sin = sint_ref[...][None]
    t1 = qt[:, M_NOPE:M_NOPE + half]
    t2 = qt[:, M_NOPE + half:M_QK]
    qt_ref[:, :M_NOPE] = qt[:, :M_NOPE].astype(BF16)
    qt_ref[:, M_NOPE:M_NOPE + half] = (t1 * cos - t2 * sin).astype(BF16)
    qt_ref[:, M_NOPE + half:M_QK] = (t1 * sin + t2 * cos).astype(BF16)
    qt_ref[:, M_QK:] = qt[:, M_QK:].astype(BF16)

    kr = kr_a * ck_ref[...] + kr_b * sk_ref[...]
    k_all = jnp.dot(ckvn, wk_ref[...], preferred_element_type=F32)
    for hd in range(M_HEADS):
        k_ref[hd] = (k_all[:, hd * M_QK_PAD:(hd + 1) * M_QK_PAD] + kr).astype(BF16)

    vt = _nt_dot(wvt_ref[...], ckvn)
    vt_ref[:, 0, :M_V] = vt.reshape(M_HEADS, M_V, tm).astype(BF16)
    vt_ref[:, 0, M_V:] = jnp.ones((M_HEADS, M_V_EXT - M_V, tm), BF16)


def _in_proj_m(x, g, scale, shift, w_c, w_zt, qg, kvg, w_uqt, w_k, w_vt, cos_t, sin_t, cos_k, sin_k):
    s = x.shape[0]
    tm = ATT_TILE
    row = lambda i: (i, 0)
    col = lambda i: (0, i)
    return pl.pallas_call(
        _in_m_kernel,
        grid=(s // tm,),
        in_specs=[pl.BlockSpec((tm, D_MODEL), row),
                  _const_spec((1, D_MODEL)), _const_spec((1, D_MODEL)), _const_spec((1, D_MODEL)),
                  _const_spec(w_c.shape), _const_spec(w_zt.shape),
                  _const_spec(qg.shape), _const_spec(kvg.shape),
                  _const_spec(w_uqt.shape), _const_spec(w_k.shape), _const_spec(w_vt.shape),
                  pl.BlockSpec((M_ROPE // 2, tm), col), pl.BlockSpec((M_ROPE // 2, tm), col),
                  pl.BlockSpec((tm, M_QK_PAD), row), pl.BlockSpec((tm, M_QK_PAD), row)],
        out_specs=[pl.BlockSpec((M_HEADS, M_QK_PAD, tm), lambda i: (0, 0, i)),
                   pl.BlockSpec((M_HEADS, tm, M_QK_PAD), lambda i: (0, i, 0)),
                   pl.BlockSpec((M_HEADS, 1, M_V_EXT, tm), lambda i: (0, i, 0, 0)),
                   pl.BlockSpec((M_WIDTH, tm), col)],
        out_shape=[jax.ShapeDtypeStruct((M_HEADS, M_QK_PAD, s), BF16),
                   jax.ShapeDtypeStruct((M_HEADS, s, M_QK_PAD), BF16),
                   jax.ShapeDtypeStruct((M_HEADS, s // tm, M_V_EXT, tm), BF16),
                   jax.ShapeDtypeStruct((M_WIDTH, s), F32)],
        compiler_params=_cparams("parallel"),
        name="in_proj_m",
    )(x, g, scale, shift, w_c, w_zt, qg, kvg, w_uqt, w_k, w_vt, cos_t, sin_t, cos_k, sin_k)


def _dilated_kernel(q_ref, kp_ref, kc_ref, vp_ref, vc_ref, bias0_ref, bias_ref, o_ref, lse_ref):
    pairs = A_WIDTH // LANES
    lo = lax.broadcasted_iota(jnp.int32, (1, LANES), 1) < A_HEAD_DIM
    ones = jnp.ones((2 * A_BLOCK, LANES), BF16)
    zero = jnp.zeros((), BF16)

    for b in range(DIL_BLOCKS):
        rows = slice(b * A_BLOCK, (b + 1) * A_BLOCK)
        b_ref = bias0_ref if b == 0 else bias_ref
        q_all = q_ref[0, rows] * jnp.asarray(A_HEAD_DIM ** -0.5, BF16)

        def keys(prev_ref, cur_ref, sl):
            if b == 0:
                return jnp.concatenate([prev_ref[0, :, sl], cur_ref[0, :A_BLOCK, sl]], axis=0)
            return cur_ref[0, (b - 1) * A_BLOCK:(b + 1) * A_BLOCK, sl]

        scores, v_ext = [], []
        for pr in range(pairs):
            sl = slice(pr * LANES, (pr + 1) * LANES)
            k = keys(kp_ref, kc_ref, sl)
            v_ext.append(jnp.concatenate([keys(vp_ref, vc_ref, sl), ones], axis=1))
            q = q_all[:, sl]
            for half in range(2):
                qh = jnp.where(lo if half == 0 else jnp.logical_not(lo), q, zero)
                scores.append(_nt_dot(qh, k) + b_ref[0, 2 * pr + half])

        probs, maxes = [], []
        for s in scores:
            m = jnp.max(s, axis=-1, keepdims=True)
            probs.append(jnp.exp(s - m).astype(BF16))
            maxes.append(m)

        for pr in range(pairs):
            sl = slice(pr * LANES, (pr + 1) * LANES)
            outs, lses = [], []
            for half in range(2):
                ol = jnp.dot(probs[2 * pr + half], v_ext[pr], preferred_element_type=F32)
                l = ol[:, LANES:]
                outs.append(ol[:, :LANES] / l)
                lses.append(maxes[2 * pr + half] + jnp.log(l))
            o_ref[0, rows, sl] = jnp.where(lo, outs[0], outs[1])
            lse_ref[0, rows, sl] = jnp.where(lo, lses[0], lses[1])


def _dilated_group(qkv, bias, g):
    r, l_sub, _ = qkv.shape
    rows = DIL_BLOCKS * A_BLOCK
    blk = (1, rows, A_WIDTH)
    bias_blk = (1,) + bias.shape[1:]

    def cur(t):
        return pl.BlockSpec(blk, lambda p, n: (p, n, t))

    def prev(t):
        return pl.BlockSpec((1, A_BLOCK, A_WIDTH), lambda p, n: (p, jnp.maximum(DIL_BLOCKS * n - 1, 0), t))

    out_spec = pl.BlockSpec(blk, lambda p, n: (p, n, 0))
    return pl.pallas_call(
        _dilated_kernel,
        grid=(r, l_sub // rows),
        in_specs=[cur(0), prev(1), cur(1), prev(2), cur(2),
                  pl.BlockSpec(bias_blk, lambda p, n: (jnp.where(n == 0, 1, 0), 0, 0, 0)),
                  pl.BlockSpec(bias_blk, lambda p, n: (0, 0, 0, 0))],
        out_specs=[out_spec, out_spec],
        out_shape=[jax.ShapeDtypeStruct((r, l_sub, A_WIDTH), F32)] * 2,
        compiler_params=_cparams("parallel", "parallel"),
        name=f"dilated_g{g}",
    )(qkv, qkv, qkv, qkv, qkv, bias, bias)


def _t5_bucket(dist):
    exact = REL_BUCKETS // 2
    d = jnp.maximum(dist, 1).astype(F32)
    large = exact + (jnp.log(d / exact) / math.log(REL_MAX_DIST / exact) * (REL_BUCKETS - exact)).astype(jnp.int32)
    large = jnp.minimum(large, REL_BUCKETS - 1)
    return jnp.where(dist < exact, dist, large)


def _dilated_bias(rel_bias, g, window, r):
    qi = jnp.arange(A_BLOCK)[:, None]
    ki = jnp.arange(2 * A_BLOCK)[None, :]
    j = qi + A_BLOCK - ki
    band = (j >= 0) & (j <= window // r)
    tab = rel_bias[:, g * A_HEADS:(g + 1) * A_HEADS].astype(F32)
    pick = jax.nn.one_hot(_t5_bucket(jnp.maximum(j, 0) * r), REL_BUCKETS, dtype=F32)
    b = jnp.einsum("qkb,bh->hqk", pick, tab, precision=lax.Precision.HIGHEST)
    b = jnp.where(band[None], b, NEG_INF)
    first = jnp.where(ki[None] < A_BLOCK, NEG_INF, b)
    return jnp.stack([b, first])


def _mla_kernel(nq, qt_ref, qn_ref, k_ref, vt_ref, mzt_ref, o_ref, s0_sc, s1_sc, mt0_sc, mt1_sc, m_sc, acc_sc):
    i = pl.program_id(1)
    tk, tq, cw = ATT_TILE, MLA_Q_TILE, MLA_CHUNK
    bufs = ((s0_sc, mt0_sc), (s1_sc, mt1_sc))
    m_sc[...] = jnp.full(m_sc.shape, NEG_INF, F32)
    acc_sc[...] = jnp.zeros(acc_sc.shape, F32)

    def scores(kt, diagonal=False, q_ref=qt_ref, col0=0):
        k = k_ref[0, pl.ds(pl.multiple_of(kt * tk, tk), tk), :]
        s = jnp.dot(k, q_ref[0, :, col0:], preferred_element_type=F32)
        if diagonal:
            kpos = lax.broadcasted_iota(jnp.int32, s.shape, 0)
            qpos = lax.broadcasted_iota(jnp.int32, s.shape, 1)
            s = jnp.where(kpos <= qpos, s, NEG_INF)
        return s

    def fill(slot, s, col0=0):
        s_ref, mt_ref = bufs[slot]
        s_ref[:, col0:] = s
        mt_ref[:, col0:] = jnp.max(s, axis=0, keepdims=True)

    def accumulate_chunk(kt, slot, c):
        s_ref, mt_ref = bufs[slot]
        cols = slice(c * cw, (c + 1) * cw)
        m_prev = m_sc[:, cols]
        m_new = jnp.maximum(m_prev, mt_ref[:, cols])
        alpha = jnp.exp2(m_prev - m_new)
        p = jnp.exp2(s_ref[:, cols] - m_new).astype(BF16)
        acc_sc[:, cols] = alpha * acc_sc[:, cols] + jnp.dot(vt_ref[0, kt], p, preferred_element_type=F32)
        m_sc[:, cols] = m_new

    def stage(kt, slot, refill, col0=0):
        s_ref, mt_ref = bufs[slot]
        q_ref, fill0, k_tile = qt_ref, 0, kt + 2
        if refill == "diagonal":
            fill0 = slot * tk
        elif refill == "next":
            q_ref, k_tile = qn_ref, slot
        if refill is not None:
            k = k_ref[0, pl.ds(pl.multiple_of(k_tile * tk, tk), tk), :]
        parts = {}
        for c in range(tq // cw):
            if refill is not None and c * cw >= fill0:
                s = jnp.dot(k, q_ref[0, :, c * cw:(c + 1) * cw], preferred_element_type=F32)
                first_q = c * cw - fill0
                if refill == "diagonal" and first_q < tk - 1:
                    kpos = lax.broadcasted_iota(jnp.int32, s.shape, 0)
                    qpos = lax.broadcasted_iota(jnp.int32, s.shape, 1) + first_q
                    s = jnp.where(kpos <= qpos, s, NEG_INF)
                parts[c] = s
            if c * cw >= col0:
                accumulate_chunk(kt, slot, c)
        for c, s in parts.items():
            cols = slice(c * cw, (c + 1) * cw)
            s_ref[:, cols] = s
            mt_ref[:, cols] = jnp.max(s, axis=0, keepdims=True)

    def pair(pr, refill="ahead", diagonal=False):
        stage(2 * pr, 0, refill)
        stage(2 * pr + 1, 1, refill, col0=tk if diagonal else 0)

    last_refill = "next" if nq > 1 else None

    @pl.when(i == 0)
    def _():
        fill(0, scores(0, True))
        fill(1, scores(1, True, col0=tk), tk)
        pair(0, refill=last_refill, diagonal=True)

    @pl.when(i > 0)
    def _():
        full_pairs = i - 1

        def body(trip, carry):
            for u in range(MLA_UNROLL):
                pair(MLA_UNROLL * trip + u)
            return carry

        lax.fori_loop(0, full_pairs // MLA_UNROLL, body, 0)
        done = full_pairs - full_pairs % MLA_UNROLL
        width = MLA_UNROLL // 2
        while width >= 1:
            @pl.when((full_pairs % (2 * width)) >= width)
            def _(done=done, width=width):
                for u in range(width):
                    pair(done + u)

            done = done + jnp.where((full_pairs % (2 * width)) >= width, width, 0)
            width //= 2

        @pl.when(i < nq - 1)
        def _():
            pair(i - 1, refill="diagonal")
            pair(i, refill="next", diagonal=True)

        @pl.when(i == nq - 1)
        def _():
            pair(i - 1, refill="diagonal")
            pair(i, refill=None, diagonal=True)

    mz = mzt_ref[...]
    gate = mz * _sigmoid(mz)
    acc = acc_sc[...]
    o_ref[...] = (acc[:M_V] / acc[M_V:M_V + 1] * gate).astype(BF16)


def _mla_attention(qt, k, vt, mzt):
    s = k.shape[1]
    tk, tq = ATT_TILE, MLA_Q_TILE
    nq = s // tq
    return pl.pallas_call(
        functools.partial(_mla_kernel, nq),
        grid=(M_HEADS, nq),
        in_specs=[pl.BlockSpec((1, M_QK_PAD, tq), lambda h, i: (h, 0, i)),
                  pl.BlockSpec((1, M_QK_PAD, tq), lambda h, i: (h, 0, jnp.minimum(i + 1, nq - 1))),
                  pl.BlockSpec((1, s, M_QK_PAD), lambda h, i: (h, 0, 0)),
                  pl.BlockSpec((1, s // tk, M_V_EXT, tk), lambda h, i: (h, 0, 0, 0)),
                  pl.BlockSpec((M_V, tq), lambda h, i: (h, i))],
        out_specs=pl.BlockSpec((M_V, tq), lambda h, i: (h, i)),
        out_shape=jax.ShapeDtypeStruct((M_WIDTH, s), BF16),
        scratch_shapes=[pltpu.VMEM((tk, tq), F32), pltpu.VMEM((tk, tq), F32),
                        pltpu.VMEM((1, tq), F32), pltpu.VMEM((1, tq), F32),
                        pltpu.VMEM((1, tq), F32), pltpu.VMEM((M_V_EXT, tq), F32)],
        compiler_params=_cparams("arbitrary", "arbitrary"),
        name="mla_attention",
    )(qt, qt, k, vt, mzt)


def _out_kernel(final, x_ref, o0_ref, o1_ref, o2_ref, l0_ref, l1_ref, l2_ref, az_ref, ymt_ref, mg_ref, gate_ref,
                wa_ref, wb_ref, wo_ref, fg_ref, out_ref, o1_sc, o2_sc, l1_sc, l2_sc):
    tm = x_ref.shape[0]
    nj = A_WIDTH // LANES
    for src, dst in ((o1_ref, o1_sc), (o2_ref, o2_sc), (l1_ref, l1_sc), (l2_ref, l2_sc)):
        r = src.shape[0]
        for p in range(r):
            for j in range(nj):
                dst[j, pl.ds(p, tm // r, stride=r), :] = src[p, :, j * LANES:(j + 1) * LANES]

    def rows(sc):
        return jnp.concatenate([sc[j] for j in range(nj)], axis=1)

    l0, l1, l2 = l0_ref[0], rows(l1_sc), rows(l2_sc)
    mx = jnp.maximum(jnp.maximum(l0, l1), l2)
    e0, e1, e2 = jnp.exp(l0 - mx), jnp.exp(l1 - mx), jnp.exp(l2 - mx)
    mix = (o0_ref[0] * e0 + rows(o1_sc) * e1 + rows(o2_sc) * e2) / (e0 + e1 + e2)
    az = az_ref[...]
    y_a = (mix * (az * _sigmoid(az))).astype(BF16)
    t_a = jnp.dot(y_a, wa_ref[...], preferred_element_type=F32)
    t_b = lax.dot_general(ymt_ref[...], wb_ref[...], (((0,), (0,)), ((), ())), preferred_element_type=F32)
    mg = mg_ref[...]
    g_a = _sigmoid(mg[:, :D_MODEL])
    g_m = _sigmoid(mg[:, D_MODEL:])
    merged = (g_a * t_a + g_m * t_b).astype(BF16)
    y = x_ref[...] + gate_ref[...] * jnp.dot(merged, wo_ref[...], preferred_element_type=F32)
    if final:
        y = y * lax.rsqrt(jnp.mean(y * y, axis=-1, keepdims=True) + EPS) * fg_ref[...]
    out_ref[...] = y


def _out_proj(x, o_groups, lse_groups, a_z, ymt, mg, gate, w_a, w_b, w_o, final_g, final):
    s = x.shape[0]
    tm = ROW_TILE
    row = lambda i: (i, 0)
    phase_specs = [pl.BlockSpec((r, tm // r, A_WIDTH), lambda i: (0, i, 0)) for _, r in A_PATTERNS]
    return pl.pallas_call(
        functools.partial(_out_kernel, final),
        grid=(s // tm,),
        in_specs=[pl.BlockSpec((tm, D_MODEL), row)] + phase_specs * 2
                 + [pl.BlockSpec((tm, A_WIDTH), row),
                    pl.BlockSpec((M_WIDTH, tm), lambda i: (0, i)),
                    pl.BlockSpec((tm, 2 * D_MODEL), row),
                    _const_spec((1, D_MODEL)),
                    _const_spec(w_a.shape), _const_spec(w_b.shape), _const_spec(w_o.shape),
                    _const_spec((1, D_MODEL))],
        out_specs=pl.BlockSpec((tm, D_MODEL), row),
        out_shape=jax.ShapeDtypeStruct((s, D_MODEL), F32),
        scratch_shapes=[pltpu.VMEM((A_WIDTH // LANES, tm, LANES), F32)] * 4,
        compiler_params=_cparams("parallel"),
        name="out_proj",
    )(x, *o_groups, *lse_groups, a_z, ymt, mg, gate, w_a, w_b, w_o, final_g)


def _prep_layer(w_in, w_uq, w_ukv, w_out_a, w_out_b, w_o):
    o_az = A_QKV
    o_cq = o_az + A_WIDTH
    o_ckv = o_cq + M_Q_LORA
    o_kr = o_ckv + M_KV_LORA
    o_mz = o_kr + M_ROPE
    o_mg = o_mz + M_WIDTH
    half = M_ROPE // 2
    w_qkv = w_in[:, :A_QKV].reshape(D_MODEL, 3, A_GROUPS, A_WIDTH).transpose(0, 2, 1, 3).reshape(D_MODEL, A_QKV)
    w_a = jnp.concatenate([w_qkv, w_in[:, o_az:o_cq]], axis=1).astype(BF16)
    w_g = w_in[:, o_mg:].astype(BF16)
    w_kr = w_in[:, o_kr:o_mz]
    z_lo = jnp.zeros((D_MODEL, M_NOPE), F32)
    z_hi = jnp.zeros((D_MODEL, M_QK_PAD - M_QK), F32)
    kr_a = jnp.concatenate([z_lo, w_kr, z_hi], axis=1)
    kr_b = jnp.concatenate([z_lo, w_kr[:, half:], w_kr[:, :half], z_hi], axis=1)
    w_c = jnp.concatenate([w_in[:, o_cq:o_kr], kr_a, kr_b], axis=1).astype(BF16)
    w_zt = w_in[:, o_mz:o_mg].T.astype(BF16)
    scale = M_QK ** -0.5 * math.log2(math.e)
    uq = (w_uq * scale).reshape(M_Q_LORA, M_HEADS, M_QK)
    uq = jnp.pad(uq, ((0, 0), (0, 0), (0, M_QK_PAD - M_QK)))
    w_uqt = uq.reshape(M_Q_LORA, M_HEADS * M_QK_PAD).T.astype(BF16)
    ukv = w_ukv.reshape(M_KV_LORA, M_HEADS, M_NOPE + M_V)
    w_k = jnp.pad(ukv[:, :, :M_NOPE], ((0, 0), (0, 0), (0, M_QK_PAD - M_NOPE)))
    w_k = w_k.reshape(M_KV_LORA, M_HEADS * M_QK_PAD).astype(BF16)
    w_vt = ukv[:, :, M_NOPE:].reshape(M_KV_LORA, M_WIDTH).T.astype(BF16)
    return (w_a, w_g, w_c, w_zt, w_uqt, w_k, w_vt,
            w_out_a.astype(BF16), w_out_b.astype(BF16), w_o.astype(BF16))


def _rope_tables(positions):
    half = M_ROPE // 2
    inv_freq = 1.0 / (ROPE_THETA ** (jnp.arange(0, M_ROPE, 2, dtype=F32) / M_ROPE))
    ang = positions.astype(F32)[:, None] * inv_freq
    cos, sin = jnp.cos(ang), jnp.sin(ang)
    s = positions.shape[0]
    z_lo = jnp.zeros((s, M_NOPE), F32)
    z_hi = jnp.zeros((s, M_QK_PAD - M_QK), F32)
    cos_k = jnp.concatenate([z_lo, cos, cos, z_hi], axis=1)
    sin_k = jnp.concatenate([z_lo, -sin, sin, z_hi], axis=1)
    return cos.T, sin.T, cos_k, sin_k


def kernel(x, c, positions, w_ada, b_ada, norm_g, w_in, q_norm_g, w_uq, kv_norm_g, w_ukv, w_out_a, w_out_b, w_o,
           rel_bias, final_norm_g):
    batch, s, _ = x.shape
    assert batch == 1 and s % (DIL_BLOCKS * A_BLOCK * A_PATTERNS[-1][1]) == 0
    assert s % MLA_Q_TILE == 0 and s % ROW_TILE == 0
    depth = w_ada.shape[0]
    xs = x[0]
    mod = _modulation(c, w_ada, b_ada)
    cos_t, sin_t, cos_k, sin_k = _rope_tables(positions[0])
    biases = [_dilated_bias(rel_bias, g, w, r) for g, (w, r) in enumerate(A_PATTERNS)]
    final_g = final_norm_g.reshape(1, D_MODEL)
    for l in range(depth):
        (w_a, w_g, w_c, w_zt, w_uqt, w_k, w_vt, wo_a, wo_b, wo) = _prep_layer(
            w_in[l], w_uq[l], w_ukv[l], w_out_a[l], w_out_b[l], w_o[l])
        shift, scale, gate = mod[l, 0], mod[l, 1], mod[l, 2]
        g = norm_g[l].reshape(1, D_MODEL)
        qkv0, qkv1, qkv2, a_z, mg = _in_proj_a(xs, g, scale, shift, w_a, w_g)
        qt, k, vt, mzt = _in_proj_m(xs, g, scale, shift, w_c, w_zt,
                                    q_norm_g[l].reshape(1, M_Q_LORA), kv_norm_g[l].reshape(1, M_KV_LORA),
                                    w_uqt, w_k, w_vt, cos_t, sin_t, cos_k, sin_k)
        outs = [_dilated_group(qkv, biases[gi], gi) for gi, qkv in enumerate((qkv0, qkv1, qkv2))]
        ymt = _mla_attention(qt, k, vt, mzt)
        xs = _out_proj(xs, [o for o, _ in outs], [ls for _, ls in outs], a_z, ymt, mg, gate,
                       wo_a, wo_b, wo, final_g, l == depth - 1)
    return xs[None]
```
